```python
import math
import jax, jax.numpy as jnp
from jax import lax
import numpy as np

D_MODEL = 1024
BATCH = 2
SEQ = 8192
DEPTH = 2
DEC_BATCH = 128
DEC_SEQ = 1
PAST_LEN = 2048
PAGE_SIZE = 128

A_HEADS = 4
A_HEAD_DIM = 64
A_VDIM = 2 * A_HEAD_DIM
A_QK_WIDTH = A_HEADS * 2 * A_HEAD_DIM
A_V_WIDTH = A_HEADS * A_VDIM
Q_BLOCK = 128
G_HEADS = 4
G_DK = 64
G_DV = 128
G_K_WIDTH = G_HEADS * G_DK
G_V_WIDTH = G_HEADS * G_DV
G_GATE_RANK = 16
G_TAU = 16.0
G_CHUNK = 64
N_EXPERTS = 256
TOP_K = 8
N_GROUPS = 8
TOPK_GROUPS = 4
D_EXPERT = 256
D_SHARED = 256
ROUTED_SCALE = 2.5
MOE_BLOCK = 128
DN_ALPHA = (2 * DEPTH) ** 0.25
DN_BETA = (8 * DEPTH) ** -0.25
LN_EPS = 1e-5
RMS_EPS = 1e-6
SPLIT_SIZES = (A_QK_WIDTH, A_QK_WIDTH, A_V_WIDTH, G_K_WIDTH, G_K_WIDTH, G_V_WIDTH, G_V_WIDTH, G_GATE_RANK, D_MODEL, D_MODEL)
SPLIT_POINTS = tuple(sum(SPLIT_SIZES[:i + 1]) for i in range(len(SPLIT_SIZES) - 1))
IN_WIDTH = sum(SPLIT_SIZES)
VALUE_SEGMENTS = (2, 5)

kernel_name = "hybrid_diffattn_gla_moe_step"

f32 = jnp.float32


def layer_norm(x, g, b):
    xf = x.astype(f32)
    mu = jnp.mean(xf, -1, keepdims=True)
    var = jnp.mean(jnp.square(xf - mu), -1, keepdims=True)
    return ((xf - mu) * lax.rsqrt(var + LN_EPS) * g + b).astype(x.dtype)


def rms_norm(x, g):
    xf = x.astype(f32)
    return (xf * lax.rsqrt(jnp.mean(xf * xf, -1, keepdims=True) + RMS_EPS) * g).astype(x.dtype)


def diff_combine(s, lam, mask, v):
    p = jax.nn.softmax(jnp.where(mask, s, -jnp.inf), axis=-1)
    a = p[:, :, 0] - lam * p[:, :, 1]
    return jnp.einsum('bhqk,bkhe->bqhe', a.astype(v.dtype), v)


def diff_attn_prompt(q, k, v, lam):
    B, S = q.shape[:2]
    nb = S // Q_BLOCK
    k5 = k.reshape(B, S, A_HEADS, 2, A_HEAD_DIM)
    qb = q.reshape(B, nb, Q_BLOCK, A_HEADS, 2, A_HEAD_DIM).transpose(1, 0, 2, 3, 4, 5)
    kpos = jnp.arange(S)
    scale = A_HEAD_DIM ** -0.5

    def block(args):
        qi, i = args
        s = jnp.einsum('bqhmd,bkhmd->bhmqk', qi, k5, preferred_element_type=f32) * scale
        qpos = i * Q_BLOCK + jnp.arange(Q_BLOCK)
        mask = kpos[None, :] <= qpos[:, None]
        return diff_combine(s, lam, mask, v)

    o = lax.map(block, (qb, jnp.arange(nb)))
    return o.transpose(1, 0, 2, 3, 4).reshape(B, S, A_HEADS, A_VDIM)


def diff_attn_sample(q, k, v, lam, k_cache, v_cache, page_table):
    Bd, L = q.shape[:2]
    kp = k_cache[page_table].reshape(Bd, -1, A_HEADS, 2 * A_HEAD_DIM)
    vp = v_cache[page_table].reshape(Bd, -1, A_HEADS, A_VDIM)
    P = kp.shape[1]
    kall = jnp.concatenate([kp.astype(k.dtype), k], axis=1).reshape(Bd, P + L, A_HEADS, 2, A_HEAD_DIM)
    vall = jnp.concatenate([vp.astype(v.dtype), v], axis=1)
    s = jnp.einsum('bqhmd,bkhmd->bhmqk', q, kall, preferred_element_type=f32) * (A_HEAD_DIM ** -0.5)
    mask = jnp.arange(P + L)[None, :] <= (P + jnp.arange(L))[:, None]
    return diff_combine(s, lam, mask, vall)


def gla_chunked(q, k, v, log_a, s0):
    B, L = q.shape[:2]
    c = math.gcd(L, G_CHUNK)
    n = L // c

    def to_chunks(t):
        return t.reshape(B, n, c, t.shape[2], t.shape[3]).transpose(1, 0, 3, 2, 4)

    qc, kc, vc, ac = to_chunks(q), to_chunks(k), to_chunks(v), to_chunks(log_a)
    causal = jnp.tril(jnp.ones((c, c), bool))[:, :, None]

    def step(S, inp):
        qi, ki, vi, ai = inp
        b = jnp.cumsum(ai.astype(f32), axis=2)
        diff = b[:, :, :, None, :] - b[:, :, None, :, :]
        decay = jnp.exp(jnp.where(causal, diff, -jnp.inf))
        att = jnp.einsum('bhid,bhjd,bhijd->bhij', qi.astype(f32), ki.astype(f32), decay)
        o = jnp.einsum('bhij,bhje->bhie', att, vi.astype(f32)) + \
            jnp.einsum('bhid,bhde->bhie', qi.astype(f32) * jnp.exp(b), S)
        b_last = b[:, :, -1, :]
        S_new = jnp.exp(b_last)[..., None] * S + \
            jnp.einsum('bhjd,bhje->bhde', ki.astype(f32) * jnp.exp(b_last[:, :, None, :] - b), vi.astype(f32))
        return S_new, o

    S, o = lax.scan(step, s0.astype(f32), (qc, kc, vc, ac))
    o = o.transpose(1, 0, 3, 2, 4).reshape(B, L, G_HEADS, G_DV)
    return o, S


def token_mix(x, attend, s0, lam, lam_init, w_in, w_a2, b_a, sub_g, gla_g, w_pa, w_pb, w_out):
    B, L, _ = x.shape
    u = jnp.einsum('bld,de->ble', x, w_in)
    q, k, v, gq, gk, gv, gr, glr, za, zb = jnp.split(u, SPLIT_POINTS, axis=-1)
    k_rows = k.reshape(B, L, A_HEADS, 2 * A_HEAD_DIM)
    v_rows = v.reshape(B, L, A_HEADS, A_VDIM)
    o_a = attend(q.reshape(B, L, A_HEADS, 2, A_HEAD_DIM), k_rows, v_rows, lam)
    o_a = rms_norm(o_a, sub_g) * (1.0 - lam_init)
    y_a = jnp.einsum('ble,ed->bld', o_a.reshape(B, L, A_V_WIDTH), w_pa)
    log_a = jax.nn.log_sigmoid((jnp.einsum('blr,rk->blk', glr, w_a2) + b_a).astype(f32)) / G_TAU
    o_g, s_new = gla_chunked((gq * (G_DK ** -0.5)).reshape(B, L, G_HEADS, G_DK),
                             gk.reshape(B, L, G_HEADS, G_DK),
                             gv.reshape(B, L, G_HEADS, G_DV),
                             log_a.reshape(B, L, G_HEADS, G_DK), s0)
    o_g = rms_norm(o_g.astype(x.dtype), gla_g).reshape(B, L, G_V_WIDTH) * jax.nn.silu(gr)
    y_b = jnp.einsum('ble,ed->bld', o_g, w_pb)
    merged = jax.nn.sigmoid(za) * y_a + jax.nn.sigmoid(zb) * y_b
    return jnp.einsum('bld,de->ble', merged, w_out), k_rows, v_rows, s_new


def moe(x, w_router, b_router, w_gate, w_up, w_down, ws_gate, ws_up, ws_down):
    shp = x.shape
    t = x.reshape(-1, D_MODEL)
    T = t.shape[0]
    scores = jax.nn.sigmoid(jnp.einsum('td,de->te', t, w_router, preferred_element_type=f32))
    biased = scores + b_router.astype(f32)
    grp_score = lax.top_k(biased.reshape(T, N_GROUPS, N_EXPERTS // N_GROUPS), 2)[0].sum(-1)
    _, top_groups = lax.top_k(grp_score, TOPK_GROUPS)
    gmask = jnp.any(top_groups[..., None] == jnp.arange(N_GROUPS), axis=1)
    masked = jnp.where(jnp.repeat(gmask, N_EXPERTS // N_GROUPS, axis=1), biased, -jnp.inf)
    _, eidx = lax.top_k(masked, TOP_K)
    s_sel = jnp.take_along_axis(scores, eidx, axis=1)
    gates = s_sel / jnp.sum(s_sel, -1, keepdims=True) * ROUTED_SCALE
    n = T * TOP_K
    blk = min(MOE_BLOCK, max(8, 1 << max(0, (n // N_EXPERTS).bit_length() - 1)))
    e_flat = eidx.reshape(-1)
    tok_flat = (jnp.arange(n) // TOP_K).astype(jnp.int32)
    g_flat = gates.reshape(-1)
    order = jnp.argsort(e_flat)
    e_sorted = e_flat[order]
    counts = jnp.bincount(e_flat, length=N_EXPERTS)
    padded = (counts + blk - 1) // blk * blk
    pad_end = jnp.cumsum(padded)
    pad_start = pad_end - padded
    start = jnp.cumsum(counts) - counts
    dest = pad_start[e_sorted] + jnp.arange(n) - start[e_sorted]
    n_rows = -(-(n + N_EXPERTS * (blk - 1)) // blk) * blk
    n_blocks = n_rows // blk
    row_tok = jnp.full((n_rows,), T, jnp.int32).at[dest].set(tok_flat[order])
    row_gate = jnp.zeros((n_rows,), f32).at[dest].set(g_flat[order])
    blk_exp = jnp.minimum(jnp.searchsorted(pad_end, jnp.arange(n_blocks) * blk, side='right'), N_EXPERTS - 1)
    t_pad = jnp.concatenate([t, jnp.zeros((1, D_MODEL), t.dtype)], axis=0)

    def body(acc, inp):
        rows, e, g = inp
        xe = t_pad[rows]
        h = jax.nn.silu(xe @ w_gate[e]) * (xe @ w_up[e])
        y = (h @ w_down[e]).astype(f32) * g[:, None]
        return acc.at[rows].add(y), None

    acc, _ = lax.scan(body, jnp.zeros((T + 1, D_MODEL), f32),
                      (row_tok.reshape(n_blocks, blk), blk_exp, row_gate.reshape(n_blocks, blk)))
    shared = (jax.nn.silu(t @ ws_gate) * (t @ ws_up)) @ ws_down
    return (acc[:T] + shared.astype(f32)).astype(x.dtype).reshape(shp)


def setup_inputs(seed: int = 0) -> dict:
    key = jax.random.key(seed)
    ks = jax.random.split(key, 40)
    n_pages = PAST_LEN // PAGE_SIZE
    n_pool = (DEC_BATCH * n_pages * 5) // 4

    def nrm(k, shape, scale):
        return jax.random.normal(k, shape, f32) * scale

    col_scale = jnp.concatenate([jnp.full((s,), DN_BETA if i in VALUE_SEGMENTS else 1.0, f32)
                                 for i, s in enumerate(SPLIT_SIZES)])
    return {
        "x_prompt": nrm(ks[0], (BATCH, SEQ, D_MODEL), 1.0),
        "x_sample": nrm(ks[1], (DEC_BATCH, DEC_SEQ, D_MODEL), 1.0),
        "cache_k": nrm(ks[2], (DEPTH, n_pool, PAGE_SIZE, A_HEADS, 2 * A_HEAD_DIM), 1.0),
        "cache_v": nrm(ks[3], (DEPTH, n_pool, PAGE_SIZE, A_HEADS, A_VDIM), DN_BETA),
        "state_gla": nrm(ks[4], (DEPTH, DEC_BATCH, G_HEADS, G_DK, G_DV), 0.1),
        "page_table": jax.random.permutation(ks[5], n_pool)[:DEC_BATCH * n_pages]
                         .reshape(DEC_BATCH, n_pages).astype(jnp.int32),
        "w_in": nrm(ks[6], (DEPTH, D_MODEL, IN_WIDTH), D_MODEL ** -0.5) * col_scale,
        "w_a2": nrm(ks[7], (DEPTH, G_GATE_RANK, G_K_WIDTH), G_GATE_RANK ** -0.5),
        "b_a": nrm(ks[8], (DEPTH, G_K_WIDTH), 0.1),
        "lam_q1": nrm(ks[9], (DEPTH, A_HEAD_DIM), 0.1),
        "lam_k1": nrm(ks[10], (DEPTH, A_HEAD_DIM), 0.1),
        "lam_q2": nrm(ks[11], (DEPTH, A_HEAD_DIM), 0.1),
        "lam_k2": nrm(ks[12], (DEPTH, A_HEAD_DIM), 0.1),
        "sub_g": 1.0 + nrm(ks[13], (DEPTH, A_VDIM), 0.02),
        "gla_g": 1.0 + nrm(ks[14], (DEPTH, G_DV), 0.02),
        "w_pa": nrm(ks[15], (DEPTH, A_V_WIDTH, D_MODEL), A_V_WIDTH ** -0.5 * DN_BETA),
        "w_pb": nrm(ks[16], (DEPTH, G_V_WIDTH, D_MODEL), G_V_WIDTH ** -0.5 * DN_BETA),
        "w_out": nrm(ks[17], (DEPTH, D_MODEL, D_MODEL), D_MODEL ** -0.5 * DN_BETA),
        "ln1_g": 1.0 + nrm(ks[18], (DEPTH, D_MODEL), 0.02),
        "ln1_b": nrm(ks[19], (DEPTH, D_MODEL), 0.02),
        "w_router": nrm(ks[20], (DEPTH, D_MODEL, N_EXPERTS), D_MODEL ** -0.5),
        "b_router": nrm(ks[21], (DEPTH, N_EXPERTS), 0.01),
        "w_gate": nrm(ks[22], (DEPTH, N_EXPERTS, D_MODEL, D_EXPERT), D_MODEL ** -0.5),
        "w_up": nrm(ks[23], (DEPTH, N_EXPERTS, D_MODEL, D_EXPERT), D_MODEL ** -0.5 * DN_BETA),
        "w_down": nrm(ks[24], (DEPTH, N_EXPERTS, D_EXPERT, D_MODEL), D_EXPERT ** -0.5 * DN_BETA),
        "ws_gate": nrm(ks[25], (DEPTH, D_MODEL, D_SHARED), D_MODEL ** -0.5),
        "ws_up": nrm(ks[26], (DEPTH, D_MODEL, D_SHARED), D_MODEL ** -0.5 * DN_BETA),
        "ws_down": nrm(ks[27], (DEPTH, D_SHARED, D_MODEL), D_SHARED ** -0.5 * DN_BETA),
        "ln2_g": 1.0 + nrm(ks[28], (DEPTH, D_MODEL), 0.02),
        "ln2_b": nrm(ks[29], (DEPTH, D_MODEL), 0.02),
    }


def reference(x_prompt, x_sample, cache_k, cache_v, state_gla, page_table,
              w_in, w_a2, b_a, lam_q1, lam_k1, lam_q2, lam_k2, sub_g, gla_g, w_pa, w_pb, w_out,
              ln1_g, ln1_b, w_router, b_router, w_gate, w_up, w_down, ws_gate, ws_up, ws_down,
              ln2_g, ln2_b):
    xp, xs = x_prompt, x_sample
    kp_l, vp_l, ks_l, vs_l, sp_l, ss_l = [], [], [], [], [], []
    for l in range(DEPTH):
        lam_init = 0.8 - 0.6 * math.exp(-0.3 * l)
        lam = (jnp.exp(jnp.sum(lam_q1[l].astype(f32) * lam_k1[l].astype(f32)))
               - jnp.exp(jnp.sum(lam_q2[l].astype(f32) * lam_k2[l].astype(f32))) + lam_init)
        mix_w = (w_in[l], w_a2[l], b_a[l], sub_g[l], gla_g[l], w_pa[l], w_pb[l], w_out[l])
        moe_w = (w_router[l], b_router[l], w_gate[l], w_up[l], w_down[l], ws_gate[l], ws_up[l], ws_down[l])
        ck, cv = cache_k[l], cache_v[l]

        s0 = jnp.zeros((xp.shape[0], G_HEADS, G_DK, G_DV), f32)
        m_p, k_p, v_p, s_p = token_mix(xp, diff_attn_prompt, s0, lam, lam_init, *mix_w)
        xp = layer_norm(DN_ALPHA * xp + m_p, ln1_g[l], ln1_b[l])
        xp = layer_norm(DN_ALPHA * xp + moe(xp, *moe_w), ln2_g[l], ln2_b[l])

        attend_s = lambda q, k, v, lm, ck=ck, cv=cv: diff_attn_sample(q, k, v, lm, ck, cv, page_table)
        m_s, k_s, v_s, s_s = token_mix(xs, attend_s, state_gla[l], lam, lam_init, *mix_w)
        xs = layer_norm(DN_ALPHA * xs + m_s, ln1_g[l], ln1_b[l])
        xs = layer_norm(DN_ALPHA * xs + moe(xs, *moe_w), ln2_g[l], ln2_b[l])

        Bp, Sp = k_p.shape[:2]
        kp_l.append(k_p.reshape(Bp, Sp // PAGE_SIZE, PAGE_SIZE, A_HEADS, 2 * A_HEAD_DIM))
        vp_l.append(v_p.reshape(Bp, Sp // PAGE_SIZE, PAGE_SIZE, A_HEADS, A_VDIM))
        ks_l.append(k_s)
        vs_l.append(v_s)
        sp_l.append(s_p)
        ss_l.append(s_s)
    k_prompt = jnp.stack(kp_l)
    v_prompt = jnp.stack(vp_l)
    k_sample = jnp.stack(ks_l)
    v_sample = jnp.stack(vs_l)
    s_prompt = jnp.stack(sp_l)
    s_sample = jnp.stack(ss_l)
    return (xp, xs, k_prompt, v_prompt, k_sample, v_sample, s_prompt, s_sample)
```

```python
import functools
import math

import jax
import jax.numpy as jnp
from jax import lax
from jax.experimental import pallas as pl
from jax.experimental.pallas import tpu as pltpu

f32 = jnp.float32
bf16 = jnp.bfloat16
u32 = jnp.uint32
i32 = jnp.int32

D_MODEL = 1024
A_HEADS = 4
A_HEAD_DIM = 64
A_VDIM = 128
A_WIDTH = A_HEADS * A_VDIM
G_HEADS = 4
G_DK = 64
G_DV = 128
G_K_WIDTH = G_HEADS * G_DK
G_V_WIDTH = G_HEADS * G_DV
G_GATE_RANK = 16
G_TAU = 16.0
N_EXPERTS = 256
TOP_K = 8
N_GROUPS = 8
TOPK_GROUPS = 4
D_EXPERT = 256
ROUTED_SCALE = 2.5
PAGE_SIZE = 128
LN_EPS = 1e-5
RMS_EPS = 1e-6

LANES = 128
VMEM_LIMIT = 56 * 1024 * 1024

NEG_BIG = -1e30
HALF = D_MODEL // 2
EXPERT_BLOCK = 256
GLA_CHUNK = 64

C_Q, C_K, C_V, C_GQ, C_GK, C_GV, C_GR, C_ZA, C_ZB, C_GLR, C_END = (
    0, 512, 1024, 1536, 1792, 2048, 2560, 3072, 4096, 5120, 5248)


def _params(sem, vmem=VMEM_LIMIT):
    return pltpu.CompilerParams(dimension_semantics=sem, vmem_limit_bytes=vmem)


def _row_tile(n, cands=(512, 384, 256, 128, 64, 32, 16, 8)):
    for c in cands:
        if n % c == 0:
            return c
    raise ValueError(f"no row tile for {n}")


def _sigmoid(x):
    return 1.0 / (1.0 + jnp.exp(-x))


def _pack_rows(x):
    lo = lax.bitcast_convert_type(x[:, :HALF].astype(bf16).astype(f32), u32) >> 16
    hi = lax.bitcast_convert_type(x[:, HALF:].astype(bf16).astype(f32), u32) & jnp.uint32(0xFFFF0000)
    return lo | hi


def _unpack_rows(w):
    lo = lax.bitcast_convert_type(w << 16, f32)
    hi = lax.bitcast_convert_type(w & jnp.uint32(0xFFFF0000), f32)
    return lo, hi


def _layer_norm(h, g, b):
    mu = jnp.mean(h, axis=-1, keepdims=True)
    d = h - mu
    var = jnp.mean(d * d, axis=-1, keepdims=True)
    return d * lax.rsqrt(var + LN_EPS) * g + b


def _rms_norm(o, g):
    return o * lax.rsqrt(jnp.mean(o * o, axis=-1, keepdims=True) + RMS_EPS) * g


def _lam_value(lam_ref, lam_init):
    l = lam_ref[...]
    s1 = jnp.sum(l[0:1] * l[1:2], axis=-1, keepdims=True)
    s2 = jnp.sum(l[2:3] * l[3:4], axis=-1, keepdims=True)
    return jnp.exp(s1) - jnp.exp(s2) + lam_init


def _inproj_body(x_ref, w_ref, q_o, kf_o, kb_o, vf_o, vb_o, gq_o, gk_o, gv_o, gr_o, za_o, zb_o, glr_o):
    xb = x_ref[...].astype(bf16)

    def mm(c0, c1):
        return jnp.dot(xb, w_ref[:, c0:c1], preferred_element_type=f32)

    q_o[...] = (mm(C_Q, C_K) * (A_HEAD_DIM ** -0.5)).astype(bf16)
    k = mm(C_K, C_V)
    kf_o[...] = k
    kb_o[...] = k.astype(bf16)
    v = mm(C_V, C_GQ)
    vf_o[...] = v
    vb_o[...] = v.astype(bf16)
    gq_o[...] = mm(C_GQ, C_GK) * (G_DK ** -0.5)
    gk_o[...] = mm(C_GK, C_GV)
    gv_o[...] = mm(C_GV, C_GR)
    gr_o[...] = mm(C_GR, C_ZA)
    za_o[...] = mm(C_ZA, C_ZB).astype(bf16)
    zb_o[...] = mm(C_ZB, C_GLR).astype(bf16)
    glr_o[...] = mm(C_GLR, C_END)


def _inproj(x, w):
    t = x.shape[0]
    tm = _row_tile(t, (384, 256, 128, 64, 32, 16, 8))
    widths = [(512, bf16), (512, f32), (512, bf16), (512, f32), (512, bf16), (256, f32), (256, f32),
              (512, f32), (512, f32), (1024, bf16), (1024, bf16), (LANES, f32)]
    return pl.pallas_call(
        _inproj_body,
        grid=(t // tm,),
        in_specs=[pl.BlockSpec((tm, D_MODEL), lambda i: (i, 0)),
                  pl.BlockSpec((D_MODEL, C_END), lambda i: (0, 0))],
        out_specs=[pl.BlockSpec((tm, wd), lambda i: (i, 0)) for wd, _ in widths],
        out_shape=[jax.ShapeDtypeStruct((t, wd), dt) for wd, dt in widths],
        compiler_params=_params(("parallel",)),
    )(x, w)


def _attn_body(qi_tab, kj_tab, diag_tab, last_tab, q_ref, k_ref, v_ref, lam_ref, subg_ref, o_ref,
               m1, l1, a1, m2, l2, a2, *, tq, tk, lam_init):
    p = pl.program_id(2)
    qi = qi_tab[p]
    kj = kj_tab[p]

    @pl.when(kj == 0)
    def _():
        for m, l, a in ((m1, l1, a1), (m2, l2, a2)):
            m[...] = jnp.full(m.shape, NEG_BIG, f32)
            l[...] = jnp.zeros(l.shape, f32)
            a[...] = jnp.zeros(a.shape, f32)

    q = q_ref[...]
    k = k_ref[...]
    v = v_ref[...]
    lane = lax.broadcasted_iota(i32, (1, LANES), 1)
    zero = jnp.zeros_like(q)
    q1 = jnp.where(lane < A_HEAD_DIM, q, zero)
    q2 = jnp.where(lane >= A_HEAD_DIM, q, zero)

    def step(masked):
        for qm, m, l, a in ((q1, m1, l1, a1), (q2, m2, l2, a2)):
            s = lax.dot_general(qm, k, (((1,), (1,)), ((), ())), preferred_element_type=f32)
            if masked:
                qpos = qi * tq + lax.broadcasted_iota(i32, (tq, tk), 0)
                kpos = kj * tk + lax.broadcasted_iota(i32, (tq, tk), 1)
                s = jnp.where(kpos <= qpos, s, NEG_BIG)
            m_prev = m[...]
            m_new = jnp.maximum(m_prev, jnp.max(s, axis=-1, keepdims=True))
            alpha = jnp.exp(m_prev - m_new)
            pr = jnp.exp(s - m_new)
            l[...] = alpha * l[...] + jnp.sum(pr, axis=-1, keepdims=True)
            a[...] = alpha * a[...] + jnp.dot(pr.astype(bf16), v, preferred_element_type=f32)
            m[...] = m_new

    @pl.when(diag_tab[p] == 1)
    def _():
        step(True)

    @pl.when(diag_tab[p] == 0)
    def _():
        step(False)

    @pl.when(last_tab[p] == 1)
    def _():
        lam = _lam_value(lam_ref, lam_init)
        o = a1[...] / l1[...] - lam * (a2[...] / l2[...])
        o_ref[...] = (_rms_norm(o, subg_ref[...]) * (1.0 - lam_init)).astype(o_ref.dtype)


def _attn_prompt(q, k, v, lam_vecs, sub_g, batch, seq, lam_init):
    tq = min(512, seq)
    tk = min(512, seq)
    nq, nk = seq // tq, seq // tk
    qi_l, kj_l, dg_l, ls_l = [], [], [], []
    for qi in range(nq):
        last = ((qi + 1) * tq - 1) // tk
        for kj in range(last + 1):
            qi_l.append(qi)
            kj_l.append(kj)
            dg_l.append(1 if (kj + 1) * tk - 1 > qi * tq else 0)
            ls_l.append(1 if kj == last else 0)
    tabs = [jnp.asarray(t, i32) for t in (qi_l, kj_l, dg_l, ls_l)]
    n_pairs = len(qi_l)
    body = functools.partial(_attn_body, tq=tq, tk=tk, lam_init=lam_init)
    grid_spec = pltpu.PrefetchScalarGridSpec(
        num_scalar_prefetch=4,
        grid=(batch, A_HEADS, n_pairs),
        in_specs=[
            pl.BlockSpec((tq, LANES), lambda b, h, p, qt, kt, dt, lt: (b * nq + qt[p], h)),
            pl.BlockSpec((tk, LANES), lambda b, h, p, qt, kt, dt, lt: (b * nk + kt[p], h)),
            pl.BlockSpec((tk, LANES), lambda b, h, p, qt, kt, dt, lt: (b * nk + kt[p], h)),
            pl.BlockSpec((4, A_HEAD_DIM), lambda b, h, p, *_: (0, 0)),
            pl.BlockSpec((1, A_VDIM), lambda b, h, p, *_: (0, 0)),
        ],
        out_specs=pl.BlockSpec((tq, LANES), lambda b, h, p, qt, kt, dt, lt: (b * nq + qt[p], h)),
        scratch_shapes=[pltpu.VMEM((tq, 1), f32), pltpu.VMEM((tq, 1), f32), pltpu.VMEM((tq, A_VDIM), f32),
                        pltpu.VMEM((tq, 1), f32), pltpu.VMEM((tq, 1), f32), pltpu.VMEM((tq, A_VDIM), f32)],
    )
    return pl.pallas_call(
        body,
        grid_spec=grid_spec,
        out_shape=jax.ShapeDtypeStruct((batch * seq, A_WIDTH), bf16),
        compiler_params=_params(("parallel", "parallel", "arbitrary")),
    )(*tabs, q, k, v, lam_vecs, sub_g)


def _decode_body(pt_ref, q_ref, kn_ref, vn_ref, lam_ref, subg_ref, *refs, n_pages, lam_init):
    k_refs = refs[:n_pages]
    v_refs = refs[n_pages:2 * n_pages]
    o_ref = refs[2 * n_pages]
    kbuf, vbuf = refs[2 * n_pages + 1:]
    rows_pg = PAGE_SIZE * A_HEADS
    past = n_pages * rows_pg
    tail = 16
    n_col = past + tail

    for p in range(n_pages):
        kbuf[p * rows_pg:(p + 1) * rows_pg, :] = k_refs[p][...].astype(bf16)
        vbuf[p * rows_pg:(p + 1) * rows_pg, :] = v_refs[p][...].astype(bf16)
    kbuf[past:, :] = kn_ref[0].astype(bf16)
    vbuf[past:, :] = vn_ref[0].astype(bf16)

    qrow = q_ref[0].astype(f32)
    row = lax.broadcasted_iota(i32, (16, LANES), 0)
    lane = lax.broadcasted_iota(i32, (16, LANES), 1)
    qmat = jnp.zeros((16, LANES), f32)
    for h in range(A_HEADS):
        qh = jnp.broadcast_to(qrow[:, h * LANES:(h + 1) * LANES], (16, LANES))
        sel = ((row >> 1) == h) & ((lane >= A_HEAD_DIM) == ((row & 1) == 1))
        qmat = jnp.where(sel, qh, qmat)
    s = lax.dot_general(qmat.astype(bf16), kbuf[...], (((1,), (1,)), ((), ())), preferred_element_type=f32)
    srow = lax.broadcasted_iota(i32, (16, n_col), 0)
    scol = lax.broadcasted_iota(i32, (16, n_col), 1)
    valid = ((scol & (A_HEADS - 1)) == (srow >> 1)) & (scol < past + A_HEADS) & (srow < 2 * A_HEADS)
    s = jnp.where(valid, s, NEG_BIG)
    m = jnp.max(s, axis=-1, keepdims=True)
    pr = jnp.where(valid, jnp.exp(s - m), 0.0)
    den = jnp.maximum(jnp.sum(pr, axis=-1, keepdims=True), 1e-30)
    pn = pr / den
    o8 = jnp.dot(pn.astype(bf16), vbuf[...], preferred_element_type=f32)
    lam = _lam_value(lam_ref, lam_init)
    for h in range(A_HEADS):
        o = o8[2 * h:2 * h + 1, :] - lam * o8[2 * h + 1:2 * h + 2, :]
        o_ref[0, :, h * LANES:(h + 1) * LANES] = (
            _rms_norm(o, subg_ref[...]) * (1.0 - lam_init)).astype(o_ref.dtype)


def _attn_decode(q_s, k_new, v_new, lam_vecs, sub_g, cache_k4, cache_v4, page_table, layer, lam_init):
    bd, n_pages = page_table.shape
    rows_pg = PAGE_SIZE * A_HEADS
    body = functools.partial(_decode_body, n_pages=n_pages, lam_init=lam_init)

    def page_spec(p):
        return pl.BlockSpec((None, None, rows_pg, LANES),
                            lambda b, pt, p=p: (layer, pt[b * n_pages + p], 0, 0))

    grid_spec = pltpu.PrefetchScalarGridSpec(
        num_scalar_prefetch=1,
        grid=(bd,),
        in_specs=[pl.BlockSpec((1, 1, A_WIDTH), lambda b, pt: (b, 0, 0)),
                  pl.BlockSpec((1, 16, LANES), lambda b, pt: (b, 0, 0)),
                  pl.BlockSpec((1, 16, LANES), lambda b, pt: (b, 0, 0)),
                  pl.BlockSpec((4, A_HEAD_DIM), lambda b, pt: (0, 0)),
                  pl.BlockSpec((1, A_VDIM), lambda b, pt: (0, 0))]
        + [page_spec(p) for p in range(n_pages)] * 2,
        out_specs=pl.BlockSpec((1, 1, A_WIDTH), lambda b, pt: (b, 0, 0)),
        scratch_shapes=[pltpu.VMEM((n_pages * rows_pg + 16, LANES), bf16),
                        pltpu.VMEM((n_pages * rows_pg + 16, LANES), bf16)],
    )
    return pl.pallas_call(
        body,
        grid_spec=grid_spec,
        out_shape=jax.ShapeDtypeStruct((bd, 1, A_WIDTH), bf16),
        compiler_params=_params(("arbitrary",)),
    )(page_table.reshape(-1), q_s, k_new, v_new, lam_vecs, sub_g,
      *([cache_k4] * n_pages), *([cache_v4] * n_pages))


def _log_decay(glr, wa2, ba):
    z = jnp.dot(glr, wa2, preferred_element_type=f32, precision=lax.Precision.HIGHEST) + ba
    return (jnp.minimum(z, 0.0) - jnp.log(1.0 + jnp.exp(-jnp.abs(z)))) * (1.0 / G_TAU)


def _gla_finish(o, gr, g):
    return _rms_norm(o, g) * (gr * _sigmoid(gr))


def _gla_prompt_body(q_ref, k_ref, v_ref, glr_ref, gr_ref, wa2_ref, ba_ref, g_ref, o_ref, s_ref, state,
                     *, c):
    ci = pl.program_id(1)

    @pl.when(ci == 0)
    def _():
        state[...] = jnp.zeros(state.shape, f32)

    la = _log_decay(glr_ref[...], wa2_ref[...], ba_ref[...])
    ri = lax.broadcasted_iota(i32, (c, c), 0)
    cj = lax.broadcasted_iota(i32, (c, c), 1)
    tril = ri >= cj
    b = jnp.dot(tril.astype(f32), la, preferred_element_type=f32, precision=lax.Precision.HIGHEST)
    b_last = b[c - 1:c, :]
    q = q_ref[...]
    k = k_ref[...]
    qd = q * jnp.exp(b)
    kinv = (k * jnp.exp(-b)).astype(bf16)
    kd = k * jnp.exp(b_last - b)
    kdt = kd.T.astype(bf16)
    tot = lax.dot_general(la.T, jnp.ones((c, LANES), f32), (((1,), (0,)), ((), ())),
                          preferred_element_type=f32, precision=lax.Precision.HIGHEST)
    dec_rows = jnp.exp(tot)
    lane = lax.broadcasted_iota(i32, (1, G_K_WIDTH), 1)
    st = state[...]
    stb = st.astype(bf16)
    upd = jnp.dot(kdt, v_ref[...].astype(bf16), preferred_element_type=f32)
    for h in range(G_HEADS):
        qh = jnp.where((lane >> 6) == h, qd, 0.0).astype(bf16)
        att = lax.dot_general(qh, kinv, (((1,), (1,)), ((), ())), preferred_element_type=f32)
        att = jnp.where(tril, att, 0.0)
        vh = v_ref[:, h * G_DV:(h + 1) * G_DV].astype(bf16)
        o = jnp.dot(att.astype(bf16), vh, preferred_element_type=f32)
        o = o + jnp.dot(qh, stb, preferred_element_type=f32)
        o_ref[:, h * G_DV:(h + 1) * G_DV] = _gla_finish(
            o, gr_ref[:, h * G_DV:(h + 1) * G_DV], g_ref[...]).astype(o_ref.dtype)
        r0 = h * G_DK
        state[r0:r0 + G_DK, :] = (dec_rows[r0:r0 + G_DK, :] * st[r0:r0 + G_DK, :]
                                  + upd[r0:r0 + G_DK, h * G_DV:(h + 1) * G_DV])

    @pl.when(ci == pl.num_programs(1) - 1)
    def _():
        s_ref[0] = state[...]


def _gla_prompt(gq, gk, gv, glr, gr, wa2, ba, gla_g, batch, seq):
    c = math.gcd(seq, GLA_CHUNK)
    n = seq // c
    body = functools.partial(_gla_prompt_body, c=c)

    def tok(width):
        return pl.BlockSpec((c, width), lambda b, i: (b * n + i, 0))

    def whole(shape):
        return pl.BlockSpec(shape, lambda b, i: (0,) * len(shape))

    return pl.pallas_call(
        body,
        grid=(batch, n),
        in_specs=[tok(G_K_WIDTH), tok(G_K_WIDTH), tok(G_V_WIDTH), tok(LANES), tok(G_V_WIDTH),
                  whole((LANES, G_K_WIDTH)), whole((1, G_K_WIDTH)), whole((1, G_DV))],
        out_specs=[tok(G_V_WIDTH),
                   pl.BlockSpec((1, G_K_WIDTH, G_DV), lambda b, i: (b, 0, 0))],
        out_shape=[jax.ShapeDtypeStruct((batch * seq, G_V_WIDTH), bf16),
                   jax.ShapeDtypeStruct((batch, G_K_WIDTH, G_DV), f32)],
        scratch_shapes=[pltpu.VMEM((G_K_WIDTH, G_DV), f32)],
        compiler_params=_params(("parallel", "arbitrary")),
    )(gq, gk, gv, glr, gr, wa2, ba, gla_g)


def _gla_sample_body(q_ref, k_ref, v_ref, glr_ref, gr_ref, s_ref, wa2_ref, ba_ref, g_ref, o_ref, sn_ref):
    la = _log_decay(glr_ref[0], wa2_ref[...], ba_ref[...])
    rows = jnp.concatenate([jnp.broadcast_to(jnp.exp(la), (G_DK, G_K_WIDTH)),
                            jnp.broadcast_to(k_ref[0], (G_DK, G_K_WIDTH)),
                            jnp.broadcast_to(q_ref[0], (G_DK, G_K_WIDTH))], axis=0)
    ri = lax.broadcasted_iota(i32, (3 * G_DK, G_K_WIDTH), 0)
    li = lax.broadcasted_iota(i32, (3 * G_DK, G_K_WIDTH), 1)
    picked = jnp.where((ri & (G_DK - 1)) == (li & (G_DK - 1)), rows, 0.0)
    si = lax.broadcasted_iota(i32, (G_K_WIDTH, G_V_WIDTH), 0)
    sj = lax.broadcasted_iota(i32, (G_K_WIDTH, G_V_WIDTH), 1)
    seg = ((si >> 6) == (sj >> 7)).astype(f32)
    cols = jnp.dot(picked, seg, preferred_element_type=f32, precision=lax.Precision.HIGHEST)
    v = v_ref[0]
    for h in range(G_HEADS):
        sl = slice(h * G_DV, (h + 1) * G_DV)
        a_c = cols[0:G_DK, sl]
        k_c = cols[G_DK:2 * G_DK, sl]
        q_c = cols[2 * G_DK:3 * G_DK, sl]
        s_new = a_c * s_ref[0, h] + k_c * v[:, sl]
        sn_ref[0, h] = s_new
        o = jnp.sum(q_c * s_new, axis=0, keepdims=True)
        o_ref[0, :, sl] = _gla_finish(o, gr_ref[0][:, sl], g_ref[...]).astype(o_ref.dtype)


def _gla_sample(gq, gk, gv, glr, gr, s0, wa2, ba, gla_g):
    bd = gq.shape[0]

    def vec(width):
        return pl.BlockSpec((1, 1, width), lambda b: (b, 0, 0))

    def whole(shape):
        return pl.BlockSpec(shape, lambda b: (0,) * len(shape))

    st = pl.BlockSpec((1, G_HEADS, G_DK, G_DV), lambda b: (b, 0, 0, 0))
    return pl.pallas_call(
        _gla_sample_body,
        grid=(bd,),
        in_specs=[vec(G_K_WIDTH), vec(G_K_WIDTH), vec(G_V_WIDTH), vec(LANES), vec(G_V_WIDTH), st,
                  whole((LANES, G_K_WIDTH)), whole((1, G_K_WIDTH)), whole((1, G_DV))],
        out_specs=[vec(G_V_WIDTH), st],
        out_shape=[jax.ShapeDtypeStruct((bd, 1, G_V_WIDTH), bf16),
                   jax.ShapeDtypeStruct((bd, G_HEADS, G_DK, G_DV), f32)],
        compiler_params=_params(("parallel",)),
    )(gq, gk, gv, glr, gr, s0, wa2, ba, gla_g)


def _postmix_body(oa_ref, og_ref, za_ref, zb_ref, x_ref, wpa_ref, wpb_ref, wout_ref, g_ref, b_ref, wr_ref,
                  x1_o, xp_o, sc_o, *, alpha):
    ya = jnp.dot(oa_ref[...], wpa_ref[...], preferred_element_type=f32)
    yb = jnp.dot(og_ref[...], wpb_ref[...], preferred_element_type=f32)
    merged = _sigmoid(za_ref[...].astype(f32)) * ya + _sigmoid(zb_ref[...].astype(f32)) * yb
    mix = jnp.dot(merged.astype(bf16), wout_ref[...], preferred_element_type=f32)
    x1 = _layer_norm(alpha * x_ref[...] + mix, g_ref[...], b_ref[...])
    x1_o[...] = x1
    xp_o[...] = _pack_rows(x1)
    logits = jnp.dot(x1, wr_ref[...], preferred_element_type=f32, precision=lax.Precision.HIGHEST)
    sc_o[...] = _sigmoid(logits)


def _postmix(oa, og, za, zb, x, wpa, wpb, wout, g, b, wr, alpha):
    t = x.shape[0]
    tm = _row_tile(t, (384, 256, 128, 64, 32, 16, 8))

    def tok(width):
        return pl.BlockSpec((tm, width), lambda i: (i, 0))

    def whole(shape):
        return pl.BlockSpec(shape, lambda i: (0,) * len(shape))

    return pl.pallas_call(
        functools.partial(_postmix_body, alpha=alpha),
        grid=(t // tm,),
        in_specs=[tok(A_WIDTH), tok(G_V_WIDTH), tok(D_MODEL), tok(D_MODEL), tok(D_MODEL),
                  whole((A_WIDTH, D_MODEL)), whole((G_V_WIDTH, D_MODEL)), whole((D_MODEL, D_MODEL)),
                  whole((1, D_MODEL)), whole((1, D_MODEL)), whole((D_MODEL, N_EXPERTS))],
        out_specs=[tok(D_MODEL), tok(HALF), tok(N_EXPERTS)],
        out_shape=[jax.ShapeDtypeStruct((t, D_MODEL), f32),
                   jax.ShapeDtypeStruct((t, HALF), u32),
                   jax.ShapeDtypeStruct((t, N_EXPERTS), f32)],
        compiler_params=_params(("parallel",)),
    )(oa, og, za, zb, x, wpa, wpb, wout, g, b, wr)


def _row_copy(src, src_row, dst, dst_row, sem):
    return pltpu.make_async_copy(src.at[pl.ds(src_row, 1)], dst.at[pl.ds(dst_row, 1)], sem)


def _dispatch_body(dest_ref, x_ref, xs_in, xs_out, sem, *, tm):
    del xs_in

    def issue(t, carry):
        for kk in range(TOP_K):
            _row_copy(x_ref, t, xs_out, dest_ref[t * TOP_K + kk], sem).start()
        return carry

    lax.fori_loop(0, tm, issue, 0)

    def drain(t, carry):
        for kk in range(TOP_K):
            _row_copy(x_ref, 0, xs_out, 0, sem).wait()
        return carry

    lax.fori_loop(0, tm, drain, 0)


def _dispatch(xp, dest_flat, n_rows):
    t = xp.shape[0]
    tm = _row_tile(t, (384, 256, 128, 64, 32, 16, 8))
    return pl.pallas_call(
        functools.partial(_dispatch_body, tm=tm),
        grid=(t // tm,),
        in_specs=[pl.BlockSpec((tm * TOP_K,), lambda i: (i,), memory_space=pltpu.SMEM),
                  pl.BlockSpec((tm, HALF), lambda i: (i, 0)),
                  pl.BlockSpec(memory_space=pl.ANY)],
        out_specs=pl.BlockSpec(memory_space=pl.ANY),
        out_shape=jax.ShapeDtypeStruct((n_rows, HALF), u32),
        scratch_shapes=[pltpu.SemaphoreType.DMA(())],
        input_output_aliases={2: 0},
        compiler_params=_params(("arbitrary",)),
    )(dest_flat, xp, jnp.zeros((n_rows, HALF), u32))


def _expert_body(be_ref, nu_ref, xs_ref, wg_ref, wu_ref, wd_ref, ys_ref, wgb, wub, wdb):
    i = pl.program_id(0)
    e = be_ref[i]
    prev = be_ref[jnp.maximum(i - 1, 0)]

    @pl.when((i == 0) | (e != prev))
    def _():
        wgb[...] = wg_ref[...].astype(bf16)
        wub[...] = wu_ref[...].astype(bf16)
        wdb[...] = wd_ref[...].astype(bf16)

    @pl.when(i < nu_ref[0])
    def _():
        lo, hi = _unpack_rows(xs_ref[...])
        lo = lo.astype(bf16)
        hi = hi.astype(bf16)
        g = (jnp.dot(lo, wgb[:HALF, :], preferred_element_type=f32)
             + jnp.dot(hi, wgb[HALF:, :], preferred_element_type=f32))
        u = (jnp.dot(lo, wub[:HALF, :], preferred_element_type=f32)
             + jnp.dot(hi, wub[HALF:, :], preferred_element_type=f32))
        hdn = (g * _sigmoid(g) * u).astype(bf16)
        ys_ref[...] = _pack_rows(jnp.dot(hdn, wdb[...], preferred_element_type=f32))

    @pl.when(i >= nu_ref[0])
    def _():
        ys_ref[...] = jnp.zeros(ys_ref.shape, u32)


def _experts(xs, blk_exp, n_used, w_gate, w_up, w_down, layer):
    n_rows = xs.shape[0]
    n_blocks = n_rows // EXPERT_BLOCK
    grid_spec = pltpu.PrefetchScalarGridSpec(
        num_scalar_prefetch=2,
        grid=(n_blocks,),
        in_specs=[
            pl.BlockSpec((EXPERT_BLOCK, HALF), lambda i, be, nu: (jnp.minimum(i, nu[0] - 1), 0)),
            pl.BlockSpec((None, None, D_MODEL, D_EXPERT), lambda i, be, nu: (layer, be[i], 0, 0)),
            pl.BlockSpec((None, None, D_MODEL, D_EXPERT), lambda i, be, nu: (layer, be[i], 0, 0)),
            pl.BlockSpec((None, None, D_EXPERT, D_MODEL), lambda i, be, nu: (layer, be[i], 0, 0)),
        ],
        out_specs=pl.BlockSpec((EXPERT_BLOCK, HALF), lambda i, be, nu: (i, 0)),
        scratch_shapes=[pltpu.VMEM((D_MODEL, D_EXPERT), bf16), pltpu.VMEM((D_MODEL, D_EXPERT), bf16),
                        pltpu.VMEM((D_EXPERT, D_MODEL), bf16)],
    )
    return pl.pallas_call(
        _expert_body,
        grid_spec=grid_spec,
        out_shape=jax.ShapeDtypeStruct((n_rows, HALF), u32),
        compiler_params=_params(("arbitrary",)),
    )(blk_exp, n_used, xs, w_gate, w_up, w_down)


def _combine_body(dest_ref, gate_ref, x1_ref, xp_ref, ys_hbm, wsg_ref, wsu_ref, wsd_ref, g_ref, b_ref,
                  o_ref, buf, sem, *, tm, alpha):
    def issue(t, carry):
        for kk in range(TOP_K):
            _row_copy(ys_hbm, dest_ref[t * TOP_K + kk], buf.at[kk], t, sem).start()
        return carry

    lax.fori_loop(0, tm, issue, 0)

    lo, hi = _unpack_rows(xp_ref[...])
    lo = lo.astype(bf16)
    hi = hi.astype(bf16)
    sg = (jnp.dot(lo, wsg_ref[:HALF, :], preferred_element_type=f32)
          + jnp.dot(hi, wsg_ref[HALF:, :], preferred_element_type=f32))
    su = (jnp.dot(lo, wsu_ref[:HALF, :], preferred_element_type=f32)
          + jnp.dot(hi, wsu_ref[HALF:, :], preferred_element_type=f32))
    shared = jnp.dot((sg * _sigmoid(sg) * su).astype(bf16), wsd_ref[...], preferred_element_type=f32)

    def drain(t, carry):
        for kk in range(TOP_K):
            _row_copy(ys_hbm, 0, buf.at[kk], 0, sem).wait()
        return carry

    lax.fori_loop(0, tm, drain, 0)

    gates = gate_ref[...]
    acc_lo = jnp.zeros((tm, HALF), f32)
    acc_hi = jnp.zeros((tm, HALF), f32)
    for kk in range(TOP_K):
        ylo, yhi = _unpack_rows(buf[kk])
        gk = gates[:, kk:kk + 1]
        acc_lo = acc_lo + gk * ylo
        acc_hi = acc_hi + gk * yhi
    moe = jnp.concatenate([acc_lo, acc_hi], axis=-1) + shared
    o_ref[...] = _layer_norm(alpha * x1_ref[...] + moe, g_ref[...], b_ref[...])


def _combine(dest_flat, gates, x1, xp, ys, wsg, wsu, wsd, g, b, alpha):
    t = x1.shape[0]
    tm = _row_tile(t, (384, 256, 128, 64, 32, 16, 8))

    def tok(width):
        return pl.BlockSpec((tm, width), lambda i: (i, 0))

    def whole(shape):
        return pl.BlockSpec(shape, lambda i: (0,) * len(shape))

    return pl.pallas_call(
        functools.partial(_combine_body, tm=tm, alpha=alpha),
        grid=(t // tm,),
        in_specs=[pl.BlockSpec((tm * TOP_K,), lambda i: (i,), memory_space=pltpu.SMEM),
                  tok(TOP_K), tok(D_MODEL), tok(HALF),
                  pl.BlockSpec(memory_space=pl.ANY),
                  whole((D_MODEL, D_EXPERT)), whole((D_MODEL, D_EXPERT)), whole((D_EXPERT, D_MODEL)),
                  whole((1, D_MODEL)), whole((1, D_MODEL))],
        out_specs=tok(D_MODEL),
        out_shape=jax.ShapeDtypeStruct((t, D_MODEL), f32),
        scratch_shapes=[pltpu.VMEM((TOP_K, tm, HALF), u32), pltpu.SemaphoreType.DMA(())],
        compiler_params=_params(("arbitrary",)),
    )(dest_flat, gates, x1, xp, ys, wsg, wsu, wsd, g, b)


def _route(scores, b_router):
    t = scores.shape[0]
    biased = scores + b_router.astype(f32)
    grp = lax.top_k(biased.reshape(t, N_GROUPS, N_EXPERTS // N_GROUPS), 2)[0].sum(-1)
    _, top_groups = lax.top_k(grp, TOPK_GROUPS)
    gmask = jnp.any(top_groups[..., None] == jnp.arange(N_GROUPS), axis=1)
    masked = jnp.where(jnp.repeat(gmask, N_EXPERTS // N_GROUPS, axis=1), biased, -jnp.inf)
    _, eidx = lax.top_k(masked, TOP_K)
    s_sel = jnp.take_along_axis(scores, eidx, axis=1)
    gates = s_sel / jnp.sum(s_sel, -1, keepdims=True) * ROUTED_SCALE

    blk = EXPERT_BLOCK
    onehot = jnp.any(eidx[:, :, None] == jnp.arange(N_EXPERTS), axis=1).astype(i32)
    before = jnp.cumsum(onehot, axis=0) - onehot
    counts = jnp.sum(onehot, axis=0)
    padded = (counts + blk - 1) // blk * blk
    pad_end = jnp.cumsum(padded)
    pad_start = pad_end - padded
    dest = pad_start[eidx] + jnp.take_along_axis(before, eidx, axis=1)
    n_rows = -(-(t * TOP_K + N_EXPERTS * (blk - 1)) // blk) * blk
    n_blocks = n_rows // blk
    n_used = (pad_end[-1] // blk).astype(i32)
    blk_exp = jnp.minimum(jnp.searchsorted(pad_end, jnp.arange(n_blocks) * blk, side='right'), N_EXPERTS - 1)
    last_exp = blk_exp[jnp.maximum(n_used - 1, 0)]
    blk_exp = jnp.where(jnp.arange(n_blocks) < n_used, blk_exp, last_exp).astype(i32)
    return dest.reshape(-1).astype(i32), gates, blk_exp, n_used.reshape(1), n_rows


def kernel(x_prompt, x_sample, cache_k, cache_v, state_gla, page_table, w_in, w_a2, b_a, lam_q1, lam_k1,
           lam_q2, lam_k2, sub_g, gla_g, w_pa, w_pb, w_out, ln1_g, ln1_b, w_router, b_router, w_gate, w_up,
           w_down, ws_gate, ws_up, ws_down, ln2_g, ln2_b):
    depth = w_in.shape[0]
    batch, seq, _ = x_prompt.shape
    bd = x_sample.shape[0]
    tp = batch * seq
    alpha = (2 * depth) ** 0.25
    n_pool = cache_k.shape[1]
    cache_k4 = cache_k.reshape(depth, n_pool, PAGE_SIZE * A_HEADS, 2 * A_HEAD_DIM)
    cache_v4 = cache_v.reshape(depth, n_pool, PAGE_SIZE * A_HEADS, A_VDIM)

    x = jnp.concatenate([x_prompt.reshape(tp, D_MODEL), x_sample.reshape(bd, D_MODEL)], axis=0)
    kp_l, vp_l, ks_l, vs_l, sp_l, ss_l = [], [], [], [], [], []
    for l in range(depth):
        lam_init = 0.8 - 0.6 * math.exp(-0.3 * l)
        wl = w_in[l]
        w_re = jnp.concatenate([wl[:, :3072], wl[:, 3088:5136], wl[:, 3072:3088],
                                jnp.zeros((D_MODEL, LANES - G_GATE_RANK), f32)], axis=1).astype(bf16)
        wa2 = jnp.concatenate([w_a2[l], jnp.zeros((LANES - G_GATE_RANK, G_K_WIDTH), f32)], axis=0)
        ba = b_a[l].reshape(1, G_K_WIDTH)
        lam_vecs = jnp.stack([lam_q1[l], lam_k1[l], lam_q2[l], lam_k2[l]]).astype(f32)
        subg = sub_g[l].reshape(1, A_VDIM)
        glag = gla_g[l].reshape(1, G_DV)

        q, kf, kb, vf, vb, gq, gk, gv, gr, za, zb, glr = _inproj(x, w_re)

        oa_p = _attn_prompt(q, kb, vb, lam_vecs, subg, batch, seq, lam_init)
        tail_pad = ((0, 0), (0, 16 - A_HEADS), (0, 0))
        oa_s = _attn_decode(q[tp:].reshape(bd, 1, A_WIDTH),
                            jnp.pad(kf[tp:].reshape(bd, A_HEADS, 2 * A_HEAD_DIM), tail_pad),
                            jnp.pad(vf[tp:].reshape(bd, A_HEADS, A_VDIM), tail_pad),
                            lam_vecs, subg, cache_k4, cache_v4, page_table, l, lam_init)
        og_p, s_p = _gla_prompt(gq, gk, gv, glr, gr, wa2, ba, glag, batch, seq)
        og_s, s_s = _gla_sample(gq[tp:].reshape(bd, 1, -1), gk[tp:].reshape(bd, 1, -1),
                                gv[tp:].reshape(bd, 1, -1), glr[tp:].reshape(bd, 1, -1),
                                gr[tp:].reshape(bd, 1, -1), state_gla[l], wa2, ba, glag)
        oa = jnp.concatenate([oa_p, oa_s.reshape(bd, A_WIDTH)], axis=0)
        og = jnp.concatenate([og_p, og_s.reshape(bd, G_V_WIDTH)], axis=0)

        x1, xp, scores = _postmix(oa, og, za, zb, x, w_pa[l].astype(bf16), w_pb[l].astype(bf16),
                                  w_out[l].astype(bf16), ln1_g[l].reshape(1, -1), ln1_b[l].reshape(1, -1),
                                  w_router[l], alpha)
        dest, gates, blk_exp, n_used, n_rows = _route(scores, b_router[l])
        xs = _dispatch(xp, dest, n_rows)
        ys = _experts(xs, blk_exp, n_used, w_gate, w_up, w_down, l)
        x = _combine(dest, gates, x1, xp, ys, ws_gate[l].astype(bf16), ws_up[l].astype(bf16),
                     ws_down[l].astype(bf16), ln2_g[l].reshape(1, -1), ln2_b[l].reshape(1, -1), alpha)

        kp_l.append(kf[:tp].reshape(batch, seq // PAGE_SIZE, PAGE_SIZE, A_HEADS, 2 * A_HEAD_DIM))
        vp_l.append(vf[:tp].reshape(batch, seq // PAGE_SIZE, PAGE_SIZE, A_HEADS, A_VDIM))
        ks_l.append(kf[tp:].reshape(bd, 1, A_HEADS, 2 * A_HEAD_DIM))
        vs_l.append(vf[tp:].reshape(bd, 1, A_HEADS, A_VDIM))
        sp_l.append(s_p.reshape(batch, G_HEADS, G_DK, G_DV))
        ss_l.append(s_s)

    y_prompt = x[:tp].reshape(batch, seq, D_MODEL)
    y_sample = x[tp:].reshape(bd, 1, D_MODEL)
    return (y_prompt, y_sample, jnp.stack(kp_l), jnp.stack(vp_l), jnp.stack(ks_l), jnp.stack(vs_l),
            jnp.stack(sp_l), jnp.stack(ss_l))
```

```python
import functools
import math

import jax
import jax.numpy as jnp
from jax import lax
from jax.experimental import pallas as pl
from jax.experimental.pallas import tpu as pltpu

f32 = jnp.float32
bf16 = jnp.bfloat16
u32 = jnp.uint32
i32 = jnp.int32

D_MODEL = 1024
A_HEADS = 4
A_HEAD_DIM = 64
A_VDIM = 128
A_WIDTH = A_HEADS * A_VDIM
G_HEADS = 4
G_DK = 64
G_DV = 128
G_K_WIDTH = G_HEADS * G_DK
G_V_WIDTH = G_HEADS * G_DV
G_GATE_RANK = 16
G_TAU = 16.0
N_EXPERTS = 256
TOP_K = 8
N_GROUPS = 8
TOPK_GROUPS = 4
D_EXPERT = 256
ROUTED_SCALE = 2.5
PAGE_SIZE = 128
LN_EPS = 1e-5
RMS_EPS = 1e-6

LANES = 128
VMEM_LIMIT = 56 * 1024 * 1024

NEG_BIG = -1e30
HALF = D_MODEL // 2
EXPERT_BLOCK = 256
GLA_CHUNK = 64
ATTN_TQ = 1024
ATTN_TK = 1024
ONES_ROWS = 16
LOG2E = math.log2(math.e)

C_Q, C_K, C_V, C_GQ, C_GK, C_GV, C_GR, C_ZA, C_ZB, C_GLR, C_END = (
    0, 512, 1024, 1536, 1792, 2048, 2560, 3072, 4096, 5120, 5248)


def _params(sem, vmem=VMEM_LIMIT):
    return pltpu.CompilerParams(dimension_semantics=sem, vmem_limit_bytes=vmem)


def _row_tile(n, cands=(512, 384, 256, 128, 64, 32, 16, 8)):
    for c in cands:
        if n % c == 0:
            return c
    raise ValueError(f"no row tile for {n}")


def _sigmoid(x):
    return 1.0 / (1.0 + jnp.exp(-x))


def _pack_rows(x):
    lo = lax.bitcast_convert_type(x[:, :HALF].astype(bf16).astype(f32), u32) >> 16
    hi = lax.bitcast_convert_type(x[:, HALF:].astype(bf16).astype(f32), u32) & jnp.uint32(0xFFFF0000)
    return lo | hi


def _unpack_rows(w):
    lo = lax.bitcast_convert_type(w << 16, f32)
    hi = lax.bitcast_convert_type(w & jnp.uint32(0xFFFF0000), f32)
    return lo, hi


def _layer_norm(h, g, b):
    mu = jnp.mean(h, axis=-1, keepdims=True)
    d = h - mu
    var = jnp.mean(d * d, axis=-1, keepdims=True)
    return d * lax.rsqrt(var + LN_EPS) * g + b


def _rms_norm(o, g):
    return o * lax.rsqrt(jnp.mean(o * o, axis=-1, keepdims=True) + RMS_EPS) * g


def _lam_value(lam_ref, lam_init):
    l = lam_ref[...]
    s1 = jnp.sum(l[0:1] * l[1:2], axis=-1, keepdims=True)
    s2 = jnp.sum(l[2:3] * l[3:4], axis=-1, keepdims=True)
    return jnp.exp(s1) - jnp.exp(s2) + lam_init


def _inproj_body(x_ref, w_ref, wvt_ref, q_o, kf_o, kb_o, vf_o, vt_o, gq_o, gk_o, gv_o, gr_o, za_o, zb_o,
                 glr_o):
    xb = x_ref[...].astype(bf16)

    def mm(c0, c1):
        return jnp.dot(xb, w_ref[:, c0:c1], preferred_element_type=f32)

    q_o[...] = (mm(C_Q, C_K) * (A_HEAD_DIM ** -0.5 * LOG2E)).astype(bf16)
    k = mm(C_K, C_V)
    kf_o[...] = k
    kb_o[...] = k.astype(bf16)
    vf_o[...] = mm(C_V, C_GQ)
    vt_o[...] = lax.dot_general(wvt_ref[...], xb, (((1,), (1,)), ((), ())),
                                preferred_element_type=f32).astype(bf16)
    gq_o[...] = mm(C_GQ, C_GK) * (G_DK ** -0.5)
    gk_o[...] = mm(C_GK, C_GV)
    gv_o[...] = mm(C_GV, C_GR)
    gr_o[...] = mm(C_GR, C_ZA)
    za_o[...] = mm(C_ZA, C_ZB).astype(bf16)
    zb_o[...] = mm(C_ZB, C_GLR).astype(bf16)
    glr_o[...] = mm(C_GLR, C_END)


def _inproj(x, w, wvt):
    t = x.shape[0]
    tm = _row_tile(t, (384, 256, 128))
    outs = [(512, bf16), (512, f32), (512, bf16), (512, f32), None, (256, f32), (256, f32),
            (512, f32), (512, f32), (1024, bf16), (1024, bf16), (LANES, f32)]
    out_specs = [pl.BlockSpec((A_WIDTH, tm), lambda i: (0, i)) if o is None
                 else pl.BlockSpec((tm, o[0]), lambda i: (i, 0)) for o in outs]
    out_shape = [jax.ShapeDtypeStruct((A_WIDTH, t), bf16) if o is None
                 else jax.ShapeDtypeStruct((t, o[0]), o[1]) for o in outs]
    return pl.pallas_call(
        _inproj_body,
        grid=(t // tm,),
        in_specs=[pl.BlockSpec((tm, D_MODEL), lambda i: (i, 0)),
                  pl.BlockSpec((D_MODEL, C_END), lambda i: (0, 0)),
                  pl.BlockSpec((A_WIDTH, D_MODEL), lambda i: (0, 0))],
        out_specs=out_specs,
        out_shape=out_shape,
        compiler_params=_params(("parallel",)),
    )(x, w, wvt)


def _attn_body(qi_tab, kj_tab, diag_tab, last_tab, q_ref, k_ref, vt_ref, lam_ref, subg_ref, o_ref,
               m1, a1, m2, a2, *, tq, tk, lam_init):
    p = pl.program_id(2)
    qi = qi_tab[p]
    kj = kj_tab[p]

    @pl.when(kj == 0)
    def _():
        for m, a in ((m1, a1), (m2, a2)):
            m[...] = jnp.full(m.shape, NEG_BIG, f32)
            a[...] = jnp.zeros(a.shape, f32)

    q = q_ref[...]
    k = k_ref[...]
    vt = jnp.concatenate([vt_ref[...], jnp.ones((ONES_ROWS, tk), bf16)], axis=0)
    lane = lax.broadcasted_iota(i32, (1, LANES), 1)
    zero = jnp.zeros_like(q)
    q1 = jnp.where(lane < A_HEAD_DIM, q, zero)
    q2 = jnp.where(lane >= A_HEAD_DIM, q, zero)

    def step(masked):
        for qm, m, a in ((q1, m1, a1), (q2, m2, a2)):
            s = lax.dot_general(k, qm, (((1,), (1,)), ((), ())), preferred_element_type=f32)
            if masked:
                kpos = kj * tk + lax.broadcasted_iota(i32, (tk, tq), 0)
                qpos = qi * tq + lax.broadcasted_iota(i32, (tk, tq), 1)
                s = jnp.where(kpos <= qpos, s, NEG_BIG)
            m_prev = m[...]
            m_new = jnp.maximum(m_prev, jnp.max(s, axis=0, keepdims=True))
            alpha = jnp.exp2(m_prev - m_new)
            pr = jnp.exp2(s - m_new).astype(bf16)
            a[...] = alpha * a[...] + jnp.dot(vt, pr, preferred_element_type=f32)
            m[...] = m_new

    @pl.when(diag_tab[p] == 1)
    def _():
        step(True)

    @pl.when(diag_tab[p] == 0)
    def _():
        step(False)

    @pl.when(last_tab[p] == 1)
    def _():
        lam = _lam_value(lam_ref, lam_init)
        ot = (a1[:A_VDIM, :] / a1[A_VDIM:A_VDIM + 1, :]
              - lam * (a2[:A_VDIM, :] / a2[A_VDIM:A_VDIM + 1, :]))
        ms = jnp.mean(ot * ot, axis=0, keepdims=True)
        on = ot * lax.rsqrt(ms + RMS_EPS) * subg_ref[...] * (1.0 - lam_init)
        o_ref[...] = on.T.astype(o_ref.dtype)


def _attn_prompt(q, k, vt, lam_vecs, sub_g_col, batch, seq, lam_init):
    tq = min(ATTN_TQ, seq)
    tk = min(ATTN_TK, seq)
    nq, nk = seq // tq, seq // tk
    qi_l, kj_l, dg_l, ls_l = [], [], [], []
    for qi in range(nq):
        last = ((qi + 1) * tq - 1) // tk
        for kj in range(last + 1):
            qi_l.append(qi)
            kj_l.append(kj)
            dg_l.append(1 if (kj + 1) * tk - 1 > qi * tq else 0)
            ls_l.append(1 if kj == last else 0)
    tabs = [jnp.asarray(t, i32) for t in (qi_l, kj_l, dg_l, ls_l)]
    n_pairs = len(qi_l)
    body = functools.partial(_attn_body, tq=tq, tk=tk, lam_init=lam_init)
    grid_spec = pltpu.PrefetchScalarGridSpec(
        num_scalar_prefetch=4,
        grid=(batch, A_HEADS, n_pairs),
        in_specs=[
            pl.BlockSpec((tq, LANES), lambda b, h, p, qt, kt, dt, lt: (b * nq + qt[p], h)),
            pl.BlockSpec((tk, LANES), lambda b, h, p, qt, kt, dt, lt: (b * nk + kt[p], h)),
            pl.BlockSpec((A_VDIM, tk), lambda b, h, p, qt, kt, dt, lt: (h, b * nk + kt[p])),
            pl.BlockSpec((4, A_HEAD_DIM), lambda b, h, p, *_: (0, 0)),
            pl.BlockSpec((A_VDIM, 1), lambda b, h, p, *_: (0, 0)),
        ],
        out_specs=pl.BlockSpec((tq, LANES), lambda b, h, p, qt, kt, dt, lt: (b * nq + qt[p], h)),
        scratch_shapes=[pltpu.VMEM((1, tq), f32), pltpu.VMEM((A_VDIM + ONES_ROWS, tq), f32),
                        pltpu.VMEM((1, tq), f32), pltpu.VMEM((A_VDIM + ONES_ROWS, tq), f32)],
    )
    return pl.pallas_call(
        body,
        grid_spec=grid_spec,
        out_shape=jax.ShapeDtypeStruct((batch * seq, A_WIDTH), bf16),
        compiler_params=_params(("parallel", "parallel", "arbitrary")),
    )(*tabs, q, k, vt, lam_vecs, sub_g_col)


def _decode_body(pt_ref, q_ref, kn_ref, vn_ref, lam_ref, subg_ref, *refs, n_pages, lam_init):
    k_refs = refs[:n_pages]
    v_refs = refs[n_pages:2 * n_pages]
    o_ref = refs[2 * n_pages]
    kbuf, vbuf = refs[2 * n_pages + 1:]
    rows_pg = PAGE_SIZE * A_HEADS
    past = n_pages * rows_pg
    tail = 16
    n_col = past + tail

    for p in range(n_pages):
        kbuf[p * rows_pg:(p + 1) * rows_pg, :] = k_refs[p][...].astype(bf16)
        vbuf[p * rows_pg:(p + 1) * rows_pg, :] = v_refs[p][...].astype(bf16)
    kbuf[past:, :] = kn_ref[0].astype(bf16)
    vbuf[past:, :] = vn_ref[0].astype(bf16)

    qrow = q_ref[0].astype(f32)
    row = lax.broadcasted_iota(i32, (16, LANES), 0)
    lane = lax.broadcasted_iota(i32, (16, LANES), 1)
    qmat = jnp.zeros((16, LANES), f32)
    for h in range(A_HEADS):
        qh = jnp.broadcast_to(qrow[:, h * LANES:(h + 1) * LANES], (16, LANES))
        sel = ((row >> 1) == h) & ((lane >= A_HEAD_DIM) == ((row & 1) == 1))
        qmat = jnp.where(sel, qh, qmat)
    s = lax.dot_general(qmat.astype(bf16), kbuf[...], (((1,), (1,)), ((), ())), preferred_element_type=f32)
    srow = lax.broadcasted_iota(i32, (16, n_col), 0)
    scol = lax.broadcasted_iota(i32, (16, n_col), 1)
    valid = ((scol & (A_HEADS - 1)) == (srow >> 1)) & (scol < past + A_HEADS) & (srow < 2 * A_HEADS)
    s = jnp.where(valid, s, NEG_BIG)
    m = jnp.max(s, axis=-1, keepdims=True)
    pr = jnp.where(valid, jnp.exp2(s - m), 0.0)
    den = jnp.maximum(jnp.sum(pr, axis=-1, keepdims=True), 1e-30)
    pn = pr / den
    o8 = jnp.dot(pn.astype(bf16), vbuf[...], preferred_element_type=f32)
    lam = _lam_value(lam_ref, lam_init)
    for h in range(A_HEADS):
        o = o8[2 * h:2 * h + 1, :] - lam * o8[2 * h + 1:2 * h + 2, :]
        o_ref[0, :, h * LANES:(h + 1) * LANES] = (
            _rms_norm(o, subg_ref[...]) * (1.0 - lam_init)).astype(o_ref.dtype)


def _attn_decode(q_s, k_new, v_new, lam_vecs, sub_g, cache_k4, cache_v4, page_table, layer, lam_init):
    bd, n_pages = page_table.shape
    rows_pg = PAGE_SIZE * A_HEADS
    body = functools.partial(_decode_body, n_pages=n_pages, lam_init=lam_init)

    def page_spec(p):
        return pl.BlockSpec((None, None, rows_pg, LANES),
                            lambda b, pt, p=p: (layer, pt[b * n_pages + p], 0, 0))

    grid_spec = pltpu.PrefetchScalarGridSpec(
        num_scalar_prefetch=1,
        grid=(bd,),
        in_specs=[pl.BlockSpec((1, 1, A_WIDTH), lambda b, pt: (b, 0, 0)),
                  pl.BlockSpec((1, 16, LANES), lambda b, pt: (b, 0, 0)),
                  pl.BlockSpec((1, 16, LANES), lambda b, pt: (b, 0, 0)),
                  pl.BlockSpec((4, A_HEAD_DIM), lambda b, pt: (0, 0)),
                  pl.BlockSpec((1, A_VDIM), lambda b, pt: (0, 0))]
        + [page_spec(p) for p in range(n_pages)] * 2,
        out_specs=pl.BlockSpec((1, 1, A_WIDTH), lambda b, pt: (b, 0, 0)),
        scratch_shapes=[pltpu.VMEM((n_pages * rows_pg + 16, LANES), bf16),
                        pltpu.VMEM((n_pages * rows_pg + 16, LANES), bf16)],
    )
    return pl.pallas_call(
        body,
        grid_spec=grid_spec,
        out_shape=jax.ShapeDtypeStruct((bd, 1, A_WIDTH), bf16),
        compiler_params=_params(("arbitrary",)),
    )(page_table.reshape(-1), q_s, k_new, v_new, lam_vecs, sub_g,
      *([cache_k4] * n_pages), *([cache_v4] * n_pages))


def _log_decay(glr, wa2, ba):
    z = jnp.dot(glr, wa2, preferred_element_type=f32, precision=lax.Precision.HIGHEST) + ba
    return (jnp.minimum(z, 0.0) - jnp.log(1.0 + jnp.exp(-jnp.abs(z)))) * (1.0 / G_TAU)


def _gla_finish(o, gr, g):
    return _rms_norm(o, g) * (gr * _sigmoid(gr))


def _gla_prompt_body(q_ref, k_ref, v_ref, glr_ref, gr_ref, wa2_ref, ba_ref, g_ref, o_ref, s_ref, state,
                     *, c):
    ci = pl.program_id(1)

    @pl.when(ci == 0)
    def _():
        state[...] = jnp.zeros(state.shape, f32)

    la = _log_decay(glr_ref[...], wa2_ref[...], ba_ref[...])
    ri = lax.broadcasted_iota(i32, (c, c), 0)
    cj = lax.broadcasted_iota(i32, (c, c), 1)
    tril = ri >= cj
    b = jnp.dot(tril.astype(f32), la, preferred_element_type=f32, precision=lax.Precision.HIGHEST)
    b_last = b[c - 1:c, :]
    q = q_ref[...]
    k = k_ref[...]
    qd = q * jnp.exp(b)
    kinv = (k * jnp.exp(-b)).astype(bf16)
    kd = k * jnp.exp(b_last - b)
    kdt = kd.T.astype(bf16)
    tot = lax.dot_general(la.T, jnp.ones((c, LANES), f32), (((1,), (0,)), ((), ())),
                          preferred_element_type=f32, precision=lax.Precision.HIGHEST)
    dec_rows = jnp.exp(tot)
    lane = lax.broadcasted_iota(i32, (1, G_K_WIDTH), 1)
    st = state[...]
    stb = st.astype(bf16)
    upd = jnp.dot(kdt, v_ref[...].astype(bf16), preferred_element_type=f32)
    for h in range(G_HEADS):
        qh = jnp.where((lane >> 6) == h, qd, 0.0).astype(bf16)
        att = lax.dot_general(qh, kinv, (((1,), (1,)), ((), ())), preferred_element_type=f32)
        att = jnp.where(tril, att, 0.0)
        vh = v_ref[:, h * G_DV:(h + 1) * G_DV].astype(bf16)
        o = jnp.dot(att.astype(bf16), vh, preferred_element_type=f32)
        o = o + jnp.dot(qh, stb, preferred_element_type=f32)
        o_ref[:, h * G_DV:(h + 1) * G_DV] = _gla_finish(
            o, gr_ref[:, h * G_DV:(h + 1) * G_DV], g_ref[...]).astype(o_ref.dtype)
        r0 = h * G_DK
        state[r0:r0 + G_DK, :] = (dec_rows[r0:r0 + G_DK, :] * st[r0:r0 + G_DK, :]
                                  + upd[r0:r0 + G_DK, h * G_DV:(h + 1) * G_DV])

    @pl.when(ci == pl.num_programs(1) - 1)
    def _():
        s_ref[0] = state[...]


def _gla_prompt(gq, gk, gv, glr, gr, wa2, ba, gla_g, batch, seq):
    c = math.gcd(seq, GLA_CHUNK)
    n = seq // c
    body = functools.partial(_gla_prompt_body, c=c)

    def tok(width):
        return pl.BlockSpec((c, width), lambda b, i: (b * n + i, 0))

    def whole(shape):
        return pl.BlockSpec(shape, lambda b, i: (0,) * len(shape))

    return pl.pallas_call(
        body,
        grid=(batch, n),
        in_specs=[tok(G_K_WIDTH), tok(G_K_WIDTH), tok(G_V_WIDTH), tok(LANES), tok(G_V_WIDTH),
                  whole((LANES, G_K_WIDTH)), whole((1, G_K_WIDTH)), whole((1, G_DV))],
        out_specs=[tok(G_V_WIDTH),
                   pl.BlockSpec((1, G_K_WIDTH, G_DV), lambda b, i: (b, 0, 0))],
        out_shape=[jax.ShapeDtypeStruct((batch * seq, G_V_WIDTH), bf16),
                   jax.ShapeDtypeStruct((batch, G_K_WIDTH, G_DV), f32)],
        scratch_shapes=[pltpu.VMEM((G_K_WIDTH, G_DV), f32)],
        compiler_params=_params(("parallel", "arbitrary")),
    )(gq, gk, gv, glr, gr, wa2, ba, gla_g)


def _gla_sample_body(q_ref, k_ref, v_ref, glr_ref, gr_ref, s_ref, wa2_ref, ba_ref, g_ref, o_ref, sn_ref):
    la = _log_decay(glr_ref[0], wa2_ref[...], ba_ref[...])
    rows = jnp.concatenate([jnp.broadcast_to(jnp.exp(la), (G_DK, G_K_WIDTH)),
                            jnp.broadcast_to(k_ref[0], (G_DK, G_K_WIDTH)),
                            jnp.broadcast_to(q_ref[0], (G_DK, G_K_WIDTH))], axis=0)
    ri = lax.broadcasted_iota(i32, (3 * G_DK, G_K_WIDTH), 0)
    li = lax.broadcasted_iota(i32, (3 * G_DK, G_K_WIDTH), 1)
    picked = jnp.where((ri & (G_DK - 1)) == (li & (G_DK - 1)), rows, 0.0)
    si = lax.broadcasted_iota(i32, (G_K_WIDTH, G_V_WIDTH), 0)
    sj = lax.broadcasted_iota(i32, (G_K_WIDTH, G_V_WIDTH), 1)
    seg = ((si >> 6) == (sj >> 7)).astype(f32)
    cols = jnp.dot(picked, seg, preferred_element_type=f32, precision=lax.Precision.HIGHEST)
    v = v_ref[0]
    for h in range(G_HEADS):
        sl = slice(h * G_DV, (h + 1) * G_DV)
        a_c = cols[0:G_DK, sl]
        k_c = cols[G_DK:2 * G_DK, sl]
        q_c = cols[2 * G_DK:3 * G_DK, sl]
        s_new = a_c * s_ref[0, h] + k_c * v[:, sl]
        sn_ref[0, h] = s_new
        o = jnp.sum(q_c * s_new, axis=0, keepdims=True)
        o_ref[0, :, sl] = _gla_finish(o, gr_ref[0][:, sl], g_ref[...]).astype(o_ref.dtype)


def _gla_sample(gq, gk, gv, glr, gr, s0, wa2, ba, gla_g):
    bd = gq.shape[0]

    def vec(width):
        return pl.BlockSpec((1, 1, width), lambda b: (b, 0, 0))

    def whole(shape):
        return pl.BlockSpec(shape, lambda b: (0,) * len(shape))

    st = pl.BlockSpec((1, G_HEADS, G_DK, G_DV), lambda b: (b, 0, 0, 0))
    return pl.pallas_call(
        _gla_sample_body,
        grid=(bd,),
        in_specs=[vec(G_K_WIDTH), vec(G_K_WIDTH), vec(G_V_WIDTH), vec(LANES), vec(G_V_WIDTH), st,
                  whole((LANES, G_K_WIDTH)), whole((1, G_K_WIDTH)), whole((1, G_DV))],
        out_specs=[vec(G_V_WIDTH), st],
        out_shape=[jax.ShapeDtypeStruct((bd, 1, G_V_WIDTH), bf16),
                   jax.ShapeDtypeStruct((bd, G_HEADS, G_DK, G_DV), f32)],
        compiler_params=_params(("parallel",)),
    )(gq, gk, gv, glr, gr, s0, wa2, ba, gla_g)


def _postmix_body(oa_ref, og_ref, za_ref, zb_ref, x_ref, wpa_ref, wpb_ref, wout_ref, g_ref, b_ref,
                  x1_o, xp_o, *, alpha):
    ya = jnp.dot(oa_ref[...], wpa_ref[...], preferred_element_type=f32)
    yb = jnp.dot(og_ref[...], wpb_ref[...], preferred_element_type=f32)
    merged = _sigmoid(za_ref[...].astype(f32)) * ya + _sigmoid(zb_ref[...].astype(f32)) * yb
    mix = jnp.dot(merged.astype(bf16), wout_ref[...], preferred_element_type=f32)
    x1 = _layer_norm(alpha * x_ref[...] + mix, g_ref[...], b_ref[...])
    x1_o[...] = x1
    xp_o[...] = _pack_rows(x1)


def _postmix(oa, og, za, zb, x, wpa, wpb, wout, g, b, alpha):
    t = x.shape[0]
    tm = _row_tile(t, (384, 256, 128, 64, 32, 16, 8))

    def tok(width):
        return pl.BlockSpec((tm, width), lambda i: (i, 0))

    def whole(shape):
        return pl.BlockSpec(shape, lambda i: (0,) * len(shape))

    return pl.pallas_call(
        functools.partial(_postmix_body, alpha=alpha),
        grid=(t // tm,),
        in_specs=[tok(A_WIDTH), tok(G_V_WIDTH), tok(D_MODEL), tok(D_MODEL), tok(D_MODEL),
                  whole((A_WIDTH, D_MODEL)), whole((G_V_WIDTH, D_MODEL)), whole((D_MODEL, D_MODEL)),
                  whole((1, D_MODEL)), whole((1, D_MODEL))],
        out_specs=[tok(D_MODEL), tok(HALF)],
        out_shape=[jax.ShapeDtypeStruct((t, D_MODEL), f32),
                   jax.ShapeDtypeStruct((t, HALF), u32)],
        compiler_params=_params(("parallel",)),
    )(oa, og, za, zb, x, wpa, wpb, wout, g, b)


def _router_body(x1_ref, wrt_ref, bcol_ref, tri_ref, eidx_o, gate_o, rank_o, cnt_o, cnt, *, tm):
    @pl.when(pl.program_id(0) == 0)
    def _():
        cnt[...] = jnp.zeros(cnt.shape, f32)

    logits = lax.dot_general(wrt_ref[...], x1_ref[...], (((1,), (1,)), ((), ())),
                             preferred_element_type=f32, precision=lax.Precision.HIGHEST)
    scores = _sigmoid(logits)
    biased = scores + bcol_ref[...]
    gsz = N_EXPERTS // N_GROUPS
    neg_inf = -jnp.inf

    gi = lax.broadcasted_iota(i32, (gsz, tm), 0)
    segs, gscore = [], []
    for g in range(N_GROUPS):
        seg = biased[g * gsz:(g + 1) * gsz, :]
        m1 = jnp.max(seg, axis=0, keepdims=True)
        i1 = jnp.min(jnp.where(seg == m1, gi, gsz), axis=0, keepdims=True)
        m2 = jnp.max(jnp.where(gi == i1, neg_inf, seg), axis=0, keepdims=True)
        segs.append(seg)
        gscore.append(m1 + m2)
    parts = []
    for g in range(N_GROUPS):
        beat = jnp.zeros((1, tm), i32)
        for o in range(N_GROUPS):
            if o != g:
                wins = (gscore[o] > gscore[g]) | ((gscore[o] == gscore[g]) & (o < g))
                beat = beat + wins.astype(i32)
        parts.append(jnp.where(beat < TOPK_GROUPS, segs[g], neg_inf))
    masked = jnp.concatenate(parts, axis=0)

    ei = lax.broadcasted_iota(i32, (N_EXPERTS, tm), 0)
    sel_rows, idx_rows = [], []
    chosen = jnp.zeros((N_EXPERTS, tm), f32)
    for _ in range(TOP_K):
        m = jnp.max(masked, axis=0, keepdims=True)
        idx = jnp.min(jnp.where(masked == m, ei, N_EXPERTS), axis=0, keepdims=True)
        hit = ei == idx
        sel_rows.append(jnp.sum(jnp.where(hit, scores, 0.0), axis=0, keepdims=True))
        idx_rows.append(idx)
        chosen = jnp.where(hit, 1.0, chosen)
        masked = jnp.where(hit, neg_inf, masked)
    s_sel = jnp.concatenate(sel_rows, axis=0)
    gate_o[...] = s_sel / jnp.sum(s_sel, axis=0, keepdims=True) * ROUTED_SCALE
    eidx_o[...] = jnp.concatenate(idx_rows, axis=0)
    before = jnp.dot(chosen.astype(bf16), tri_ref[...], preferred_element_type=f32) + cnt[...]
    rank_rows = [jnp.sum(jnp.where(ei == idx, before, 0.0), axis=0, keepdims=True) for idx in idx_rows]
    rank_o[...] = jnp.concatenate(rank_rows, axis=0).astype(i32)
    cnt[...] = cnt[...] + jnp.sum(chosen, axis=1, keepdims=True)
    cnt_o[...] = cnt[...]


def _router(x1, wrt, b_col):
    t = x1.shape[0]
    tm = _row_tile(t, (384, 256, 128))
    tri = (jnp.arange(tm)[:, None] < jnp.arange(tm)[None, :]).astype(bf16)

    def tokcol(dt):
        return pl.BlockSpec((TOP_K, tm), lambda i: (0, i)), jax.ShapeDtypeStruct((TOP_K, t), dt)

    def whole(shape):
        return pl.BlockSpec(shape, lambda i: (0,) * len(shape))

    specs, shapes = zip(tokcol(i32), tokcol(f32), tokcol(i32),
                        (whole((N_EXPERTS, 1)), jax.ShapeDtypeStruct((N_EXPERTS, 1), f32)))
    return pl.pallas_call(
        functools.partial(_router_body, tm=tm),
        grid=(t // tm,),
        in_specs=[pl.BlockSpec((tm, D_MODEL), lambda i: (i, 0)), whole((N_EXPERTS, D_MODEL)),
                  whole((N_EXPERTS, 1)), whole((tm, tm))],
        out_specs=list(specs),
        out_shape=list(shapes),
        scratch_shapes=[pltpu.VMEM((N_EXPERTS, 1), f32)],
        compiler_params=_params(("arbitrary",)),
    )(x1, wrt, b_col, tri)


def _row_copy(src, src_row, dst, dst_row, sem):
    return pltpu.make_async_copy(src.at[pl.ds(src_row, 1)], dst.at[pl.ds(dst_row, 1)], sem)


def _dispatch_body(dest_ref, x_ref, xs_in, xs_out, sem, *, tm):
    del xs_in

    def issue(t, carry):
        for kk in range(TOP_K):
            _row_copy(x_ref, t, xs_out, dest_ref[t * TOP_K + kk], sem).start()
        return carry

    lax.fori_loop(0, tm, issue, 0)

    def drain(t, carry):
        for kk in range(TOP_K):
            _row_copy(x_ref, 0, xs_out, 0, sem).wait()
        return carry

    lax.fori_loop(0, tm, drain, 0)


def _dispatch(xp, dest_flat, n_rows):
    t = xp.shape[0]
    tm = _row_tile(t, (384, 256, 128, 64, 32, 16, 8))
    return pl.pallas_call(
        functools.partial(_dispatch_body, tm=tm),
        grid=(t // tm,),
        in_specs=[pl.BlockSpec((tm * TOP_K,), lambda i: (i,), memory_space=pltpu.SMEM),
                  pl.BlockSpec((tm, HALF), lambda i: (i, 0)),
                  pl.BlockSpec(memory_space=pl.ANY)],
        out_specs=pl.BlockSpec(memory_space=pl.ANY),
        out_shape=jax.ShapeDtypeStruct((n_rows, HALF), u32),
        scratch_shapes=[pltpu.SemaphoreType.DMA(())],
        input_output_aliases={2: 0},
        compiler_params=_params(("arbitrary",)),
    )(dest_flat, xp, jnp.zeros((n_rows, HALF), u32))


def _expert_body(be_ref, nu_ref, xs_ref, wg_ref, wu_ref, wd_ref, ys_ref, wgb, wub, wdb):
    i = pl.program_id(0)
    e = be_ref[i]
    prev = be_ref[jnp.maximum(i - 1, 0)]

    @pl.when((i == 0) | (e != prev))
    def _():
        wgb[...] = wg_ref[...].astype(bf16)
        wub[...] = wu_ref[...].astype(bf16)
        wdb[...] = wd_ref[...].astype(bf16)

    @pl.when(i < nu_ref[0])
    def _():
        lo, hi = _unpack_rows(xs_ref[...])
        lo = lo.astype(bf16)
        hi = hi.astype(bf16)
        g = (jnp.dot(lo, wgb[:HALF, :], preferred_element_type=f32)
             + jnp.dot(hi, wgb[HALF:, :], preferred_element_type=f32))
        u = (jnp.dot(lo, wub[:HALF, :], preferred_element_type=f32)
             + jnp.dot(hi, wub[HALF:, :], preferred_element_type=f32))
        hdn = (g * _sigmoid(g) * u).astype(bf16)
        ys_ref[...] = _pack_rows(jnp.dot(hdn, wdb[...], preferred_element_type=f32))

    @pl.when(i >= nu_ref[0])
    def _():
        ys_ref[...] = jnp.zeros(ys_ref.shape, u32)


def _experts(xs, blk_exp, n_used, w_gate, w_up, w_down, layer):
    n_rows = xs.shape[0]
    n_blocks = n_rows // EXPERT_BLOCK
    grid_spec = pltpu.PrefetchScalarGridSpec(
        num_scalar_prefetch=2,
        grid=(n_blocks,),
        in_specs=[
            pl.BlockSpec((EXPERT_BLOCK, HALF), lambda i, be, nu: (jnp.minimum(i, nu[0] - 1), 0)),
            pl.BlockSpec((None, None, D_MODEL, D_EXPERT), lambda i, be, nu: (layer, be[i], 0, 0)),
            pl.BlockSpec((None, None, D_MODEL, D_EXPERT), lambda i, be, nu: (layer, be[i], 0, 0)),
            pl.BlockSpec((None, None, D_EXPERT, D_MODEL), lambda i, be, nu: (layer, be[i], 0, 0)),
        ],
        out_specs=pl.BlockSpec((EXPERT_BLOCK, HALF), lambda i, be, nu: (i, 0)),
        scratch_shapes=[pltpu.VMEM((D_MODEL, D_EXPERT), bf16), pltpu.VMEM((D_MODEL, D_EXPERT), bf16),
                        pltpu.VMEM((D_EXPERT, D_MODEL), bf16)],
    )
    return pl.pallas_call(
        _expert_body,
        grid_spec=grid_spec,
        out_shape=jax.ShapeDtypeStruct((n_rows, HALF), u32),
        compiler_params=_params(("arbitrary",)),
    )(blk_exp, n_used, xs, w_gate, w_up, w_down)


def _combine_body(dest_ref, gate_ref, x1_ref, xp_ref, ys_hbm, wsg_ref, wsu_ref, wsd_ref, g_ref, b_ref,
                  o_ref, buf, sem, *, tm, alpha):
    def issue(t, carry):
        for kk in range(TOP_K):
            _row_copy(ys_hbm, dest_ref[t * TOP_K + kk], buf.at[kk], t, sem).start()
        return carry

    lax.fori_loop(0, tm, issue, 0)

    lo, hi = _unpack_rows(xp_ref[...])
    lo = lo.astype(bf16)
    hi = hi.astype(bf16)
    sg = (jnp.dot(lo, wsg_ref[:HALF, :], preferred_element_type=f32)
          + jnp.dot(hi, wsg_ref[HALF:, :], preferred_element_type=f32))
    su = (jnp.dot(lo, wsu_ref[:HALF, :], preferred_element_type=f32)
          + jnp.dot(hi, wsu_ref[HALF:, :], preferred_element_type=f32))
    shared = jnp.dot((sg * _sigmoid(sg) * su).astype(bf16), wsd_ref[...], preferred_element_type=f32)

    def drain(t, carry):
        for kk in range(TOP_K):
            _row_copy(ys_hbm, 0, buf.at[kk], 0, sem).wait()
        return carry

    lax.fori_loop(0, tm, drain, 0)

    gates = gate_ref[...]
    acc_lo = jnp.zeros((tm, HALF), f32)
    acc_hi = jnp.zeros((tm, HALF), f32)
    for kk in range(TOP_K):
        ylo, yhi = _unpack_rows(buf[kk])
        gk = gates[:, kk:kk + 1]
        acc_lo = acc_lo + gk * ylo
        acc_hi = acc_hi + gk * yhi
    moe = jnp.concatenate([acc_lo, acc_hi], axis=-1) + shared
    o_ref[...] = _layer_norm(alpha * x1_ref[...] + moe, g_ref[...], b_ref[...])


def _combine(dest_flat, gates, x1, xp, ys, wsg, wsu, wsd, g, b, alpha):
    t = x1.shape[0]
    tm = _row_tile(t, (384, 256, 128, 64, 32, 16, 8))

    def tok(width):
        return pl.BlockSpec((tm, width), lambda i: (i, 0))

    def whole(shape):
        return pl.BlockSpec(shape, lambda i: (0,) * len(shape))

    return pl.pallas_call(
        functools.partial(_combine_body, tm=tm, alpha=alpha),
        grid=(t // tm,),
        in_specs=[pl.BlockSpec((tm * TOP_K,), lambda i: (i,), memory_space=pltpu.SMEM),
                  tok(TOP_K), tok(D_MODEL), tok(HALF),
                  pl.BlockSpec(memory_space=pl.ANY),
                  whole((D_MODEL, D_EXPERT)), whole((D_MODEL, D_EXPERT)), whole((D_EXPERT, D_MODEL)),
                  whole((1, D_MODEL)), whole((1, D_MODEL))],
        out_specs=tok(D_MODEL),
        out_shape=jax.ShapeDtypeStruct((t, D_MODEL), f32),
        scratch_shapes=[pltpu.VMEM((TOP_K, tm, HALF), u32), pltpu.SemaphoreType.DMA(())],
        compiler_params=_params(("arbitrary",)),
    )(dest_flat, gates, x1, xp, ys, wsg, wsu, wsd, g, b)


def _layout(eidx_t, rank_t, counts):
    t = eidx_t.shape[1]
    blk = EXPERT_BLOCK
    padded = (counts + blk - 1) // blk * blk
    pad_end = jnp.cumsum(padded)
    pad_start = pad_end - padded
    dest = (pad_start[eidx_t] + rank_t).T
    n_rows = -(-(t * TOP_K + N_EXPERTS * (blk - 1)) // blk) * blk
    n_blocks = n_rows // blk
    n_used = (pad_end[-1] // blk).astype(i32)
    blk_exp = jnp.minimum(jnp.searchsorted(pad_end, jnp.arange(n_blocks) * blk, side='right'), N_EXPERTS - 1)
    last_exp = blk_exp[jnp.maximum(n_used - 1, 0)]
    blk_exp = jnp.where(jnp.arange(n_blocks) < n_used, blk_exp, last_exp).astype(i32)
    return dest.reshape(-1).astype(i32), blk_exp, n_used.reshape(1), n_rows


def kernel(x_prompt, x_sample, cache_k, cache_v, state_gla, page_table, w_in, w_a2, b_a, lam_q1, lam_k1,
           lam_q2, lam_k2, sub_g, gla_g, w_pa, w_pb, w_out, ln1_g, ln1_b, w_router, b_router, w_gate, w_up,
           w_down, ws_gate, ws_up, ws_down, ln2_g, ln2_b):
    depth = w_in.shape[0]
    batch, seq, _ = x_prompt.shape
    bd = x_sample.shape[0]
    tp = batch * seq
    alpha = (2 * depth) ** 0.25
    n_pool = cache_k.shape[1]
    cache_k4 = cache_k.reshape(depth, n_pool, PAGE_SIZE * A_HEADS, 2 * A_HEAD_DIM)
    cache_v4 = cache_v.reshape(depth, n_pool, PAGE_SIZE * A_HEADS, A_VDIM)

    x = jnp.concatenate([x_prompt.reshape(tp, D_MODEL), x_sample.reshape(bd, D_MODEL)], axis=0)
    kp_l, vp_l, ks_l, vs_l, sp_l, ss_l = [], [], [], [], [], []
    for l in range(depth):
        lam_init = 0.8 - 0.6 * math.exp(-0.3 * l)
        wl = w_in[l]
        w_re = jnp.concatenate([wl[:, :3072], wl[:, 3088:5136], wl[:, 3072:3088],
                                jnp.zeros((D_MODEL, LANES - G_GATE_RANK), f32)], axis=1).astype(bf16)
        wa2 = jnp.concatenate([w_a2[l], jnp.zeros((LANES - G_GATE_RANK, G_K_WIDTH), f32)], axis=0)
        ba = b_a[l].reshape(1, G_K_WIDTH)
        lam_vecs = jnp.stack([lam_q1[l], lam_k1[l], lam_q2[l], lam_k2[l]]).astype(f32)
        subg = sub_g[l].reshape(1, A_VDIM)
        glag = gla_g[l].reshape(1, G_DV)

        wvt = wl[:, C_V:C_GQ].T.astype(bf16)
        q, kf, kb, vf, vt, gq, gk, gv, gr, za, zb, glr = _inproj(x, w_re, wvt)

        oa_p = _attn_prompt(q, kb, vt, lam_vecs, sub_g[l].reshape(A_VDIM, 1), batch, seq, lam_init)
        tail_pad = ((0, 0), (0, 16 - A_HEADS), (0, 0))
        oa_s = _attn_decode(q[tp:].reshape(bd, 1, A_WIDTH),
                            jnp.pad(kf[tp:].reshape(bd, A_HEADS, 2 * A_HEAD_DIM), tail_pad),
                            jnp.pad(vf[tp:].reshape(bd, A_HEADS, A_VDIM), tail_pad),
                            lam_vecs, subg, cache_k4, cache_v4, page_table, l, lam_init)
        og_p, s_p = _gla_prompt(gq, gk, gv, glr, gr, wa2, ba, glag, batch, seq)
        og_s, s_s = _gla_sample(gq[tp:].reshape(bd, 1, -1), gk[tp:].reshape(bd, 1, -1),
                                gv[tp:].reshape(bd, 1, -1), glr[tp:].reshape(bd, 1, -1),
                                gr[tp:].reshape(bd, 1, -1), state_gla[l], wa2, ba, glag)
        oa = jnp.concatenate([oa_p, oa_s.reshape(bd, A_WIDTH)], axis=0)
        og = jnp.concatenate([og_p, og_s.reshape(bd, G_V_WIDTH)], axis=0)

        x1, xp = _postmix(oa, og, za, zb, x, w_pa[l].astype(bf16), w_pb[l].astype(bf16),
                          w_out[l].astype(bf16), ln1_g[l].reshape(1, -1), ln1_b[l].reshape(1, -1), alpha)
        eidx_t, gates_t, rank_t, counts = _router(x1, w_router[l].T, b_router[l].reshape(N_EXPERTS, 1))
        dest, blk_exp, n_used, n_rows = _layout(eidx_t, rank_t, counts.reshape(-1).astype(i32))
        xs = _dispatch(xp, dest, n_rows)
        ys = _experts(xs, blk_exp, n_used, w_gate, w_up, w_down, l)
        x = _combine(dest, gates_t.T, x1, xp, ys, ws_gate[l].astype(bf16), ws_up[l].astype(bf16),
                     ws_down[l].astype(bf16), ln2_g[l].reshape(1, -1), ln2_b[l].reshape(1, -1), alpha)

        kp_l.append(kf[:tp].reshape(batch, seq // PAGE_SIZE, PAGE_SIZE, A_HEADS, 2 * A_HEAD_DIM))
        vp_l.append(vf[:tp].reshape(batch, seq // PAGE_SIZE, PAGE_SIZE, A_HEADS, A_VDIM))
        ks_l.append(kf[tp:].reshape(bd, 1, A_HEADS, 2 * A_HEAD_DIM))
        vs_l.append(vf[tp:].reshape(bd, 1, A_HEADS, A_VDIM))
        sp_l.append(s_p.reshape(batch, G_HEADS, G_DK, G_DV))
        ss_l.append(s_s)

    y_prompt = x[:tp].reshape(batch, seq, D_MODEL)
    y_sample = x[tp:].reshape(bd, 1, D_MODEL)
    return (y_prompt, y_sample, jnp.stack(kp_l), jnp.stack(vp_l), jnp.stack(ks_l), jnp.stack(vs_l),
            jnp.stack(sp_l), jnp.stack(ss_l))
```

```python
import functools
import math

import jax
import jax.numpy as jnp
from jax import lax
from jax.experimental import pallas as pl
from jax.experimental.pallas import tpu as pltpu

f32 = jnp.float32
bf16 = jnp.bfloat16
u32 = jnp.uint32
i32 = jnp.int32

D_MODEL = 1024
A_HEADS = 4
A_HEAD_DIM = 64
A_VDIM = 128
A_WIDTH = A_HEADS * A_VDIM
G_HEADS = 4
G_DK = 64
G_DV = 128
G_K_WIDTH = G_HEADS * G_DK
G_V_WIDTH = G_HEADS * G_DV
G_GATE_RANK = 16
G_TAU = 16.0
N_EXPERTS = 256
TOP_K = 8
N_GROUPS = 8
TOPK_GROUPS = 4
D_EXPERT = 256
ROUTED_SCALE = 2.5
PAGE_SIZE = 128
LN_EPS = 1e-5
RMS_EPS = 1e-6

LANES = 128
VMEM_LIMIT = 56 * 1024 * 1024

NEG_BIG = -1e30
HALF = D_MODEL // 2
EXPERT_BLOCK = 256
GLA_CHUNK = 64
GLA_SUB = 16
ATTN_TQ = 1024
ATTN_TK = 1024
ONES_ROWS = 16
LOG2E = math.log2(math.e)

C_Q, C_K, C_V, C_GQ, C_GK, C_GV, C_GR, C_ZA, C_ZB, C_GLR, C_END = (
    0, 512, 1024, 1536, 1792, 2048, 2560, 3072, 4096, 5120, 5248)


def _params(sem, vmem=VMEM_LIMIT):
    return pltpu.CompilerParams(dimension_semantics=sem, vmem_limit_bytes=vmem)


def _row_tile(n, cands=(512, 384, 256, 128, 64, 32, 16, 8)):
    for c in cands:
        if n % c == 0:
            return c
    raise ValueError(f"no row tile for {n}")


def _sigmoid(x):
    return 1.0 / (1.0 + jnp.exp(-x))


def _pack_rows(x):
    lo = lax.bitcast_convert_type(x[:, :HALF].astype(bf16).astype(f32), u32) >> 16
    hi = lax.bitcast_convert_type(x[:, HALF:].astype(bf16).astype(f32), u32) & jnp.uint32(0xFFFF0000)
    return lo | hi


def _unpack_rows(w):
    lo = lax.bitcast_convert_type(w << 16, f32)
    hi = lax.bitcast_convert_type(w & jnp.uint32(0xFFFF0000), f32)
    return lo, hi


def _layer_norm(h, g, b):
    mu = jnp.mean(h, axis=-1, keepdims=True)
    d = h - mu
    var = jnp.mean(d * d, axis=-1, keepdims=True)
    return d * lax.rsqrt(var + LN_EPS) * g + b


def _rms_norm(o, g):
    return o * lax.rsqrt(jnp.mean(o * o, axis=-1, keepdims=True) + RMS_EPS) * g


def _lam_value(lam_ref, lam_init):
    l = lam_ref[...]
    s1 = jnp.sum(l[0:1] * l[1:2], axis=-1, keepdims=True)
    s2 = jnp.sum(l[2:3] * l[3:4], axis=-1, keepdims=True)
    return jnp.exp(s1) - jnp.exp(s2) + lam_init


def _inproj_body(x_ref, w_ref, wvt_ref, q_o, kf_o, kb_o, vf_o, vt_o, gq_o, gk_o, gv_o, gr_o, za_o, zb_o,
                 glr_o):
    xb = x_ref[...].astype(bf16)

    def mm(c0, c1):
        return jnp.dot(xb, w_ref[:, c0:c1], preferred_element_type=f32)

    q_o[...] = (mm(C_Q, C_K) * (A_HEAD_DIM ** -0.5 * LOG2E)).astype(bf16)
    k = mm(C_K, C_V)
    kf_o[...] = k
    kb_o[...] = k.astype(bf16)
    vf_o[...] = mm(C_V, C_GQ)
    vt_o[...] = lax.dot_general(wvt_ref[...], xb, (((1,), (1,)), ((), ())),
                                preferred_element_type=f32).astype(bf16)
    gq_o[...] = mm(C_GQ, C_GK) * (G_DK ** -0.5)
    gk_o[...] = mm(C_GK, C_GV)
    gv_o[...] = mm(C_GV, C_GR)
    gr_o[...] = mm(C_GR, C_ZA)
    za_o[...] = mm(C_ZA, C_ZB).astype(bf16)
    zb_o[...] = mm(C_ZB, C_GLR).astype(bf16)
    glr_o[...] = mm(C_GLR, C_END)


def _inproj(x, w, wvt):
    t = x.shape[0]
    tm = _row_tile(t, (384, 256, 128))
    outs = [(512, bf16), (512, f32), (512, bf16), (512, f32), None, (256, f32), (256, f32),
            (512, f32), (512, f32), (1024, bf16), (1024, bf16), (LANES, f32)]
    out_specs = [pl.BlockSpec((A_WIDTH, tm), lambda i: (0, i)) if o is None
                 else pl.BlockSpec((tm, o[0]), lambda i: (i, 0)) for o in outs]
    out_shape = [jax.ShapeDtypeStruct((A_WIDTH, t), bf16) if o is None
                 else jax.ShapeDtypeStruct((t, o[0]), o[1]) for o in outs]
    return pl.pallas_call(
        _inproj_body,
        grid=(t // tm,),
        in_specs=[pl.BlockSpec((tm, D_MODEL), lambda i: (i, 0)),
                  pl.BlockSpec((D_MODEL, C_END), lambda i: (0, 0)),
                  pl.BlockSpec((A_WIDTH, D_MODEL), lambda i: (0, 0))],
        out_specs=out_specs,
        out_shape=out_shape,
        compiler_params=_params(("parallel",)),
    )(x, w, wvt)


def _attn_body(qi_tab, kj_tab, diag_tab, last_tab, q_ref, k_ref, vt_ref, lam_ref, subg_ref, o_ref,
               m1, a1, m2, a2, *, tq, tk, lam_init):
    p = pl.program_id(2)
    qi = qi_tab[p]
    kj = kj_tab[p]

    @pl.when(kj == 0)
    def _():
        for m, a in ((m1, a1), (m2, a2)):
            m[...] = jnp.full(m.shape, NEG_BIG, f32)
            a[...] = jnp.zeros(a.shape, f32)

    q = q_ref[...]
    k = k_ref[...]
    vt = jnp.concatenate([vt_ref[...], jnp.ones((ONES_ROWS, tk), bf16)], axis=0)
    lane = lax.broadcasted_iota(i32, (1, LANES), 1)
    zero = jnp.zeros_like(q)
    q1 = jnp.where(lane < A_HEAD_DIM, q, zero)
    q2 = jnp.where(lane >= A_HEAD_DIM, q, zero)

    def step(masked):
        for qm, m, a in ((q1, m1, a1), (q2, m2, a2)):
            s = lax.dot_general(k, qm, (((1,), (1,)), ((), ())), preferred_element_type=f32)
            if masked:
                kpos = kj * tk + lax.broadcasted_iota(i32, (tk, tq), 0)
                qpos = qi * tq + lax.broadcasted_iota(i32, (tk, tq), 1)
                s = jnp.where(kpos <= qpos, s, NEG_BIG)
            m_prev = m[...]
            m_new = jnp.maximum(m_prev, jnp.max(s, axis=0, keepdims=True))
            alpha = jnp.exp2(m_prev - m_new)
            pr = jnp.exp2(s - m_new).astype(bf16)
            a[...] = alpha * a[...] + jnp.dot(vt, pr, preferred_element_type=f32)
            m[...] = m_new

    @pl.when(diag_tab[p] == 1)
    def _():
        step(True)

    @pl.when(diag_tab[p] == 0)
    def _():
        step(False)

    @pl.when(last_tab[p] == 1)
    def _():
        lam = _lam_value(lam_ref, lam_init)
        ot = (a1[:A_VDIM, :] / a1[A_VDIM:A_VDIM + 1, :]
              - lam * (a2[:A_VDIM, :] / a2[A_VDIM:A_VDIM + 1, :]))
        ms = jnp.mean(ot * ot, axis=0, keepdims=True)
        on = ot * lax.rsqrt(ms + RMS_EPS) * subg_ref[...] * (1.0 - lam_init)
        o_ref[...] = on.T.astype(o_ref.dtype)


def _attn_prompt(q, k, vt, lam_vecs, sub_g_col, batch, seq, lam_init):
    tq = min(ATTN_TQ, seq)
    tk = min(ATTN_TK, seq)
    nq, nk = seq // tq, seq // tk
    qi_l, kj_l, dg_l, ls_l = [], [], [], []
    for qi in range(nq):
        last = ((qi + 1) * tq - 1) // tk
        for kj in range(last + 1):
            qi_l.append(qi)
            kj_l.append(kj)
            dg_l.append(1 if (kj + 1) * tk - 1 > qi * tq else 0)
            ls_l.append(1 if kj == last else 0)
    tabs = [jnp.asarray(t, i32) for t in (qi_l, kj_l, dg_l, ls_l)]
    n_pairs = len(qi_l)
    body = functools.partial(_attn_body, tq=tq, tk=tk, lam_init=lam_init)
    grid_spec = pltpu.PrefetchScalarGridSpec(
        num_scalar_prefetch=4,
        grid=(batch, A_HEADS, n_pairs),
        in_specs=[
            pl.BlockSpec((tq, LANES), lambda b, h, p, qt, kt, dt, lt: (b * nq + qt[p], h)),
            pl.BlockSpec((tk, LANES), lambda b, h, p, qt, kt, dt, lt: (b * nk + kt[p], h)),
            pl.BlockSpec((A_VDIM, tk), lambda b, h, p, qt, kt, dt, lt: (h, b * nk + kt[p])),
            pl.BlockSpec((4, A_HEAD_DIM), lambda b, h, p, *_: (0, 0)),
            pl.BlockSpec((A_VDIM, 1), lambda b, h, p, *_: (0, 0)),
        ],
        out_specs=pl.BlockSpec((tq, LANES), lambda b, h, p, qt, kt, dt, lt: (b * nq + qt[p], h)),
        scratch_shapes=[pltpu.VMEM((1, tq), f32), pltpu.VMEM((A_VDIM + ONES_ROWS, tq), f32),
                        pltpu.VMEM((1, tq), f32), pltpu.VMEM((A_VDIM + ONES_ROWS, tq), f32)],
    )
    return pl.pallas_call(
        body,
        grid_spec=grid_spec,
        out_shape=jax.ShapeDtypeStruct((batch * seq, A_WIDTH), bf16),
        compiler_params=_params(("parallel", "parallel", "arbitrary")),
    )(*tabs, q, k, vt, lam_vecs, sub_g_col)


def _decode_body(pt_ref, q_ref, kn_ref, vn_ref, lam_ref, subg_ref, *refs, n_pages, lam_init):
    k_refs = refs[:n_pages]
    v_refs = refs[n_pages:2 * n_pages]
    o_ref = refs[2 * n_pages]
    kbuf, vbuf = refs[2 * n_pages + 1:]
    rows_pg = PAGE_SIZE * A_HEADS
    past = n_pages * rows_pg
    tail = 16
    n_col = past + tail

    for p in range(n_pages):
        kbuf[p * rows_pg:(p + 1) * rows_pg, :] = k_refs[p][...].astype(bf16)
        vbuf[p * rows_pg:(p + 1) * rows_pg, :] = v_refs[p][...].astype(bf16)
    kbuf[past:, :] = kn_ref[0].astype(bf16)
    vbuf[past:, :] = vn_ref[0].astype(bf16)

    qrow = q_ref[0].astype(f32)
    row = lax.broadcasted_iota(i32, (16, LANES), 0)
    lane = lax.broadcasted_iota(i32, (16, LANES), 1)
    qmat = jnp.zeros((16, LANES), f32)
    for h in range(A_HEADS):
        qh = jnp.broadcast_to(qrow[:, h * LANES:(h + 1) * LANES], (16, LANES))
        sel = ((row >> 1) == h) & ((lane >= A_HEAD_DIM) == ((row & 1) == 1))
        qmat = jnp.where(sel, qh, qmat)
    s = lax.dot_general(qmat.astype(bf16), kbuf[...], (((1,), (1,)), ((), ())), preferred_element_type=f32)
    srow = lax.broadcasted_iota(i32, (16, n_col), 0)
    scol = lax.broadcasted_iota(i32, (16, n_col), 1)
    valid = ((scol & (A_HEADS - 1)) == (srow >> 1)) & (scol < past + A_HEADS) & (srow < 2 * A_HEADS)
    s = jnp.where(valid, s, NEG_BIG)
    m = jnp.max(s, axis=-1, keepdims=True)
    pr = jnp.where(valid, jnp.exp2(s - m), 0.0)
    den = jnp.maximum(jnp.sum(pr, axis=-1, keepdims=True), 1e-30)
    pn = pr / den
    o8 = jnp.dot(pn.astype(bf16), vbuf[...], preferred_element_type=f32)
    lam = _lam_value(lam_ref, lam_init)
    for h in range(A_HEADS):
        o = o8[2 * h:2 * h + 1, :] - lam * o8[2 * h + 1:2 * h + 2, :]
        o_ref[0, :, h * LANES:(h + 1) * LANES] = (
            _rms_norm(o, subg_ref[...]) * (1.0 - lam_init)).astype(o_ref.dtype)


def _attn_decode(q_s, k_new, v_new, lam_vecs, sub_g, cache_k4, cache_v4, page_table, layer, lam_init):
    bd, n_pages = page_table.shape
    rows_pg = PAGE_SIZE * A_HEADS
    body = functools.partial(_decode_body, n_pages=n_pages, lam_init=lam_init)

    def page_spec(p):
        return pl.BlockSpec((None, None, rows_pg, LANES),
                            lambda b, pt, p=p: (layer, pt[b * n_pages + p], 0, 0))

    grid_spec = pltpu.PrefetchScalarGridSpec(
        num_scalar_prefetch=1,
        grid=(bd,),
        in_specs=[pl.BlockSpec((1, 1, A_WIDTH), lambda b, pt: (b, 0, 0)),
                  pl.BlockSpec((1, 16, LANES), lambda b, pt: (b, 0, 0)),
                  pl.BlockSpec((1, 16, LANES), lambda b, pt: (b, 0, 0)),
                  pl.BlockSpec((4, A_HEAD_DIM), lambda b, pt: (0, 0)),
                  pl.BlockSpec((1, A_VDIM), lambda b, pt: (0, 0))]
        + [page_spec(p) for p in range(n_pages)] * 2,
        out_specs=pl.BlockSpec((1, 1, A_WIDTH), lambda b, pt: (b, 0, 0)),
        scratch_shapes=[pltpu.VMEM((n_pages * rows_pg + 16, LANES), bf16),
                        pltpu.VMEM((n_pages * rows_pg + 16, LANES), bf16)],
    )
    return pl.pallas_call(
        body,
        grid_spec=grid_spec,
        out_shape=jax.ShapeDtypeStruct((bd, 1, A_WIDTH), bf16),
        compiler_params=_params(("arbitrary",)),
    )(page_table.reshape(-1), q_s, k_new, v_new, lam_vecs, sub_g,
      *([cache_k4] * n_pages), *([cache_v4] * n_pages))


def _log_decay(glr, wa2, ba):
    z = jnp.dot(glr, wa2, preferred_element_type=f32, precision=lax.Precision.HIGHEST) + ba
    return (jnp.minimum(z, 0.0) - jnp.log(1.0 + jnp.exp(-jnp.abs(z)))) * (1.0 / G_TAU)


def _gla_finish(o, gr, g):
    return _rms_norm(o, g) * (gr * _sigmoid(gr))


def _gla_prompt_body(q_ref, k_ref, v_ref, glr_ref, gr_ref, wa2_ref, ba_ref, g_ref, o_ref, s_ref, state, tmp,
                     *, c):
    ci = pl.program_id(1)
    nsub = c // GLA_SUB

    @pl.when(ci == 0)
    def _():
        state[...] = jnp.zeros(state.shape, f32)

    la = _log_decay(glr_ref[...], wa2_ref[...], ba_ref[...])
    ri = lax.broadcasted_iota(i32, (c, c), 0)
    cj = lax.broadcasted_iota(i32, (c, c), 1)
    b = jnp.dot((ri >= cj).astype(f32), la, preferred_element_type=f32, precision=lax.Precision.HIGHEST)
    b_last = b[c - 1:c, :]
    q = q_ref[...]
    k = k_ref[...]
    v = v_ref[...]
    vb = v.astype(bf16)
    lane = lax.broadcasted_iota(i32, (1, G_K_WIDTH), 1)
    heads = [(lane >> 6) == h for h in range(G_HEADS)]

    def stack_heads(x):
        return jnp.concatenate([jnp.where(m, x, 0.0) for m in heads], axis=0).astype(bf16)

    sub_i = lax.broadcasted_iota(i32, (GLA_SUB, G_K_WIDTH), 0)
    for blk in range(nsub):
        r0 = blk * GLA_SUB
        q_b = q[r0:r0 + GLA_SUB]
        b_b = b[r0:r0 + GLA_SUB]
        for j in range(GLA_SUB):
            w = jnp.where(sub_i >= j, jnp.exp(jnp.minimum(b_b - b_b[j:j + 1], 0.0)), 0.0)
            t0 = (r0 + j) * GLA_SUB
            tmp[t0:t0 + GLA_SUB, :] = (q_b * w * k[r0 + j:r0 + j + 1]).astype(bf16)
    si = lax.broadcasted_iota(i32, (G_K_WIDTH, G_V_WIDTH), 0)
    sj = lax.broadcasted_iota(i32, (G_K_WIDTH, G_V_WIDTH), 1)
    seg = ((si >> 6) == (sj >> 7)).astype(bf16)
    pair = jnp.dot(tmp[...], seg, preferred_element_type=f32)
    o_rows = []
    for blk in range(nsub):
        r0 = blk * GLA_SUB
        acc = jnp.zeros((GLA_SUB, G_V_WIDTH), f32)
        for j in range(GLA_SUB):
            t0 = (r0 + j) * GLA_SUB
            acc = acc + pair[t0:t0 + GLA_SUB, :] * v[r0 + j:r0 + j + 1]
        o_rows.append(acc)
    o_diag = jnp.concatenate(o_rows, axis=0)

    att_rows = [[jnp.zeros((GLA_SUB, c), f32)] for _ in range(G_HEADS)]
    col = lax.broadcasted_iota(i32, (G_HEADS * GLA_SUB, c), 1)
    for blk in range(1, nsub):
        r0 = blk * GLA_SUB
        ref = b[r0 - 1:r0]
        q_b = q[r0:r0 + GLA_SUB] * jnp.exp(b[r0:r0 + GLA_SUB] - ref)
        k_b = (k * jnp.exp(jnp.minimum(ref - b, 0.0))).astype(bf16)
        a = lax.dot_general(stack_heads(q_b), k_b, (((1,), (1,)), ((), ())), preferred_element_type=f32)
        a = jnp.where(col < r0, a, 0.0)
        for h in range(G_HEADS):
            att_rows[h].append(a[h * GLA_SUB:(h + 1) * GLA_SUB])

    st = state[...]
    o_inter = jnp.dot(stack_heads(q * jnp.exp(b)), st.astype(bf16), preferred_element_type=f32)
    kdt = (k * jnp.exp(b_last - b)).T.astype(bf16)
    upd = jnp.dot(kdt, vb, preferred_element_type=f32)
    dec_col = jnp.exp(jnp.broadcast_to(b_last, (8, G_K_WIDTH)).T[:, 0:1])
    for h in range(G_HEADS):
        cols = slice(h * G_DV, (h + 1) * G_DV)
        att = jnp.concatenate(att_rows[h], axis=0).astype(bf16)
        o = (o_diag[:, cols] + jnp.dot(att, vb[:, cols], preferred_element_type=f32)
             + o_inter[h * c:(h + 1) * c])
        o_ref[:, cols] = _gla_finish(o, gr_ref[:, cols], g_ref[...]).astype(o_ref.dtype)
        r0 = h * G_DK
        state[r0:r0 + G_DK, :] = dec_col[r0:r0 + G_DK] * st[r0:r0 + G_DK, :] + upd[r0:r0 + G_DK, cols]

    @pl.when(ci == pl.num_programs(1) - 1)
    def _():
        s_ref[0] = state[...]


def _gla_prompt(gq, gk, gv, glr, gr, wa2, ba, gla_g, batch, seq):
    c = math.gcd(seq, GLA_CHUNK)
    n = seq // c
    body = functools.partial(_gla_prompt_body, c=c)

    def tok(width):
        return pl.BlockSpec((c, width), lambda b, i: (b * n + i, 0))

    def whole(shape):
        return pl.BlockSpec(shape, lambda b, i: (0,) * len(shape))

    return pl.pallas_call(
        body,
        grid=(batch, n),
        in_specs=[tok(G_K_WIDTH), tok(G_K_WIDTH), tok(G_V_WIDTH), tok(LANES), tok(G_V_WIDTH),
                  whole((LANES, G_K_WIDTH)), whole((1, G_K_WIDTH)), whole((1, G_DV))],
        out_specs=[tok(G_V_WIDTH),
                   pl.BlockSpec((1, G_K_WIDTH, G_DV), lambda b, i: (b, 0, 0))],
        out_shape=[jax.ShapeDtypeStruct((batch * seq, G_V_WIDTH), bf16),
                   jax.ShapeDtypeStruct((batch, G_K_WIDTH, G_DV), f32)],
        scratch_shapes=[pltpu.VMEM((G_K_WIDTH, G_DV), f32),
                        pltpu.VMEM((c * GLA_SUB, G_K_WIDTH), bf16)],
        compiler_params=_params(("parallel", "arbitrary")),
    )(gq, gk, gv, glr, gr, wa2, ba, gla_g)


def _gla_sample_body(q_ref, k_ref, v_ref, glr_ref, gr_ref, s_ref, wa2_ref, ba_ref, g_ref, o_ref, sn_ref):
    la = _log_decay(glr_ref[0], wa2_ref[...], ba_ref[...])
    rows = jnp.concatenate([jnp.broadcast_to(jnp.exp(la), (G_DK, G_K_WIDTH)),
                            jnp.broadcast_to(k_ref[0], (G_DK, G_K_WIDTH)),
                            jnp.broadcast_to(q_ref[0], (G_DK, G_K_WIDTH))], axis=0)
    ri = lax.broadcasted_iota(i32, (3 * G_DK, G_K_WIDTH), 0)
    li = lax.broadcasted_iota(i32, (3 * G_DK, G_K_WIDTH), 1)
    picked = jnp.where((ri & (G_DK - 1)) == (li & (G_DK - 1)), rows, 0.0)
    si = lax.broadcasted_iota(i32, (G_K_WIDTH, G_V_WIDTH), 0)
    sj = lax.broadcasted_iota(i32, (G_K_WIDTH, G_V_WIDTH), 1)
    seg = ((si >> 6) == (sj >> 7)).astype(f32)
    cols = jnp.dot(picked, seg, preferred_element_type=f32, precision=lax.Precision.HIGHEST)
    v = v_ref[0]
    for h in range(G_HEADS):
        sl = slice(h * G_DV, (h + 1) * G_DV)
        a_c = cols[0:G_DK, sl]
        k_c = cols[G_DK:2 * G_DK, sl]
        q_c = cols[2 * G_DK:3 * G_DK, sl]
        s_new = a_c * s_ref[0, h] + k_c * v[:, sl]
        sn_ref[0, h] = s_new
        o = jnp.sum(q_c * s_new, axis=0, keepdims=True)
        o_ref[0, :, sl] = _gla_finish(o, gr_ref[0][:, sl], g_ref[...]).astype(o_ref.dtype)


def _gla_sample(gq, gk, gv, glr, gr, s0, wa2, ba, gla_g):
    bd = gq.shape[0]

    def vec(width):
        return pl.BlockSpec((1, 1, width), lambda b: (b, 0, 0))

    def whole(shape):
        return pl.BlockSpec(shape, lambda b: (0,) * len(shape))

    st = pl.BlockSpec((1, G_HEADS, G_DK, G_DV), lambda b: (b, 0, 0, 0))
    return pl.pallas_call(
        _gla_sample_body,
        grid=(bd,),
        in_specs=[vec(G_K_WIDTH), vec(G_K_WIDTH), vec(G_V_WIDTH), vec(LANES), vec(G_V_WIDTH), st,
                  whole((LANES, G_K_WIDTH)), whole((1, G_K_WIDTH)), whole((1, G_DV))],
        out_specs=[vec(G_V_WIDTH), st],
        out_shape=[jax.ShapeDtypeStruct((bd, 1, G_V_WIDTH), bf16),
                   jax.ShapeDtypeStruct((bd, G_HEADS, G_DK, G_DV), f32)],
        compiler_params=_params(("parallel",)),
    )(gq, gk, gv, glr, gr, s0, wa2, ba, gla_g)


def _postmix_body(oa_ref, og_ref, za_ref, zb_ref, x_ref, wpa_ref, wpb_ref, wout_ref, g_ref, b_ref,
                  x1_o, xp_o, *, alpha):
    ya = jnp.dot(oa_ref[...], wpa_ref[...], preferred_element_type=f32)
    yb = jnp.dot(og_ref[...], wpb_ref[...], preferred_element_type=f32)
    merged = _sigmoid(za_ref[...].astype(f32)) * ya + _sigmoid(zb_ref[...].astype(f32)) * yb
    mix = jnp.dot(merged.astype(bf16), wout_ref[...], preferred_element_type=f32)
    x1 = _layer_norm(alpha * x_ref[...] + mix, g_ref[...], b_ref[...])
    x1_o[...] = x1
    xp_o[...] = _pack_rows(x1)


def _postmix(oa, og, za, zb, x, wpa, wpb, wout, g, b, alpha):
    t = x.shape[0]
    tm = _row_tile(t, (384, 256, 128, 64, 32, 16, 8))

    def tok(width):
        return pl.BlockSpec((tm, width), lambda i: (i, 0))

    def whole(shape):
        return pl.BlockSpec(shape, lambda i: (0,) * len(shape))

    return pl.pallas_call(
        functools.partial(_postmix_body, alpha=alpha),
        grid=(t // tm,),
        in_specs=[tok(A_WIDTH), tok(G_V_WIDTH), tok(D_MODEL), tok(D_MODEL), tok(D_MODEL),
                  whole((A_WIDTH, D_MODEL)), whole((G_V_WIDTH, D_MODEL)), whole((D_MODEL, D_MODEL)),
                  whole((1, D_MODEL)), whole((1, D_MODEL))],
        out_specs=[tok(D_MODEL), tok(HALF)],
        out_shape=[jax.ShapeDtypeStruct((t, D_MODEL), f32),
                   jax.ShapeDtypeStruct((t, HALF), u32)],
        compiler_params=_params(("parallel",)),
    )(oa, og, za, zb, x, wpa, wpb, wout, g, b)


def _router_body(x1_ref, wrt_ref, bcol_ref, tri_ref, eidx_o, gate_o, rank_o, cnt_o, cnt, *, tm):
    @pl.when(pl.program_id(0) == 0)
    def _():
        cnt[...] = jnp.zeros(cnt.shape, f32)

    logits = lax.dot_general(wrt_ref[...], x1_ref[...], (((1,), (1,)), ((), ())),
                             preferred_element_type=f32, precision=lax.Precision.HIGHEST)
    scores = _sigmoid(logits)
    biased = scores + bcol_ref[...]
    gsz = N_EXPERTS // N_GROUPS
    neg_inf = -jnp.inf

    gi = lax.broadcasted_iota(i32, (gsz, tm), 0)
    segs, gscore = [], []
    for g in range(N_GROUPS):
        seg = biased[g * gsz:(g + 1) * gsz, :]
        m1 = jnp.max(seg, axis=0, keepdims=True)
        i1 = jnp.min(jnp.where(seg == m1, gi, gsz), axis=0, keepdims=True)
        m2 = jnp.max(jnp.where(gi == i1, neg_inf, seg), axis=0, keepdims=True)
        segs.append(seg)
        gscore.append(m1 + m2)
    parts = []
    for g in range(N_GROUPS):
        beat = jnp.zeros((1, tm), i32)
        for o in range(N_GROUPS):
            if o != g:
                wins = (gscore[o] > gscore[g]) | ((gscore[o] == gscore[g]) & (o < g))
                beat = beat + wins.astype(i32)
        parts.append(jnp.where(beat < TOPK_GROUPS, segs[g], neg_inf))
    masked = jnp.concatenate(parts, axis=0)

    ei = lax.broadcasted_iota(i32, (N_EXPERTS, tm), 0)
    sel_rows, idx_rows = [], []
    chosen = jnp.zeros((N_EXPERTS, tm), f32)
    for _ in range(TOP_K):
        m = jnp.max(masked, axis=0, keepdims=True)
        idx = jnp.min(jnp.where(masked == m, ei, N_EXPERTS), axis=0, keepdims=True)
        hit = ei == idx
        sel_rows.append(jnp.sum(jnp.where(hit, scores, 0.0), axis=0, keepdims=True))
        idx_rows.append(idx)
        chosen = jnp.where(hit, 1.0, chosen)
        masked = jnp.where(hit, neg_inf, masked)
    s_sel = jnp.concatenate(sel_rows, axis=0)
    gate_o[...] = s_sel / jnp.sum(s_sel, axis=0, keepdims=True) * ROUTED_SCALE
    eidx_o[...] = jnp.concatenate(idx_rows, axis=0)
    before = jnp.dot(chosen.astype(bf16), tri_ref[...], preferred_element_type=f32) + cnt[...]
    rank_rows = [jnp.sum(jnp.where(ei == idx, before, 0.0), axis=0, keepdims=True) for idx in idx_rows]
    rank_o[...] = jnp.concatenate(rank_rows, axis=0).astype(i32)
    cnt[...] = cnt[...] + jnp.sum(chosen, axis=1, keepdims=True)
    cnt_o[...] = cnt[...]


def _router(x1, wrt, b_col):
    t = x1.shape[0]
    tm = _row_tile(t, (384, 256, 128))
    tri = (jnp.arange(tm)[:, None] < jnp.arange(tm)[None, :]).astype(bf16)

    def tokcol(dt):
        return pl.BlockSpec((TOP_K, tm), lambda i: (0, i)), jax.ShapeDtypeStruct((TOP_K, t), dt)

    def whole(shape):
        return pl.BlockSpec(shape, lambda i: (0,) * len(shape))

    specs, shapes = zip(tokcol(i32), tokcol(f32), tokcol(i32),
                        (whole((N_EXPERTS, 1)), jax.ShapeDtypeStruct((N_EXPERTS, 1), f32)))
    return pl.pallas_call(
        functools.partial(_router_body, tm=tm),
        grid=(t // tm,),
        in_specs=[pl.BlockSpec((tm, D_MODEL), lambda i: (i, 0)), whole((N_EXPERTS, D_MODEL)),
                  whole((N_EXPERTS, 1)), whole((tm, tm))],
        out_specs=list(specs),
        out_shape=list(shapes),
        scratch_shapes=[pltpu.VMEM((N_EXPERTS, 1), f32)],
        compiler_params=_params(("arbitrary",)),
    )(x1, wrt, b_col, tri)


def _row_copy(src, src_row, dst, dst_row, sem):
    return pltpu.make_async_copy(src.at[pl.ds(src_row, 1)], dst.at[pl.ds(dst_row, 1)], sem)


def _dispatch_body(dest_ref, x_ref, xs_in, xs_out, sem, *, tm):
    del xs_in

    def issue(t, carry):
        for kk in range(TOP_K):
            _row_copy(x_ref, t, xs_out, dest_ref[kk, t], sem).start()
        return carry

    lax.fori_loop(0, tm, issue, 0)

    def drain(t, carry):
        for kk in range(TOP_K):
            _row_copy(x_ref, 0, xs_out, 0, sem).wait()
        return carry

    lax.fori_loop(0, tm, drain, 0)


def _dispatch(xp, dest, n_rows):
    t = xp.shape[0]
    tm = _row_tile(t, (384, 256, 128, 64, 32, 16, 8))
    return pl.pallas_call(
        functools.partial(_dispatch_body, tm=tm),
        grid=(t // tm,),
        in_specs=[pl.BlockSpec((TOP_K, tm), lambda i: (0, i), memory_space=pltpu.SMEM),
                  pl.BlockSpec((tm, HALF), lambda i: (i, 0)),
                  pl.BlockSpec(memory_space=pl.ANY)],
        out_specs=pl.BlockSpec(memory_space=pl.ANY),
        out_shape=jax.ShapeDtypeStruct((n_rows, HALF), u32),
        scratch_shapes=[pltpu.SemaphoreType.DMA(())],
        input_output_aliases={2: 0},
        compiler_params=_params(("arbitrary",)),
    )(dest, xp, jnp.zeros((n_rows, HALF), u32))


def _expert_body(be_ref, nu_ref, xs_ref, wg_ref, wu_ref, wd_ref, ys_ref, wgb, wub, wdb):
    i = pl.program_id(0)
    e = be_ref[i]
    prev = be_ref[jnp.maximum(i - 1, 0)]

    @pl.when((i == 0) | (e != prev))
    def _():
        wgb[...] = wg_ref[...].astype(bf16)
        wub[...] = wu_ref[...].astype(bf16)
        wdb[...] = wd_ref[...].astype(bf16)

    @pl.when(i < nu_ref[0])
    def _():
        lo, hi = _unpack_rows(xs_ref[...])
        lo = lo.astype(bf16)
        hi = hi.astype(bf16)
        g = (jnp.dot(lo, wgb[:HALF, :], preferred_element_type=f32)
             + jnp.dot(hi, wgb[HALF:, :], preferred_element_type=f32))
        u = (jnp.dot(lo, wub[:HALF, :], preferred_element_type=f32)
             + jnp.dot(hi, wub[HALF:, :], preferred_element_type=f32))
        hdn = (g * _sigmoid(g) * u).astype(bf16)
        ys_ref[...] = _pack_rows(jnp.dot(hdn, wdb[...], preferred_element_type=f32))

    @pl.when(i >= nu_ref[0])
    def _():
        ys_ref[...] = jnp.zeros(ys_ref.shape, u32)


def _experts(xs, blk_exp, n_used, w_gate, w_up, w_down, layer):
    n_rows = xs.shape[0]
    n_blocks = n_rows // EXPERT_BLOCK
    grid_spec = pltpu.PrefetchScalarGridSpec(
        num_scalar_prefetch=2,
        grid=(n_blocks,),
        in_specs=[
            pl.BlockSpec((EXPERT_BLOCK, HALF), lambda i, be, nu: (jnp.minimum(i, nu[0] - 1), 0)),
            pl.BlockSpec((None, None, D_MODEL, D_EXPERT), lambda i, be, nu: (layer, be[i], 0, 0)),
            pl.BlockSpec((None, None, D_MODEL, D_EXPERT), lambda i, be, nu: (layer, be[i], 0, 0)),
            pl.BlockSpec((None, None, D_EXPERT, D_MODEL), lambda i, be, nu: (layer, be[i], 0, 0)),
        ],
        out_specs=pl.BlockSpec((EXPERT_BLOCK, HALF), lambda i, be, nu: (i, 0)),
        scratch_shapes=[pltpu.VMEM((D_MODEL, D_EXPERT), bf16), pltpu.VMEM((D_MODEL, D_EXPERT), bf16),
                        pltpu.VMEM((D_EXPERT, D_MODEL), bf16)],
    )
    return pl.pallas_call(
        _expert_body,
        grid_spec=grid_spec,
        out_shape=jax.ShapeDtypeStruct((n_rows, HALF), u32),
        compiler_params=_params(("arbitrary",)),
    )(blk_exp, n_used, xs, w_gate, w_up, w_down)


def _combine_body(dest_ref, gate_ref, x1_ref, xp_ref, ys_hbm, wsg_ref, wsu_ref, wsd_ref, g_ref, b_ref,
                  o_ref, buf, sem, *, tm, alpha):
    def issue(t, carry):
        for kk in range(TOP_K):
            _row_copy(ys_hbm, dest_ref[kk, t], buf.at[kk], t, sem).start()
        return carry

    lax.fori_loop(0, tm, issue, 0)

    lo, hi = _unpack_rows(xp_ref[...])
    lo = lo.astype(bf16)
    hi = hi.astype(bf16)
    sg = (jnp.dot(lo, wsg_ref[:HALF, :], preferred_element_type=f32)
          + jnp.dot(hi, wsg_ref[HALF:, :], preferred_element_type=f32))
    su = (jnp.dot(lo, wsu_ref[:HALF, :], preferred_element_type=f32)
          + jnp.dot(hi, wsu_ref[HALF:, :], preferred_element_type=f32))
    shared = jnp.dot((sg * _sigmoid(sg) * su).astype(bf16), wsd_ref[...], preferred_element_type=f32)

    def drain(t, carry):
        for kk in range(TOP_K):
            _row_copy(ys_hbm, 0, buf.at[kk], 0, sem).wait()
        return carry

    lax.fori_loop(0, tm, drain, 0)

    gates = gate_ref[...]
    acc_lo = jnp.zeros((tm, HALF), f32)
    acc_hi = jnp.zeros((tm, HALF), f32)
    for kk in range(TOP_K):
        ylo, yhi = _unpack_rows(buf[kk])
        gk = gates[:, kk:kk + 1]
        acc_lo = acc_lo + gk * ylo
        acc_hi = acc_hi + gk * yhi
    moe = jnp.concatenate([acc_lo, acc_hi], axis=-1) + shared
    o_ref[...] = _layer_norm(alpha * x1_ref[...] + moe, g_ref[...], b_ref[...])


def _combine(dest, gates, x1, xp, ys, wsg, wsu, wsd, g, b, alpha):
    t = x1.shape[0]
    tm = _row_tile(t, (384, 256, 128, 64, 32, 16, 8))

    def tok(width):
        return pl.BlockSpec((tm, width), lambda i: (i, 0))

    def whole(shape):
        return pl.BlockSpec(shape, lambda i: (0,) * len(shape))

    return pl.pallas_call(
        functools.partial(_combine_body, tm=tm, alpha=alpha),
        grid=(t // tm,),
        in_specs=[pl.BlockSpec((TOP_K, tm), lambda i: (0, i), memory_space=pltpu.SMEM),
                  tok(TOP_K), tok(D_MODEL), tok(HALF),
                  pl.BlockSpec(memory_space=pl.ANY),
                  whole((D_MODEL, D_EXPERT)), whole((D_MODEL, D_EXPERT)), whole((D_EXPERT, D_MODEL)),
                  whole((1, D_MODEL)), whole((1, D_MODEL))],
        out_specs=tok(D_MODEL),
        out_shape=jax.ShapeDtypeStruct((t, D_MODEL), f32),
        scratch_shapes=[pltpu.VMEM((TOP_K, tm, HALF), u32), pltpu.SemaphoreType.DMA(())],
        compiler_params=_params(("arbitrary",)),
    )(dest, gates, x1, xp, ys, wsg, wsu, wsd, g, b)


def _dest_body(eidx_ref, rank_ref, ps_ref, dest_o, *, tm):
    ei = lax.broadcasted_iota(i32, (N_EXPERTS, tm), 0)
    ps = ps_ref[...]
    rows = [jnp.sum(jnp.where(ei == eidx_ref[kk:kk + 1, :], ps, 0.0), axis=0, keepdims=True)
            for kk in range(TOP_K)]
    dest_o[...] = jnp.concatenate(rows, axis=0).astype(i32) + rank_ref[...]


def _dest_rows(eidx_t, rank_t, pad_start):
    t = eidx_t.shape[1]
    tm = _row_tile(t, (384, 256, 128))
    blk = pl.BlockSpec((TOP_K, tm), lambda i: (0, i))
    return pl.pallas_call(
        functools.partial(_dest_body, tm=tm),
        grid=(t // tm,),
        in_specs=[blk, blk, pl.BlockSpec((N_EXPERTS, 1), lambda i: (0, 0))],
        out_specs=blk,
        out_shape=jax.ShapeDtypeStruct((TOP_K, t), i32),
        compiler_params=_params(("parallel",)),
    )(eidx_t, rank_t, pad_start.astype(f32).reshape(N_EXPERTS, 1))


def _layout(eidx_t, rank_t, counts):
    t = eidx_t.shape[1]
    blk = EXPERT_BLOCK
    padded = (counts + blk - 1) // blk * blk
    pad_end = jnp.cumsum(padded)
    pad_start = pad_end - padded
    dest = _dest_rows(eidx_t, rank_t, pad_start)
    n_rows = -(-(t * TOP_K + N_EXPERTS * (blk - 1)) // blk) * blk
    n_blocks = n_rows // blk
    n_used = (pad_end[-1] // blk).astype(i32)
    first_row = jnp.minimum(jnp.arange(n_blocks), n_used - 1) * blk
    blk_exp = jnp.sum(pad_end[None, :] <= first_row[:, None], axis=1).astype(i32)
    blk_exp = jnp.minimum(blk_exp, N_EXPERTS - 1)
    return dest, blk_exp, n_used.reshape(1), n_rows


def kernel(x_prompt, x_sample, cache_k, cache_v, state_gla, page_table, w_in, w_a2, b_a, lam_q1, lam_k1,
           lam_q2, lam_k2, sub_g, gla_g, w_pa, w_pb, w_out, ln1_g, ln1_b, w_router, b_router, w_gate, w_up,
           w_down, ws_gate, ws_up, ws_down, ln2_g, ln2_b):
    depth = w_in.shape[0]
    batch, seq, _ = x_prompt.shape
    bd = x_sample.shape[0]
    tp = batch * seq
    alpha = (2 * depth) ** 0.25
    n_pool = cache_k.shape[1]
    cache_k4 = cache_k.reshape(depth, n_pool, PAGE_SIZE * A_HEADS, 2 * A_HEAD_DIM)
    cache_v4 = cache_v.reshape(depth, n_pool, PAGE_SIZE * A_HEADS, A_VDIM)

    x = jnp.concatenate([x_prompt.reshape(tp, D_MODEL), x_sample.reshape(bd, D_MODEL)], axis=0)
    kp_l, vp_l, ks_l, vs_l, sp_l, ss_l = [], [], [], [], [], []
    for l in range(depth):
        lam_init = 0.8 - 0.6 * math.exp(-0.3 * l)
        wl = w_in[l]
        w_re = jnp.concatenate([wl[:, :3072], wl[:, 3088:5136], wl[:, 3072:3088],
                                jnp.zeros((D_MODEL, LANES - G_GATE_RANK), f32)], axis=1).astype(bf16)
        wa2 = jnp.concatenate([w_a2[l], jnp.zeros((LANES - G_GATE_RANK, G_K_WIDTH), f32)], axis=0)
        ba = b_a[l].reshape(1, G_K_WIDTH)
        lam_vecs = jnp.stack([lam_q1[l], lam_k1[l], lam_q2[l], lam_k2[l]]).astype(f32)
        subg = sub_g[l].reshape(1, A_VDIM)
        glag = gla_g[l].reshape(1, G_DV)

        wvt = wl[:, C_V:C_GQ].T.astype(bf16)
        q, kf, kb, vf, vt, gq, gk, gv, gr, za, zb, glr = _inproj(x, w_re, wvt)

        oa_p = _attn_prompt(q, kb, vt, lam_vecs, sub_g[l].reshape(A_VDIM, 1), batch, seq, lam_init)
        tail_pad = ((0, 0), (0, 16 - A_HEADS), (0, 0))
        oa_s = _attn_decode(q[tp:].reshape(bd, 1, A_WIDTH),
                            jnp.pad(kf[tp:].reshape(bd, A_HEADS, 2 * A_HEAD_DIM), tail_pad),
                            jnp.pad(vf[tp:].reshape(bd, A_HEADS, A_VDIM), tail_pad),
                            lam_vecs, subg, cache_k4, cache_v4, page_table, l, lam_init)
        og_p, s_p = _gla_prompt(gq, gk, gv, glr, gr, wa2, ba, glag, batch, seq)
        og_s, s_s = _gla_sample(gq[tp:].reshape(bd, 1, -1), gk[tp:].reshape(bd, 1, -1),
                                gv[tp:].reshape(bd, 1, -1), glr[tp:].reshape(bd, 1, -1),
                                gr[tp:].reshape(bd, 1, -1), state_gla[l], wa2, ba, glag)
        oa = jnp.concatenate([oa_p, oa_s.reshape(bd, A_WIDTH)], axis=0)
        og = jnp.concatenate([og_p, og_s.reshape(bd, G_V_WIDTH)], axis=0)

        x1, xp = _postmix(oa, og, za, zb, x, w_pa[l].astype(bf16), w_pb[l].astype(bf16),
                          w_out[l].astype(bf16), ln1_g[l].reshape(1, -1), ln1_b[l].reshape(1, -1), alpha)
        eidx_t, gates_t, rank_t, counts = _router(x1, w_router[l].T, b_router[l].reshape(N_EXPERTS, 1))
        dest, blk_exp, n_used, n_rows = _layout(eidx_t, rank_t, counts.reshape(-1).astype(i32))
        xs = _dispatch(xp, dest, n_rows)
        ys = _experts(xs, blk_exp, n_used, w_gate, w_up, w_down, l)
        x = _combine(dest, gates_t.T, x1, xp, ys, ws_gate[l].astype(bf16), ws_up[l].astype(bf16),
                     ws_down[l].astype(bf16), ln2_g[l].reshape(1, -1), ln2_b[l].reshape(1, -1), alpha)

        kp_l.append(kf[:tp].reshape(batch, seq // PAGE_SIZE, PAGE_SIZE, A_HEADS, 2 * A_HEAD_DIM))
        vp_l.append(vf[:tp].reshape(batch, seq // PAGE_SIZE, PAGE_SIZE, A_HEADS, A_VDIM))
        ks_l.append(kf[tp:].reshape(bd, 1, A_HEADS, 2 * A_HEAD_DIM))
        vs_l.append(vf[tp:].reshape(bd, 1, A_HEADS, A_VDIM))
        sp_l.append(s_p.reshape(batch, G_HEADS, G_DK, G_DV))
        ss_l.append(s_s)

    y_prompt = x[:tp].reshape(batch, seq, D_MODEL)
    y_sample = x[tp:].reshape(bd, 1, D_MODEL)
    return (y_prompt, y_sample, jnp.stack(kp_l), jnp.stack(vp_l), jnp.stack(ks_l), jnp.stack(vs_l),
            jnp.stack(sp_l), jnp.stack(ss_l))
```

```python
import functools
import math

import jax
import jax.numpy as jnp
from jax import lax
from jax.experimental import pallas as pl
from jax.experimental.pallas import tpu as pltpu

f32 = jnp.float32
bf16 = jnp.bfloat16
u32 = jnp.uint32
i32 = jnp.int32

D_MODEL = 1024
A_HEADS = 4
A_HEAD_DIM = 64
A_VDIM = 128
A_WIDTH = A_HEADS * A_VDIM
G_HEADS = 4
G_DK = 64
G_DV = 128
G_K_WIDTH = G_HEADS * G_DK
G_V_WIDTH = G_HEADS * G_DV
G_GATE_RANK = 16
G_TAU = 16.0
N_EXPERTS = 256
TOP_K = 8
N_GROUPS = 8
TOPK_GROUPS = 4
D_EXPERT = 256
ROUTED_SCALE = 2.5
PAGE_SIZE = 128
LN_EPS = 1e-5
RMS_EPS = 1e-6

LANES = 128
VMEM_LIMIT = 56 * 1024 * 1024

NEG_BIG = -1e30
HALF = D_MODEL // 2
EXPERT_BLOCK = 256
WEIGHT_SLOTS = 3
GLA_CHUNK = 64
GLA_SUB = 16
ATTN_TQ = 1024
ATTN_TK = 1024
ONES_ROWS = 16
LOG2E = math.log2(math.e)

C_Q, C_K, C_V, C_GQ, C_GK, C_GV, C_GR, C_ZA, C_ZB, C_GLR, C_END = (
    0, 512, 1024, 1536, 1792, 2048, 2560, 3072, 4096, 5120, 5248)


def _params(sem, vmem=VMEM_LIMIT):
    return pltpu.CompilerParams(dimension_semantics=sem, vmem_limit_bytes=vmem)


def _row_tile(n, cands=(512, 384, 256, 128, 64, 32, 16, 8)):
    for c in cands:
        if n % c == 0:
            return c
    raise ValueError(f"no row tile for {n}")


def _sigmoid(x):
    return 1.0 / (1.0 + jnp.exp(-x))


def _pack_rows(x):
    lo = lax.bitcast_convert_type(x[:, :HALF].astype(bf16).astype(f32), u32) >> 16
    hi = lax.bitcast_convert_type(x[:, HALF:].astype(bf16).astype(f32), u32) & jnp.uint32(0xFFFF0000)
    return lo | hi


def _unpack_rows(w):
    lo = lax.bitcast_convert_type(w << 16, f32)
    hi = lax.bitcast_convert_type(w & jnp.uint32(0xFFFF0000), f32)
    return lo, hi


def _layer_norm(h, g, b):
    mu = jnp.mean(h, axis=-1, keepdims=True)
    d = h - mu
    var = jnp.mean(d * d, axis=-1, keepdims=True)
    return d * lax.rsqrt(var + LN_EPS) * g + b


def _rms_norm(o, g):
    return o * lax.rsqrt(jnp.mean(o * o, axis=-1, keepdims=True) + RMS_EPS) * g


def _lam_value(lam_ref, lam_init):
    l = lam_ref[...]
    s1 = jnp.sum(l[0:1] * l[1:2], axis=-1, keepdims=True)
    s2 = jnp.sum(l[2:3] * l[3:4], axis=-1, keepdims=True)
    return jnp.exp(s1) - jnp.exp(s2) + lam_init


def _inproj_body(x_ref, w_ref, wvt_ref, q_o, kf_o, kb_o, vf_o, vt_o, gq_o, gk_o, gv_o, gr_o, za_o, zb_o,
                 glr_o):
    xb = x_ref[...].astype(bf16)

    def mm(c0, c1):
        return jnp.dot(xb, w_ref[:, c0:c1], preferred_element_type=f32)

    q_o[...] = (mm(C_Q, C_K) * (A_HEAD_DIM ** -0.5 * LOG2E)).astype(bf16)
    k = mm(C_K, C_V)
    kf_o[...] = k
    kb_o[...] = k.astype(bf16)
    vf_o[...] = mm(C_V, C_GQ)
    vt_o[...] = lax.dot_general(wvt_ref[...], xb, (((1,), (1,)), ((), ())),
                                preferred_element_type=f32).astype(bf16)
    gq_o[...] = mm(C_GQ, C_GK) * (G_DK ** -0.5)
    gk_o[...] = mm(C_GK, C_GV)
    gv_o[...] = mm(C_GV, C_GR)
    gr_o[...] = mm(C_GR, C_ZA)
    za_o[...] = mm(C_ZA, C_ZB).astype(bf16)
    zb_o[...] = mm(C_ZB, C_GLR).astype(bf16)
    glr_o[...] = mm(C_GLR, C_END)


def _inproj(x, w, wvt):
    t = x.shape[0]
    tm = _row_tile(t, (384, 256, 128))
    outs = [(512, bf16), (512, f32), (512, bf16), (512, f32), None, (256, f32), (256, f32),
            (512, f32), (512, f32), (1024, bf16), (1024, bf16), (LANES, f32)]
    out_specs = [pl.BlockSpec((A_WIDTH, tm), lambda i: (0, i)) if o is None
                 else pl.BlockSpec((tm, o[0]), lambda i: (i, 0)) for o in outs]
    out_shape = [jax.ShapeDtypeStruct((A_WIDTH, t), bf16) if o is None
                 else jax.ShapeDtypeStruct((t, o[0]), o[1]) for o in outs]
    return pl.pallas_call(
        _inproj_body,
        grid=(t // tm,),
        in_specs=[pl.BlockSpec((tm, D_MODEL), lambda i: (i, 0)),
                  pl.BlockSpec((D_MODEL, C_END), lambda i: (0, 0)),
                  pl.BlockSpec((A_WIDTH, D_MODEL), lambda i: (0, 0))],
        out_specs=out_specs,
        out_shape=out_shape,
        compiler_params=_params(("parallel",)),
    )(x, w, wvt)


def _attn_body(qi_tab, kj_tab, diag_tab, last_tab, q_ref, k_ref, vt_ref, lam_ref, subg_ref, o_ref,
               m1, a1, m2, a2, *, tq, tk, lam_init):
    p = pl.program_id(2)
    qi = qi_tab[p]
    kj = kj_tab[p]

    @pl.when(kj == 0)
    def _():
        for m, a in ((m1, a1), (m2, a2)):
            m[...] = jnp.full(m.shape, NEG_BIG, f32)
            a[...] = jnp.zeros(a.shape, f32)

    q = q_ref[...]
    k = k_ref[...]
    vt = jnp.concatenate([vt_ref[...], jnp.ones((ONES_ROWS, tk), bf16)], axis=0)
    lane = lax.broadcasted_iota(i32, (1, LANES), 1)
    zero = jnp.zeros_like(q)
    q1 = jnp.where(lane < A_HEAD_DIM, q, zero)
    q2 = jnp.where(lane >= A_HEAD_DIM, q, zero)

    def step(masked):
        for qm, m, a in ((q1, m1, a1), (q2, m2, a2)):
            s = lax.dot_general(k, qm, (((1,), (1,)), ((), ())), preferred_element_type=f32)
            if masked:
                kpos = kj * tk + lax.broadcasted_iota(i32, (tk, tq), 0)
                qpos = qi * tq + lax.broadcasted_iota(i32, (tk, tq), 1)
                s = jnp.where(kpos <= qpos, s, NEG_BIG)
            m_prev = m[...]
            m_new = jnp.maximum(m_prev, jnp.max(s, axis=0, keepdims=True))
            alpha = jnp.exp2(m_prev - m_new)
            pr = jnp.exp2(s - m_new).astype(bf16)
            a[...] = alpha * a[...] + jnp.dot(vt, pr, preferred_element_type=f32)
            m[...] = m_new

    @pl.when(diag_tab[p] == 1)
    def _():
        step(True)

    @pl.when(diag_tab[p] == 0)
    def _():
        step(False)

    @pl.when(last_tab[p] == 1)
    def _():
        lam = _lam_value(lam_ref, lam_init)
        ot = (a1[:A_VDIM, :] / a1[A_VDIM:A_VDIM + 1, :]
              - lam * (a2[:A_VDIM, :] / a2[A_VDIM:A_VDIM + 1, :]))
        ms = jnp.mean(ot * ot, axis=0, keepdims=True)
        on = ot * lax.rsqrt(ms + RMS_EPS) * subg_ref[...] * (1.0 - lam_init)
        o_ref[...] = on.T.astype(o_ref.dtype)


def _attn_prompt(q, k, vt, lam_vecs, sub_g_col, batch, seq, lam_init):
    tq = min(ATTN_TQ, seq)
    tk = min(ATTN_TK, seq)
    nq, nk = seq // tq, seq // tk
    qi_l, kj_l, dg_l, ls_l = [], [], [], []
    for qi in range(nq):
        last = ((qi + 1) * tq - 1) // tk
        for kj in range(last + 1):
            qi_l.append(qi)
            kj_l.append(kj)
            dg_l.append(1 if (kj + 1) * tk - 1 > qi * tq else 0)
            ls_l.append(1 if kj == last else 0)
    tabs = [jnp.asarray(t, i32) for t in (qi_l, kj_l, dg_l, ls_l)]
    n_pairs = len(qi_l)
    body = functools.partial(_attn_body, tq=tq, tk=tk, lam_init=lam_init)
    grid_spec = pltpu.PrefetchScalarGridSpec(
        num_scalar_prefetch=4,
        grid=(batch, A_HEADS, n_pairs),
        in_specs=[
            pl.BlockSpec((tq, LANES), lambda b, h, p, qt, kt, dt, lt: (b * nq + qt[p], h)),
            pl.BlockSpec((tk, LANES), lambda b, h, p, qt, kt, dt, lt: (b * nk + kt[p], h)),
            pl.BlockSpec((A_VDIM, tk), lambda b, h, p, qt, kt, dt, lt: (h, b * nk + kt[p])),
            pl.BlockSpec((4, A_HEAD_DIM), lambda b, h, p, *_: (0, 0)),
            pl.BlockSpec((A_VDIM, 1), lambda b, h, p, *_: (0, 0)),
        ],
        out_specs=pl.BlockSpec((tq, LANES), lambda b, h, p, qt, kt, dt, lt: (b * nq + qt[p], h)),
        scratch_shapes=[pltpu.VMEM((1, tq), f32), pltpu.VMEM((A_VDIM + ONES_ROWS, tq), f32),
                        pltpu.VMEM((1, tq), f32), pltpu.VMEM((A_VDIM + ONES_ROWS, tq), f32)],
    )
    return pl.pallas_call(
        body,
        grid_spec=grid_spec,
        out_shape=jax.ShapeDtypeStruct((batch * seq, A_WIDTH), bf16),
        compiler_params=_params(("parallel", "parallel", "arbitrary")),
    )(*tabs, q, k, vt, lam_vecs, sub_g_col)


def _decode_body(pt_ref, q_ref, kn_ref, vn_ref, lam_ref, subg_ref, *refs, n_pages, lam_init):
    k_refs = refs[:n_pages]
    v_refs = refs[n_pages:2 * n_pages]
    o_ref = refs[2 * n_pages]
    kbuf, vbuf = refs[2 * n_pages + 1:]
    rows_pg = PAGE_SIZE * A_HEADS
    past = n_pages * rows_pg
    tail = 16
    n_col = past + tail

    for p in range(n_pages):
        kbuf[p * rows_pg:(p + 1) * rows_pg, :] = k_refs[p][...].astype(bf16)
        vbuf[p * rows_pg:(p + 1) * rows_pg, :] = v_refs[p][...].astype(bf16)
    kbuf[past:, :] = kn_ref[0].astype(bf16)
    vbuf[past:, :] = vn_ref[0].astype(bf16)

    qrow = q_ref[0].astype(f32)
    row = lax.broadcasted_iota(i32, (16, LANES), 0)
    lane = lax.broadcasted_iota(i32, (16, LANES), 1)
    qmat = jnp.zeros((16, LANES), f32)
    for h in range(A_HEADS):
        qh = jnp.broadcast_to(qrow[:, h * LANES:(h + 1) * LANES], (16, LANES))
        sel = ((row >> 1) == h) & ((lane >= A_HEAD_DIM) == ((row & 1) == 1))
        qmat = jnp.where(sel, qh, qmat)
    s = lax.dot_general(qmat.astype(bf16), kbuf[...], (((1,), (1,)), ((), ())), preferred_element_type=f32)
    srow = lax.broadcasted_iota(i32, (16, n_col), 0)
    scol = lax.broadcasted_iota(i32, (16, n_col), 1)
    valid = ((scol & (A_HEADS - 1)) == (srow >> 1)) & (scol < past + A_HEADS) & (srow < 2 * A_HEADS)
    s = jnp.where(valid, s, NEG_BIG)
    m = jnp.max(s, axis=-1, keepdims=True)
    pr = jnp.where(valid, jnp.exp2(s - m), 0.0)
    den = jnp.maximum(jnp.sum(pr, axis=-1, keepdims=True), 1e-30)
    pn = pr / den
    o8 = jnp.dot(pn.astype(bf16), vbuf[...], preferred_element_type=f32)
    lam = _lam_value(lam_ref, lam_init)
    for h in range(A_HEADS):
        o = o8[2 * h:2 * h + 1, :] - lam * o8[2 * h + 1:2 * h + 2, :]
        o_ref[0, :, h * LANES:(h + 1) * LANES] = (
            _rms_norm(o, subg_ref[...]) * (1.0 - lam_init)).astype(o_ref.dtype)


def _attn_decode(q_s, k_new, v_new, lam_vecs, sub_g, cache_k4, cache_v4, page_table, layer, lam_init):
    bd, n_pages = page_table.shape
    rows_pg = PAGE_SIZE * A_HEADS
    body = functools.partial(_decode_body, n_pages=n_pages, lam_init=lam_init)

    def page_spec(p):
        return pl.BlockSpec((None, None, rows_pg, LANES),
                            lambda b, pt, p=p: (layer, pt[b * n_pages + p], 0, 0))

    grid_spec = pltpu.PrefetchScalarGridSpec(
        num_scalar_prefetch=1,
        grid=(bd,),
        in_specs=[pl.BlockSpec((1, 1, A_WIDTH), lambda b, pt: (b, 0, 0)),
                  pl.BlockSpec((1, 16, LANES), lambda b, pt: (b, 0, 0)),
                  pl.BlockSpec((1, 16, LANES), lambda b, pt: (b, 0, 0)),
                  pl.BlockSpec((4, A_HEAD_DIM), lambda b, pt: (0, 0)),
                  pl.BlockSpec((1, A_VDIM), lambda b, pt: (0, 0))]
        + [page_spec(p) for p in range(n_pages)] * 2,
        out_specs=pl.BlockSpec((1, 1, A_WIDTH), lambda b, pt: (b, 0, 0)),
        scratch_shapes=[pltpu.VMEM((n_pages * rows_pg + 16, LANES), bf16),
                        pltpu.VMEM((n_pages * rows_pg + 16, LANES), bf16)],
    )
    return pl.pallas_call(
        body,
        grid_spec=grid_spec,
        out_shape=jax.ShapeDtypeStruct((bd, 1, A_WIDTH), bf16),
        compiler_params=_params(("arbitrary",)),
    )(page_table.reshape(-1), q_s, k_new, v_new, lam_vecs, sub_g,
      *([cache_k4] * n_pages), *([cache_v4] * n_pages))


def _log_decay(glr, wa2, ba):
    z = jnp.dot(glr, wa2, preferred_element_type=f32, precision=lax.Precision.HIGHEST) + ba
    return (jnp.minimum(z, 0.0) - jnp.log(1.0 + jnp.exp(-jnp.abs(z)))) * (1.0 / G_TAU)


def _gla_finish(o, gr, g):
    return _rms_norm(o, g) * (gr * _sigmoid(gr))


def _gla_prompt_body(q_ref, k_ref, v_ref, glr_ref, gr_ref, wa2_ref, ba_ref, g_ref, o_ref, s_ref, state, tmp,
                     *, c):
    ci = pl.program_id(1)
    nsub = c // GLA_SUB

    @pl.when(ci == 0)
    def _():
        state[...] = jnp.zeros(state.shape, f32)

    la = _log_decay(glr_ref[...], wa2_ref[...], ba_ref[...])
    ri = lax.broadcasted_iota(i32, (c, c), 0)
    cj = lax.broadcasted_iota(i32, (c, c), 1)
    b = jnp.dot((ri >= cj).astype(f32), la, preferred_element_type=f32, precision=lax.Precision.HIGHEST)
    b_last = b[c - 1:c, :]
    q = q_ref[...]
    k = k_ref[...]
    v = v_ref[...]
    vb = v.astype(bf16)
    lane = lax.broadcasted_iota(i32, (1, G_K_WIDTH), 1)
    heads = [(lane >> 6) == h for h in range(G_HEADS)]

    def stack_heads(x):
        return jnp.concatenate([jnp.where(m, x, 0.0) for m in heads], axis=0).astype(bf16)

    sub_i = lax.broadcasted_iota(i32, (GLA_SUB, G_K_WIDTH), 0)
    for blk in range(nsub):
        r0 = blk * GLA_SUB
        q_b = q[r0:r0 + GLA_SUB]
        b_b = b[r0:r0 + GLA_SUB]
        for j in range(GLA_SUB):
            w = jnp.where(sub_i >= j, jnp.exp(jnp.minimum(b_b - b_b[j:j + 1], 0.0)), 0.0)
            t0 = (r0 + j) * GLA_SUB
            tmp[t0:t0 + GLA_SUB, :] = (q_b * w * k[r0 + j:r0 + j + 1]).astype(bf16)
    si = lax.broadcasted_iota(i32, (G_K_WIDTH, G_V_WIDTH), 0)
    sj = lax.broadcasted_iota(i32, (G_K_WIDTH, G_V_WIDTH), 1)
    seg = ((si >> 6) == (sj >> 7)).astype(bf16)
    pair = jnp.dot(tmp[...], seg, preferred_element_type=f32)
    o_rows = []
    for blk in range(nsub):
        r0 = blk * GLA_SUB
        acc = jnp.zeros((GLA_SUB, G_V_WIDTH), f32)
        for j in range(GLA_SUB):
            t0 = (r0 + j) * GLA_SUB
            acc = acc + pair[t0:t0 + GLA_SUB, :] * v[r0 + j:r0 + j + 1]
        o_rows.append(acc)
    o_diag = jnp.concatenate(o_rows, axis=0)

    att_rows = [[jnp.zeros((GLA_SUB, c), f32)] for _ in range(G_HEADS)]
    col = lax.broadcasted_iota(i32, (G_HEADS * GLA_SUB, c), 1)
    for blk in range(1, nsub):
        r0 = blk * GLA_SUB
        ref = b[r0 - 1:r0]
        q_b = q[r0:r0 + GLA_SUB] * jnp.exp(b[r0:r0 + GLA_SUB] - ref)
        k_b = (k * jnp.exp(jnp.minimum(ref - b, 0.0))).astype(bf16)
        a = lax.dot_general(stack_heads(q_b), k_b, (((1,), (1,)), ((), ())), preferred_element_type=f32)
        a = jnp.where(col < r0, a, 0.0)
        for h in range(G_HEADS):
            att_rows[h].append(a[h * GLA_SUB:(h + 1) * GLA_SUB])

    st = state[...]
    o_inter = jnp.dot(stack_heads(q * jnp.exp(b)), st.astype(bf16), preferred_element_type=f32)
    kdt = (k * jnp.exp(b_last - b)).T.astype(bf16)
    upd = jnp.dot(kdt, vb, preferred_element_type=f32)
    dec_col = jnp.exp(jnp.broadcast_to(b_last, (8, G_K_WIDTH)).T[:, 0:1])
    for h in range(G_HEADS):
        cols = slice(h * G_DV, (h + 1) * G_DV)
        att = jnp.concatenate(att_rows[h], axis=0).astype(bf16)
        o = (o_diag[:, cols] + jnp.dot(att, vb[:, cols], preferred_element_type=f32)
             + o_inter[h * c:(h + 1) * c])
        o_ref[:, cols] = _gla_finish(o, gr_ref[:, cols], g_ref[...]).astype(o_ref.dtype)
        r0 = h * G_DK
        state[r0:r0 + G_DK, :] = dec_col[r0:r0 + G_DK] * st[r0:r0 + G_DK, :] + upd[r0:r0 + G_DK, cols]

    @pl.when(ci == pl.num_programs(1) - 1)
    def _():
        s_ref[0] = state[...]


def _gla_prompt(gq, gk, gv, glr, gr, wa2, ba, gla_g, batch, seq):
    c = math.gcd(seq, GLA_CHUNK)
    n = seq // c
    body = functools.partial(_gla_prompt_body, c=c)

    def tok(width):
        return pl.BlockSpec((c, width), lambda b, i: (b * n + i, 0))

    def whole(shape):
        return pl.BlockSpec(shape, lambda b, i: (0,) * len(shape))

    return pl.pallas_call(
        body,
        grid=(batch, n),
        in_specs=[tok(G_K_WIDTH), tok(G_K_WIDTH), tok(G_V_WIDTH), tok(LANES), tok(G_V_WIDTH),
                  whole((LANES, G_K_WIDTH)), whole((1, G_K_WIDTH)), whole((1, G_DV))],
        out_specs=[tok(G_V_WIDTH),
                   pl.BlockSpec((1, G_K_WIDTH, G_DV), lambda b, i: (b, 0, 0))],
        out_shape=[jax.ShapeDtypeStruct((batch * seq, G_V_WIDTH), bf16),
                   jax.ShapeDtypeStruct((batch, G_K_WIDTH, G_DV), f32)],
        scratch_shapes=[pltpu.VMEM((G_K_WIDTH, G_DV), f32),
                        pltpu.VMEM((c * GLA_SUB, G_K_WIDTH), bf16)],
        compiler_params=_params(("parallel", "arbitrary")),
    )(gq, gk, gv, glr, gr, wa2, ba, gla_g)


def _gla_sample_body(q_ref, k_ref, v_ref, glr_ref, gr_ref, s_ref, wa2_ref, ba_ref, g_ref, o_ref, sn_ref):
    la = _log_decay(glr_ref[0], wa2_ref[...], ba_ref[...])
    rows = jnp.concatenate([jnp.broadcast_to(jnp.exp(la), (G_DK, G_K_WIDTH)),
                            jnp.broadcast_to(k_ref[0], (G_DK, G_K_WIDTH)),
                            jnp.broadcast_to(q_ref[0], (G_DK, G_K_WIDTH))], axis=0)
    ri = lax.broadcasted_iota(i32, (3 * G_DK, G_K_WIDTH), 0)
    li = lax.broadcasted_iota(i32, (3 * G_DK, G_K_WIDTH), 1)
    picked = jnp.where((ri & (G_DK - 1)) == (li & (G_DK - 1)), rows, 0.0)
    si = lax.broadcasted_iota(i32, (G_K_WIDTH, G_V_WIDTH), 0)
    sj = lax.broadcasted_iota(i32, (G_K_WIDTH, G_V_WIDTH), 1)
    seg = ((si >> 6) == (sj >> 7)).astype(f32)
    cols = jnp.dot(picked, seg, preferred_element_type=f32, precision=lax.Precision.HIGHEST)
    v = v_ref[0]
    for h in range(G_HEADS):
        sl = slice(h * G_DV, (h + 1) * G_DV)
        a_c = cols[0:G_DK, sl]
        k_c = cols[G_DK:2 * G_DK, sl]
        q_c = cols[2 * G_DK:3 * G_DK, sl]
        s_new = a_c * s_ref[0, h] + k_c * v[:, sl]
        sn_ref[0, h] = s_new
        o = jnp.sum(q_c * s_new, axis=0, keepdims=True)
        o_ref[0, :, sl] = _gla_finish(o, gr_ref[0][:, sl], g_ref[...]).astype(o_ref.dtype)


def _gla_sample(gq, gk, gv, glr, gr, s0, wa2, ba, gla_g):
    bd = gq.shape[0]

    def vec(width):
        return pl.BlockSpec((1, 1, width), lambda b: (b, 0, 0))

    def whole(shape):
        return pl.BlockSpec(shape, lambda b: (0,) * len(shape))

    st = pl.BlockSpec((1, G_HEADS, G_DK, G_DV), lambda b: (b, 0, 0, 0))
    return pl.pallas_call(
        _gla_sample_body,
        grid=(bd,),
        in_specs=[vec(G_K_WIDTH), vec(G_K_WIDTH), vec(G_V_WIDTH), vec(LANES), vec(G_V_WIDTH), st,
                  whole((LANES, G_K_WIDTH)), whole((1, G_K_WIDTH)), whole((1, G_DV))],
        out_specs=[vec(G_V_WIDTH), st],
        out_shape=[jax.ShapeDtypeStruct((bd, 1, G_V_WIDTH), bf16),
                   jax.ShapeDtypeStruct((bd, G_HEADS, G_DK, G_DV), f32)],
        compiler_params=_params(("parallel",)),
    )(gq, gk, gv, glr, gr, s0, wa2, ba, gla_g)


def _postmix_body(oa_ref, og_ref, za_ref, zb_ref, x_ref, wpa_ref, wpb_ref, wout_ref, g_ref, b_ref,
                  x1_o, xp_o, *, alpha):
    ya = jnp.dot(oa_ref[...], wpa_ref[...], preferred_element_type=f32)
    yb = jnp.dot(og_ref[...], wpb_ref[...], preferred_element_type=f32)
    merged = _sigmoid(za_ref[...].astype(f32)) * ya + _sigmoid(zb_ref[...].astype(f32)) * yb
    mix = jnp.dot(merged.astype(bf16), wout_ref[...], preferred_element_type=f32)
    x1 = _layer_norm(alpha * x_ref[...] + mix, g_ref[...], b_ref[...])
    x1_o[...] = x1
    xp_o[...] = _pack_rows(x1)


def _postmix(oa, og, za, zb, x, wpa, wpb, wout, g, b, alpha):
    t = x.shape[0]
    tm = _row_tile(t, (384, 256, 128, 64, 32, 16, 8))

    def tok(width):
        return pl.BlockSpec((tm, width), lambda i: (i, 0))

    def whole(shape):
        return pl.BlockSpec(shape, lambda i: (0,) * len(shape))

    return pl.pallas_call(
        functools.partial(_postmix_body, alpha=alpha),
        grid=(t // tm,),
        in_specs=[tok(A_WIDTH), tok(G_V_WIDTH), tok(D_MODEL), tok(D_MODEL), tok(D_MODEL),
                  whole((A_WIDTH, D_MODEL)), whole((G_V_WIDTH, D_MODEL)), whole((D_MODEL, D_MODEL)),
                  whole((1, D_MODEL)), whole((1, D_MODEL))],
        out_specs=[tok(D_MODEL), tok(HALF)],
        out_shape=[jax.ShapeDtypeStruct((t, D_MODEL), f32),
                   jax.ShapeDtypeStruct((t, HALF), u32)],
        compiler_params=_params(("parallel",)),
    )(oa, og, za, zb, x, wpa, wpb, wout, g, b)


def _router_body(x1_ref, wrt_ref, bcol_ref, tri_ref, eidx_o, gate_o, rank_o, cnt_o, cnt, *, tm):
    @pl.when(pl.program_id(0) == 0)
    def _():
        cnt[...] = jnp.zeros(cnt.shape, f32)

    logits = lax.dot_general(wrt_ref[...], x1_ref[...], (((1,), (1,)), ((), ())),
                             preferred_element_type=f32, precision=lax.Precision.HIGHEST)
    scores = _sigmoid(logits)
    biased = scores + bcol_ref[...]
    gsz = N_EXPERTS // N_GROUPS
    neg_inf = -jnp.inf

    gi = lax.broadcasted_iota(i32, (gsz, tm), 0)
    segs, gscore = [], []
    for g in range(N_GROUPS):
        seg = biased[g * gsz:(g + 1) * gsz, :]
        m1 = jnp.max(seg, axis=0, keepdims=True)
        i1 = jnp.min(jnp.where(seg == m1, gi, gsz), axis=0, keepdims=True)
        m2 = jnp.max(jnp.where(gi == i1, neg_inf, seg), axis=0, keepdims=True)
        segs.append(seg)
        gscore.append(m1 + m2)
    parts = []
    for g in range(N_GROUPS):
        beat = jnp.zeros((1, tm), i32)
        for o in range(N_GROUPS):
            if o != g:
                wins = (gscore[o] > gscore[g]) | ((gscore[o] == gscore[g]) & (o < g))
                beat = beat + wins.astype(i32)
        parts.append(jnp.where(beat < TOPK_GROUPS, segs[g], neg_inf))
    masked = jnp.concatenate(parts, axis=0)

    ei = lax.broadcasted_iota(i32, (N_EXPERTS, tm), 0)
    sel_rows, idx_rows = [], []
    chosen = jnp.zeros((N_EXPERTS, tm), f32)
    for _ in range(TOP_K):
        m = jnp.max(masked, axis=0, keepdims=True)
        idx = jnp.min(jnp.where(masked == m, ei, N_EXPERTS), axis=0, keepdims=True)
        hit = ei == idx
        sel_rows.append(jnp.sum(jnp.where(hit, scores, 0.0), axis=0, keepdims=True))
        idx_rows.append(idx)
        chosen = jnp.where(hit, 1.0, chosen)
        masked = jnp.where(hit, neg_inf, masked)
    s_sel = jnp.concatenate(sel_rows, axis=0)
    gate_o[...] = s_sel / jnp.sum(s_sel, axis=0, keepdims=True) * ROUTED_SCALE
    eidx_o[...] = jnp.concatenate(idx_rows, axis=0)
    before = jnp.dot(chosen.astype(bf16), tri_ref[...], preferred_element_type=f32) + cnt[...]
    rank_rows = [jnp.sum(jnp.where(ei == idx, before, 0.0), axis=0, keepdims=True) for idx in idx_rows]
    rank_o[...] = jnp.concatenate(rank_rows, axis=0).astype(i32)
    cnt[...] = cnt[...] + jnp.sum(chosen, axis=1, keepdims=True)
    cnt_o[...] = cnt[...]


def _router(x1, wrt, b_col):
    t = x1.shape[0]
    tm = _row_tile(t, (384, 256, 128))
    tri = (jnp.arange(tm)[:, None] < jnp.arange(tm)[None, :]).astype(bf16)

    def tokcol(dt):
        return pl.BlockSpec((TOP_K, tm), lambda i: (0, i)), jax.ShapeDtypeStruct((TOP_K, t), dt)

    def whole(shape):
        return pl.BlockSpec(shape, lambda i: (0,) * len(shape))

    specs, shapes = zip(tokcol(i32), tokcol(f32), tokcol(i32),
                        (whole((N_EXPERTS, 1)), jax.ShapeDtypeStruct((N_EXPERTS, 1), f32)))
    return pl.pallas_call(
        functools.partial(_router_body, tm=tm),
        grid=(t // tm,),
        in_specs=[pl.BlockSpec((tm, D_MODEL), lambda i: (i, 0)), whole((N_EXPERTS, D_MODEL)),
                  whole((N_EXPERTS, 1)), whole((tm, tm))],
        out_specs=list(specs),
        out_shape=list(shapes),
        scratch_shapes=[pltpu.VMEM((N_EXPERTS, 1), f32)],
        compiler_params=_params(("arbitrary",)),
    )(x1, wrt, b_col, tri)


def _row_copy(src, src_row, dst, dst_row, sem):
    return pltpu.make_async_copy(src.at[pl.ds(src_row, 1)], dst.at[pl.ds(dst_row, 1)], sem)


def _dispatch_body(dest_ref, x_ref, xs_in, xs_out, sem, *, tm):
    del xs_in

    def issue(t, carry):
        for kk in range(TOP_K):
            _row_copy(x_ref, t, xs_out, dest_ref[t * TOP_K + kk], sem).start(priority=kk % 2)
        return carry

    lax.fori_loop(0, tm, issue, 0)

    def drain(t, carry):
        for kk in range(TOP_K):
            _row_copy(x_ref, 0, xs_out, 0, sem).wait()
        return carry

    lax.fori_loop(0, tm, drain, 0)


def _dispatch(xp, dest, n_rows):
    t = xp.shape[0]
    tm = _row_tile(t, (384, 256, 128, 64, 32, 16, 8))
    return pl.pallas_call(
        functools.partial(_dispatch_body, tm=tm),
        grid=(t // tm,),
        in_specs=[pl.BlockSpec((tm * TOP_K,), lambda i: (i,), memory_space=pltpu.SMEM),
                  pl.BlockSpec((tm, HALF), lambda i: (i, 0)),
                  pl.BlockSpec(memory_space=pl.ANY)],
        out_specs=pl.BlockSpec(memory_space=pl.ANY),
        out_shape=jax.ShapeDtypeStruct((n_rows, HALF), u32),
        scratch_shapes=[pltpu.SemaphoreType.DMA(())],
        input_output_aliases={2: 0},
        compiler_params=_params(("arbitrary",)),
    )(dest, xp, jnp.zeros((n_rows, HALF), u32))


def _expert_weight_copies(seq_ref, w_hbm, w_buf, sems, seq_idx, *, layer):
    e = seq_ref[seq_idx]
    slot = lax.rem(seq_idx, WEIGHT_SLOTS)
    return [pltpu.make_async_copy(w_hbm[n].at[layer, e], w_buf[n].at[slot], sems.at[slot, n]) for n in range(3)]


def _expert_body(bs_ref, seq_ref, ns_ref, nu_ref, xs_ref, wg_hbm, wu_hbm, wd_hbm, ys_ref,
                 wgf, wuf, wdf, sems, wgb, wub, wdb, *, layer):
    i = pl.program_id(0)
    j = bs_ref[i]
    copies = functools.partial(_expert_weight_copies, seq_ref, (wg_hbm, wu_hbm, wd_hbm), (wgf, wuf, wdf),
                               sems, layer=layer)

    @pl.when(i == 0)
    def _():
        for c in copies(0):
            c.start()

        @pl.when(ns_ref[0] > 1)
        def _():
            for c in copies(1):
                c.start()

    @pl.when((i == 0) | (j != bs_ref[jnp.maximum(i - 1, 0)]))
    def _():
        for c in copies(j):
            c.wait()
        slot = lax.rem(j, WEIGHT_SLOTS)
        wgb[...] = wgf[slot].astype(bf16)
        wub[...] = wuf[slot].astype(bf16)
        wdb[...] = wdf[slot].astype(bf16)

        @pl.when(j + 2 < ns_ref[0])
        def _():
            for c in copies(j + 2):
                c.start()

    @pl.when(i < nu_ref[0])
    def _():
        lo, hi = _unpack_rows(xs_ref[...])
        lo = lo.astype(bf16)
        hi = hi.astype(bf16)
        g = (jnp.dot(lo, wgb[:HALF, :], preferred_element_type=f32)
             + jnp.dot(hi, wgb[HALF:, :], preferred_element_type=f32))
        u = (jnp.dot(lo, wub[:HALF, :], preferred_element_type=f32)
             + jnp.dot(hi, wub[HALF:, :], preferred_element_type=f32))
        hdn = (g * _sigmoid(g) * u).astype(bf16)
        ys_ref[...] = _pack_rows(jnp.dot(hdn, wdb[...], preferred_element_type=f32))

    @pl.when(i >= nu_ref[0])
    def _():
        ys_ref[...] = jnp.zeros(ys_ref.shape, u32)


def _experts(xs, blk_seq, seq_exp, n_seq, n_used, w_gate, w_up, w_down, layer):
    n_rows = xs.shape[0]
    n_blocks = n_rows // EXPERT_BLOCK
    hbm = pl.BlockSpec(memory_space=pl.ANY)
    grid_spec = pltpu.PrefetchScalarGridSpec(
        num_scalar_prefetch=4,
        grid=(n_blocks,),
        in_specs=[pl.BlockSpec((EXPERT_BLOCK, HALF), lambda i, bs, sq, ns, nu: (jnp.minimum(i, nu[0] - 1), 0)),
                  hbm, hbm, hbm],
        out_specs=pl.BlockSpec((EXPERT_BLOCK, HALF), lambda i, bs, sq, ns, nu: (i, 0)),
        scratch_shapes=[pltpu.VMEM((WEIGHT_SLOTS, D_MODEL, D_EXPERT), f32),
                        pltpu.VMEM((WEIGHT_SLOTS, D_MODEL, D_EXPERT), f32),
                        pltpu.VMEM((WEIGHT_SLOTS, D_EXPERT, D_MODEL), f32),
                        pltpu.SemaphoreType.DMA((WEIGHT_SLOTS, 3)),
                        pltpu.VMEM((D_MODEL, D_EXPERT), bf16), pltpu.VMEM((D_MODEL, D_EXPERT), bf16),
                        pltpu.VMEM((D_EXPERT, D_MODEL), bf16)],
    )
    return pl.pallas_call(
        functools.partial(_expert_body, layer=layer),
        grid_spec=grid_spec,
        out_shape=jax.ShapeDtypeStruct((n_rows, HALF), u32),
        compiler_params=_params(("arbitrary",)),
    )(blk_seq, seq_exp, n_seq, n_used, xs, w_gate, w_up, w_down)


def _combine_body(dest_ref, gate_ref, x1_ref, xp_ref, ys_hbm, wsg_ref, wsu_ref, wsd_ref, g_ref, b_ref,
                  o_ref, buf, sem, *, tm, alpha):
    def issue(t, carry):
        for kk in range(TOP_K):
            _row_copy(ys_hbm, dest_ref[t * TOP_K + kk], buf.at[kk], t, sem).start(priority=kk % 2)
        return carry

    lax.fori_loop(0, tm, issue, 0)

    lo, hi = _unpack_rows(xp_ref[...])
    lo = lo.astype(bf16)
    hi = hi.astype(bf16)
    sg = (jnp.dot(lo, wsg_ref[:HALF, :], preferred_element_type=f32)
          + jnp.dot(hi, wsg_ref[HALF:, :], preferred_element_type=f32))
    su = (jnp.dot(lo, wsu_ref[:HALF, :], preferred_element_type=f32)
          + jnp.dot(hi, wsu_ref[HALF:, :], preferred_element_type=f32))
    shared = jnp.dot((sg * _sigmoid(sg) * su).astype(bf16), wsd_ref[...], preferred_element_type=f32)

    def drain(t, carry):
        for kk in range(TOP_K):
            _row_copy(ys_hbm, 0, buf.at[kk], 0, sem).wait()
        return carry

    lax.fori_loop(0, tm, drain, 0)

    gates = gate_ref[...]
    acc_lo = jnp.zeros((tm, HALF), f32)
    acc_hi = jnp.zeros((tm, HALF), f32)
    for kk in range(TOP_K):
        ylo, yhi = _unpack_rows(buf[kk])
        gk = gates[:, kk:kk + 1]
        acc_lo = acc_lo + gk * ylo
        acc_hi = acc_hi + gk * yhi
    moe = jnp.concatenate([acc_lo, acc_hi], axis=-1) + shared
    o_ref[...] = _layer_norm(alpha * x1_ref[...] + moe, g_ref[...], b_ref[...])


def _combine(dest, gates, x1, xp, ys, wsg, wsu, wsd, g, b, alpha):
    t = x1.shape[0]
    tm = _row_tile(t, (384, 256, 128, 64, 32, 16, 8))

    def tok(width):
        return pl.BlockSpec((tm, width), lambda i: (i, 0))

    def whole(shape):
        return pl.BlockSpec(shape, lambda i: (0,) * len(shape))

    return pl.pallas_call(
        functools.partial(_combine_body, tm=tm, alpha=alpha),
        grid=(t // tm,),
        in_specs=[pl.BlockSpec((tm * TOP_K,), lambda i: (i,), memory_space=pltpu.SMEM),
                  tok(TOP_K), tok(D_MODEL), tok(HALF),
                  pl.BlockSpec(memory_space=pl.ANY),
                  whole((D_MODEL, D_EXPERT)), whole((D_MODEL, D_EXPERT)), whole((D_EXPERT, D_MODEL)),
                  whole((1, D_MODEL)), whole((1, D_MODEL))],
        out_specs=tok(D_MODEL),
        out_shape=jax.ShapeDtypeStruct((t, D_MODEL), f32),
        scratch_shapes=[pltpu.VMEM((TOP_K, tm, HALF), u32), pltpu.SemaphoreType.DMA(())],
        compiler_params=_params(("arbitrary",)),
    )(dest, gates, x1, xp, ys, wsg, wsu, wsd, g, b)


def _dest_body(eidx_ref, rank_ref, ps_ref, dest_o, *, tm):
    ei = lax.broadcasted_iota(i32, (N_EXPERTS, tm), 0)
    ps = ps_ref[...]
    rows = [jnp.sum(jnp.where(ei == eidx_ref[kk:kk + 1, :], ps, 0.0), axis=0, keepdims=True)
            for kk in range(TOP_K)]
    dest_o[...] = jnp.concatenate(rows, axis=0).astype(i32) + rank_ref[...]


def _dest_rows(eidx_t, rank_t, pad_start):
    t = eidx_t.shape[1]
    tm = _row_tile(t, (384, 256, 128))
    blk = pl.BlockSpec((TOP_K, tm), lambda i: (0, i))
    return pl.pallas_call(
        functools.partial(_dest_body, tm=tm),
        grid=(t // tm,),
        in_specs=[blk, blk, pl.BlockSpec((N_EXPERTS, 1), lambda i: (0, 0))],
        out_specs=blk,
        out_shape=jax.ShapeDtypeStruct((TOP_K, t), i32),
        compiler_params=_params(("parallel",)),
    )(eidx_t, rank_t, pad_start.astype(f32).reshape(N_EXPERTS, 1))


def _layout(eidx_t, rank_t, counts):
    t = eidx_t.shape[1]
    blk = EXPERT_BLOCK
    padded = (counts + blk - 1) // blk * blk
    pad_end = jnp.cumsum(padded)
    pad_start = pad_end - padded
    dest = _dest_rows(eidx_t, rank_t, pad_start)
    n_rows = -(-(t * TOP_K + N_EXPERTS * (blk - 1)) // blk) * blk
    n_blocks = n_rows // blk
    n_used = (pad_end[-1] // blk).astype(i32)
    first_row = jnp.minimum(jnp.arange(n_blocks), n_used - 1) * blk
    blk_exp = jnp.sum(pad_end[None, :] <= first_row[:, None], axis=1).astype(i32)
    blk_exp = jnp.minimum(blk_exp, N_EXPERTS - 1)
    used = counts > 0
    seq_of = jnp.cumsum(used.astype(i32)) - 1
    ids = jnp.arange(N_EXPERTS, dtype=i32)
    seq_exp = jnp.sum(jnp.where(used[None, :] & (seq_of[None, :] == ids[:, None]), ids[None, :], 0), axis=1)
    blk_seq = jnp.sum(jnp.where(blk_exp[:, None] == ids[None, :], seq_of[None, :], 0), axis=1)
    n_seq = jnp.sum(used.astype(i32))
    return dest, blk_seq.astype(i32), seq_exp.astype(i32), n_seq.reshape(1), n_used.reshape(1), n_rows


def kernel(x_prompt, x_sample, cache_k, cache_v, state_gla, page_table, w_in, w_a2, b_a, lam_q1, lam_k1,
           lam_q2, lam_k2, sub_g, gla_g, w_pa, w_pb, w_out, ln1_g, ln1_b, w_router, b_router, w_gate, w_up,
           w_down, ws_gate, ws_up, ws_down, ln2_g, ln2_b):
    depth = w_in.shape[0]
    batch, seq, _ = x_prompt.shape
    bd = x_sample.shape[0]
    tp = batch * seq
    alpha = (2 * depth) ** 0.25
    n_pool = cache_k.shape[1]
    cache_k4 = cache_k.reshape(depth, n_pool, PAGE_SIZE * A_HEADS, 2 * A_HEAD_DIM)
    cache_v4 = cache_v.reshape(depth, n_pool, PAGE_SIZE * A_HEADS, A_VDIM)

    x = jnp.concatenate([x_prompt.reshape(tp, D_MODEL), x_sample.reshape(bd, D_MODEL)], axis=0)
    kp_l, vp_l, ks_l, vs_l, sp_l, ss_l = [], [], [], [], [], []
    for l in range(depth):
        lam_init = 0.8 - 0.6 * math.exp(-0.3 * l)
        wl = w_in[l]
        w_re = jnp.concatenate([wl[:, :3072], wl[:, 3088:5136], wl[:, 3072:3088],
                                jnp.zeros((D_MODEL, LANES - G_GATE_RANK), f32)], axis=1).astype(bf16)
        wa2 = jnp.concatenate([w_a2[l], jnp.zeros((LANES - G_GATE_RANK, G_K_WIDTH), f32)], axis=0)
        ba = b_a[l].reshape(1, G_K_WIDTH)
        lam_vecs = jnp.stack([lam_q1[l], lam_k1[l], lam_q2[l], lam_k2[l]]).astype(f32)
        subg = sub_g[l].reshape(1, A_VDIM)
        glag = gla_g[l].reshape(1, G_DV)

        wvt = wl[:, C_V:C_GQ].T.astype(bf16)
        q, kf, kb, vf, vt, gq, gk, gv, gr, za, zb, glr = _inproj(x, w_re, wvt)

        oa_p = _attn_prompt(q, kb, vt, lam_vecs, sub_g[l].reshape(A_VDIM, 1), batch, seq, lam_init)
        tail_pad = ((0, 0), (0, 16 - A_HEADS), (0, 0))
        oa_s = _attn_decode(q[tp:].reshape(bd, 1, A_WIDTH),
                            jnp.pad(kf[tp:].reshape(bd, A_HEADS, 2 * A_HEAD_DIM), tail_pad),
                            jnp.pad(vf[tp:].reshape(bd, A_HEADS, A_VDIM), tail_pad),
                            lam_vecs, subg, cache_k4, cache_v4, page_table, l, lam_init)
        og_p, s_p = _gla_prompt(gq, gk, gv, glr, gr, wa2, ba, glag, batch, seq)
        og_s, s_s = _gla_sample(gq[tp:].reshape(bd, 1, -1), gk[tp:].reshape(bd, 1, -1),
                                gv[tp:].reshape(bd, 1, -1), glr[tp:].reshape(bd, 1, -1),
                                gr[tp:].reshape(bd, 1, -1), state_gla[l], wa2, ba, glag)
        oa = jnp.concatenate([oa_p, oa_s.reshape(bd, A_WIDTH)], axis=0)
        og = jnp.concatenate([og_p, og_s.reshape(bd, G_V_WIDTH)], axis=0)

        x1, xp = _postmix(oa, og, za, zb, x, w_pa[l].astype(bf16), w_pb[l].astype(bf16),
                          w_out[l].astype(bf16), ln1_g[l].reshape(1, -1), ln1_b[l].reshape(1, -1), alpha)
        eidx_t, gates_t, rank_t, counts = _router(x1, w_router[l].T, b_router[l].reshape(N_EXPERTS, 1))
        dest_t, blk_seq, seq_exp, n_seq, n_used, n_rows = _layout(eidx_t, rank_t,
                                                                  counts.reshape(-1).astype(i32))
        dest = dest_t.T.reshape(-1)
        xs = _dispatch(xp, dest, n_rows)
        ys = _experts(xs, blk_seq, seq_exp, n_seq, n_used, w_gate, w_up, w_down, l)
        x = _combine(dest, gates_t.T, x1, xp, ys, ws_gate[l].astype(bf16), ws_up[l].astype(bf16),
                     ws_down[l].astype(bf16), ln2_g[l].reshape(1, -1), ln2_b[l].reshape(1, -1), alpha)

        kp_l.append(kf[:tp].reshape(batch, seq // PAGE_SIZE, PAGE_SIZE, A_HEADS, 2 * A_HEAD_DIM))
        vp_l.append(vf[:tp].reshape(batch, seq // PAGE_SIZE, PAGE_SIZE, A_HEADS, A_VDIM))
        ks_l.append(kf[tp:].reshape(bd, 1, A_HEADS, 2 * A_HEAD_DIM))
        vs_l.append(vf[tp:].reshape(bd, 1, A_HEADS, A_VDIM))
        sp_l.append(s_p.reshape(batch, G_HEADS, G_DK, G_DV))
        ss_l.append(s_s)

    y_prompt = x[:tp].reshape(batch, seq, D_MODEL)
    y_sample = x[tp:].reshape(bd, 1, D_MODEL)
    return (y_prompt, y_sample, jnp.stack(kp_l), jnp.stack(vp_l), jnp.stack(ks_l), jnp.stack(vs_l),
            jnp.stack(sp_l), jnp.stack(ss_l))
```

```python
import functools
import math

import jax
import jax.numpy as jnp
from jax import lax
from jax.experimental import pallas as pl
from jax.experimental.pallas import tpu as pltpu

f32 = jnp.float32
bf16 = jnp.bfloat16
u32 = jnp.uint32
i32 = jnp.int32

D_MODEL = 1024
A_HEADS = 4
A_HEAD_DIM = 64
A_VDIM = 128
A_WIDTH = A_HEADS * A_VDIM
G_HEADS = 4
G_DK = 64
G_DV = 128
G_K_WIDTH = G_HEADS * G_DK
G_V_WIDTH = G_HEADS * G_DV
G_GATE_RANK = 16
G_TAU = 16.0
N_EXPERTS = 256
TOP_K = 8
N_GROUPS = 8
TOPK_GROUPS = 4
D_EXPERT = 256
ROUTED_SCALE = 2.5
PAGE_SIZE = 128
LN_EPS = 1e-5
RMS_EPS = 1e-6

LANES = 128
SUBLANES = 8
VMEM_LIMIT = 56 * 1024 * 1024

NEG_BIG = -1e30
HALF = D_MODEL // 2
EXPERT_BLOCK = 256
WEIGHT_SLOTS = 3
GLA_CHUNK = 64
GLA_SUB = 16
ATTN_TQ = 1024
ATTN_TK = 1024
ONES_ROWS = 16
LOG2E = math.log2(math.e)

C_Q, C_K, C_V, C_GQ, C_GK, C_GV, C_GR, C_ZA, C_ZB, C_GLR, C_END = (
    0, 512, 1024, 1536, 1792, 2048, 2560, 3072, 4096, 5120, 5248)


def _params(sem, vmem=VMEM_LIMIT):
    return pltpu.CompilerParams(dimension_semantics=sem, vmem_limit_bytes=vmem)


def _row_tile(n, cands=(512, 384, 256, 128, 64, 32, 16, 8)):
    for c in cands:
        if n % c == 0:
            return c
    raise ValueError(f"no row tile for {n}")


def _sigmoid(x):
    return 1.0 / (1.0 + jnp.exp(-x))


def _pack_rows(x):
    lo = lax.bitcast_convert_type(x[:, :HALF].astype(bf16).astype(f32), u32) >> 16
    hi = lax.bitcast_convert_type(x[:, HALF:].astype(bf16).astype(f32), u32) & jnp.uint32(0xFFFF0000)
    return lo | hi


def _unpack_rows(w):
    lo = lax.bitcast_convert_type(w << 16, f32)
    hi = lax.bitcast_convert_type(w & jnp.uint32(0xFFFF0000), f32)
    return lo, hi


def _layer_norm(h, g, b):
    mu = jnp.mean(h, axis=-1, keepdims=True)
    d = h - mu
    var = jnp.mean(d * d, axis=-1, keepdims=True)
    return d * lax.rsqrt(var + LN_EPS) * g + b


def _rms_norm(o, g):
    return o * lax.rsqrt(jnp.mean(o * o, axis=-1, keepdims=True) + RMS_EPS) * g


def _lam_value(lam_ref, lam_init):
    l = lam_ref[...]
    s1 = jnp.sum(l[0:1] * l[1:2], axis=-1, keepdims=True)
    s2 = jnp.sum(l[2:3] * l[3:4], axis=-1, keepdims=True)
    return jnp.exp(s1) - jnp.exp(s2) + lam_init


def _inproj_body(x_ref, w_ref, wvt_ref, *refs):
    q_o, kf_o, kb_o, vf_o, vt_o, gq_o, gk_o, gv_o, gr_o, za_o, zb_o, glr_o = refs[-12:]
    xb = x_ref[...].astype(bf16)

    def mm(c0, c1):
        return jnp.dot(xb, w_ref[:, c0:c1], preferred_element_type=f32)

    q_o[...] = (mm(C_Q, C_K) * (A_HEAD_DIM ** -0.5 * LOG2E)).astype(bf16)
    k = mm(C_K, C_V)
    kf_o[...] = k
    kb_o[...] = k.astype(bf16)
    vf_o[...] = mm(C_V, C_GQ)
    vt_o[...] = lax.dot_general(wvt_ref[...], xb, (((1,), (1,)), ((), ())),
                                preferred_element_type=f32).astype(bf16)
    gq_o[...] = mm(C_GQ, C_GK) * (G_DK ** -0.5)
    gk_o[...] = mm(C_GK, C_GV)
    gv_o[...] = mm(C_GV, C_GR)
    gr_o[...] = mm(C_GR, C_ZA)
    za_o[...] = mm(C_ZA, C_ZB).astype(bf16)
    zb_o[...] = mm(C_ZB, C_GLR).astype(bf16)
    glr_o[...] = mm(C_GLR, C_END)


def _inproj(x, w, wvt, layer, depth, kv_prev):
    t = x.shape[0]
    tm = _row_tile(t, (384, 256, 128))
    nb = t // tm
    outs = [(512, bf16), (512, f32), (512, bf16), (512, f32), None, (256, f32), (256, f32),
            (512, f32), (512, f32), (1024, bf16), (1024, bf16), (LANES, f32)]
    shared = (1, 3)
    out_specs, out_shape = [], []
    for n, o in enumerate(outs):
        if o is None:
            out_specs.append(pl.BlockSpec((A_WIDTH, tm), lambda i: (0, i)))
            out_shape.append(jax.ShapeDtypeStruct((A_WIDTH, t), bf16))
        elif n in shared:
            out_specs.append(pl.BlockSpec((tm, o[0]), lambda i: (layer * nb + i, 0)))
            out_shape.append(jax.ShapeDtypeStruct((depth * t, o[0]), o[1]))
        else:
            out_specs.append(pl.BlockSpec((tm, o[0]), lambda i: (i, 0)))
            out_shape.append(jax.ShapeDtypeStruct((t, o[0]), o[1]))
    return pl.pallas_call(
        _inproj_body,
        grid=(nb,),
        in_specs=[pl.BlockSpec((tm, D_MODEL), lambda i: (i, 0)),
                  pl.BlockSpec((D_MODEL, C_END), lambda i: (0, 0)),
                  pl.BlockSpec((A_WIDTH, D_MODEL), lambda i: (0, 0))]
        + [pl.BlockSpec(memory_space=pl.ANY)] * len(kv_prev),
        out_specs=out_specs,
        out_shape=out_shape,
        input_output_aliases={3 + n: pos for n, pos in enumerate(shared[:len(kv_prev)])},
        compiler_params=_params(("parallel",)),
    )(x, w, wvt, *kv_prev)


def _attn_body(qi_tab, kj_tab, diag_tab, last_tab, q_ref, k_ref, vt_ref, lam_ref, subg_ref, o_ref,
               m1, a1, m2, a2, *, tq, tk, lam_init):
    p = pl.program_id(2)
    qi = qi_tab[p]
    kj = kj_tab[p]

    @pl.when(kj == 0)
    def _():
        for m, a in ((m1, a1), (m2, a2)):
            m[...] = jnp.full(m.shape, NEG_BIG, f32)
            a[...] = jnp.zeros(a.shape, f32)

    q = q_ref[...]
    k = k_ref[...]
    vt = jnp.concatenate([vt_ref[...], jnp.ones((ONES_ROWS, tk), bf16)], axis=0)
    lane = lax.broadcasted_iota(i32, (1, LANES), 1)
    zero = jnp.zeros_like(q)
    q1 = jnp.where(lane < A_HEAD_DIM, q, zero)
    q2 = jnp.where(lane >= A_HEAD_DIM, q, zero)

    def step(masked):
        for qm, m, a in ((q1, m1, a1), (q2, m2, a2)):
            s = lax.dot_general(k, qm, (((1,), (1,)), ((), ())), preferred_element_type=f32)
            if masked:
                kpos = kj * tk + lax.broadcasted_iota(i32, (tk, tq), 0)
                qpos = qi * tq + lax.broadcasted_iota(i32, (tk, tq), 1)
                s = jnp.where(kpos <= qpos, s, NEG_BIG)
            m_prev = m[...]
            m_new = jnp.maximum(m_prev, jnp.max(s, axis=0, keepdims=True))
            alpha = jnp.exp2(m_prev - m_new)
            pr = jnp.exp2(s - m_new).astype(bf16)
            a[...] = alpha * a[...] + jnp.dot(vt, pr, preferred_element_type=f32)
            m[...] = m_new

    @pl.when(diag_tab[p] == 1)
    def _():
        step(True)

    @pl.when(diag_tab[p] == 0)
    def _():
        step(False)

    @pl.when(last_tab[p] == 1)
    def _():
        lam = _lam_value(lam_ref, lam_init)
        ot = (a1[:A_VDIM, :] / a1[A_VDIM:A_VDIM + 1, :]
              - lam * (a2[:A_VDIM, :] / a2[A_VDIM:A_VDIM + 1, :]))
        ms = jnp.mean(ot * ot, axis=0, keepdims=True)
        on = ot * lax.rsqrt(ms + RMS_EPS) * subg_ref[...] * (1.0 - lam_init)
        o_ref[...] = on.T.astype(o_ref.dtype)


def _attn_prompt(q, k, vt, lam_vecs, sub_g_col, batch, seq, lam_init):
    tq = min(ATTN_TQ, seq)
    tk = min(ATTN_TK, seq)
    nq, nk = seq // tq, seq // tk
    qi_l, kj_l, dg_l, ls_l = [], [], [], []
    for qi in range(nq):
        last = ((qi + 1) * tq - 1) // tk
        for kj in range(last + 1):
            qi_l.append(qi)
            kj_l.append(kj)
            dg_l.append(1 if (kj + 1) * tk - 1 > qi * tq else 0)
            ls_l.append(1 if kj == last else 0)
    tabs = [jnp.asarray(t, i32) for t in (qi_l, kj_l, dg_l, ls_l)]
    n_pairs = len(qi_l)
    body = functools.partial(_attn_body, tq=tq, tk=tk, lam_init=lam_init)
    grid_spec = pltpu.PrefetchScalarGridSpec(
        num_scalar_prefetch=4,
        grid=(batch, A_HEADS, n_pairs),
        in_specs=[
            pl.BlockSpec((tq, LANES), lambda b, h, p, qt, kt, dt, lt: (b * nq + qt[p], h)),
            pl.BlockSpec((tk, LANES), lambda b, h, p, qt, kt, dt, lt: (b * nk + kt[p], h)),
            pl.BlockSpec((A_VDIM, tk), lambda b, h, p, qt, kt, dt, lt: (h, b * nk + kt[p])),
            pl.BlockSpec((4, A_HEAD_DIM), lambda b, h, p, *_: (0, 0)),
            pl.BlockSpec((A_VDIM, 1), lambda b, h, p, *_: (0, 0)),
        ],
        out_specs=pl.BlockSpec((tq, LANES), lambda b, h, p, qt, kt, dt, lt: (b * nq + qt[p], h)),
        scratch_shapes=[pltpu.VMEM((1, tq), f32), pltpu.VMEM((A_VDIM + ONES_ROWS, tq), f32),
                        pltpu.VMEM((1, tq), f32), pltpu.VMEM((A_VDIM + ONES_ROWS, tq), f32)],
    )
    return pl.pallas_call(
        body,
        grid_spec=grid_spec,
        out_shape=jax.ShapeDtypeStruct((batch * seq, A_WIDTH), bf16),
        compiler_params=_params(("parallel", "parallel", "arbitrary")),
    )(*tabs, q, k, vt, lam_vecs, sub_g_col)


def _decode_body(pt_ref, q_ref, kn_ref, vn_ref, lam_ref, subg_ref, *refs, n_pages, lam_init):
    k_refs = refs[:n_pages]
    v_refs = refs[n_pages:2 * n_pages]
    o_ref = refs[2 * n_pages]
    kbuf, vbuf = refs[2 * n_pages + 1:]
    rows_pg = PAGE_SIZE * A_HEADS
    past = n_pages * rows_pg
    tail = 16
    n_col = past + tail

    for p in range(n_pages):
        kbuf[p * rows_pg:(p + 1) * rows_pg, :] = k_refs[p][...].astype(bf16)
        vbuf[p * rows_pg:(p + 1) * rows_pg, :] = v_refs[p][...].astype(bf16)
    kbuf[past:, :] = kn_ref[0].astype(bf16)
    vbuf[past:, :] = vn_ref[0].astype(bf16)

    qrow = q_ref[0].astype(f32)
    row = lax.broadcasted_iota(i32, (16, LANES), 0)
    lane = lax.broadcasted_iota(i32, (16, LANES), 1)
    qmat = jnp.zeros((16, LANES), f32)
    for h in range(A_HEADS):
        qh = jnp.broadcast_to(qrow[:, h * LANES:(h + 1) * LANES], (16, LANES))
        sel = ((row >> 1) == h) & ((lane >= A_HEAD_DIM) == ((row & 1) == 1))
        qmat = jnp.where(sel, qh, qmat)
    s = lax.dot_general(qmat.astype(bf16), kbuf[...], (((1,), (1,)), ((), ())), preferred_element_type=f32)
    srow = lax.broadcasted_iota(i32, (16, n_col), 0)
    scol = lax.broadcasted_iota(i32, (16, n_col), 1)
    valid = ((scol & (A_HEADS - 1)) == (srow >> 1)) & (scol < past + A_HEADS) & (srow < 2 * A_HEADS)
    s = jnp.where(valid, s, NEG_BIG)
    m = jnp.max(s, axis=-1, keepdims=True)
    pr = jnp.where(valid, jnp.exp2(s - m), 0.0)
    den = jnp.maximum(jnp.sum(pr, axis=-1, keepdims=True), 1e-30)
    pn = pr / den
    o8 = jnp.dot(pn.astype(bf16), vbuf[...], preferred_element_type=f32)
    lam = _lam_value(lam_ref, lam_init)
    for h in range(A_HEADS):
        o = o8[2 * h:2 * h + 1, :] - lam * o8[2 * h + 1:2 * h + 2, :]
        o_ref[0, :, h * LANES:(h + 1) * LANES] = (
            _rms_norm(o, subg_ref[...]) * (1.0 - lam_init)).astype(o_ref.dtype)


def _attn_decode(q_s, k_new, v_new, lam_vecs, sub_g, cache_k4, cache_v4, page_table, layer, lam_init):
    bd, n_pages = page_table.shape
    rows_pg = PAGE_SIZE * A_HEADS
    body = functools.partial(_decode_body, n_pages=n_pages, lam_init=lam_init)

    def page_spec(p):
        return pl.BlockSpec((None, None, rows_pg, LANES),
                            lambda b, pt, p=p: (layer, pt[b * n_pages + p], 0, 0))

    grid_spec = pltpu.PrefetchScalarGridSpec(
        num_scalar_prefetch=1,
        grid=(bd,),
        in_specs=[pl.BlockSpec((1, 1, A_WIDTH), lambda b, pt: (b, 0, 0)),
                  pl.BlockSpec((1, 16, LANES), lambda b, pt: (b, 0, 0)),
                  pl.BlockSpec((1, 16, LANES), lambda b, pt: (b, 0, 0)),
                  pl.BlockSpec((4, A_HEAD_DIM), lambda b, pt: (0, 0)),
                  pl.BlockSpec((1, A_VDIM), lambda b, pt: (0, 0))]
        + [page_spec(p) for p in range(n_pages)] * 2,
        out_specs=pl.BlockSpec((1, 1, A_WIDTH), lambda b, pt: (b, 0, 0)),
        scratch_shapes=[pltpu.VMEM((n_pages * rows_pg + 16, LANES), bf16),
                        pltpu.VMEM((n_pages * rows_pg + 16, LANES), bf16)],
    )
    return pl.pallas_call(
        body,
        grid_spec=grid_spec,
        out_shape=jax.ShapeDtypeStruct((bd, 1, A_WIDTH), bf16),
        compiler_params=_params(("arbitrary",)),
    )(page_table.reshape(-1), q_s, k_new, v_new, lam_vecs, sub_g,
      *([cache_k4] * n_pages), *([cache_v4] * n_pages))


def _log_decay(glr, wa2, ba):
    z = jnp.dot(glr, wa2, preferred_element_type=f32, precision=lax.Precision.HIGHEST) + ba
    return (jnp.minimum(z, 0.0) - jnp.log(1.0 + jnp.exp(-jnp.abs(z)))) * (1.0 / G_TAU)


def _gla_finish(o, gr, g):
    return _rms_norm(o, g) * (gr * _sigmoid(gr))


def _gla_prompt_body(q_ref, k_ref, v_ref, glr_ref, gr_ref, wa2_ref, ba_ref, g_ref, o_ref, s_ref, state, tmp,
                     *, c):
    ci = pl.program_id(1)
    nsub = c // GLA_SUB

    @pl.when(ci == 0)
    def _():
        state[...] = jnp.zeros(state.shape, f32)

    la = _log_decay(glr_ref[...], wa2_ref[...], ba_ref[...])
    ri = lax.broadcasted_iota(i32, (c, c), 0)
    cj = lax.broadcasted_iota(i32, (c, c), 1)
    b = jnp.dot((ri >= cj).astype(f32), la, preferred_element_type=f32, precision=lax.Precision.HIGHEST)
    b_last = b[c - 1:c, :]
    q = q_ref[...]
    k = k_ref[...]
    v = v_ref[...]
    vb = v.astype(bf16)
    lane = lax.broadcasted_iota(i32, (1, G_K_WIDTH), 1)
    heads = [(lane >> 6) == h for h in range(G_HEADS)]

    def stack_heads(x):
        return jnp.concatenate([jnp.where(m, x, 0.0) for m in heads], axis=0).astype(bf16)

    sub_i = lax.broadcasted_iota(i32, (GLA_SUB, G_K_WIDTH), 0)
    for blk in range(nsub):
        r0 = blk * GLA_SUB
        q_b = q[r0:r0 + GLA_SUB]
        b_b = b[r0:r0 + GLA_SUB]
        for j in range(GLA_SUB):
            w = jnp.where(sub_i >= j, jnp.exp(jnp.minimum(b_b - b_b[j:j + 1], 0.0)), 0.0)
            t0 = (r0 + j) * GLA_SUB
            tmp[t0:t0 + GLA_SUB, :] = (q_b * w * k[r0 + j:r0 + j + 1]).astype(bf16)
    si = lax.broadcasted_iota(i32, (G_K_WIDTH, G_V_WIDTH), 0)
    sj = lax.broadcasted_iota(i32, (G_K_WIDTH, G_V_WIDTH), 1)
    seg = ((si >> 6) == (sj >> 7)).astype(bf16)
    pair = jnp.dot(tmp[...], seg, preferred_element_type=f32)
    o_rows = []
    for blk in range(nsub):
        r0 = blk * GLA_SUB
        acc = jnp.zeros((GLA_SUB, G_V_WIDTH), f32)
        for j in range(GLA_SUB):
            t0 = (r0 + j) * GLA_SUB
            acc = acc + pair[t0:t0 + GLA_SUB, :] * v[r0 + j:r0 + j + 1]
        o_rows.append(acc)
    o_diag = jnp.concatenate(o_rows, axis=0)

    att_rows = [[jnp.zeros((GLA_SUB, c), f32)] for _ in range(G_HEADS)]
    col = lax.broadcasted_iota(i32, (G_HEADS * GLA_SUB, c), 1)
    for blk in range(1, nsub):
        r0 = blk * GLA_SUB
        ref = b[r0 - 1:r0]
        q_b = q[r0:r0 + GLA_SUB] * jnp.exp(b[r0:r0 + GLA_SUB] - ref)
        k_b = (k * jnp.exp(jnp.minimum(ref - b, 0.0))).astype(bf16)
        a = lax.dot_general(stack_heads(q_b), k_b, (((1,), (1,)), ((), ())), preferred_element_type=f32)
        a = jnp.where(col < r0, a, 0.0)
        for h in range(G_HEADS):
            att_rows[h].append(a[h * GLA_SUB:(h + 1) * GLA_SUB])

    st = state[...]
    o_inter = jnp.dot(stack_heads(q * jnp.exp(b)), st.astype(bf16), preferred_element_type=f32)
    kdt = (k * jnp.exp(b_last - b)).T.astype(bf16)
    upd = jnp.dot(kdt, vb, preferred_element_type=f32)
    dec_col = jnp.exp(jnp.broadcast_to(b_last, (8, G_K_WIDTH)).T[:, 0:1])
    for h in range(G_HEADS):
        cols = slice(h * G_DV, (h + 1) * G_DV)
        att = jnp.concatenate(att_rows[h], axis=0).astype(bf16)
        o = (o_diag[:, cols] + jnp.dot(att, vb[:, cols], preferred_element_type=f32)
             + o_inter[h * c:(h + 1) * c])
        o_ref[:, cols] = _gla_finish(o, gr_ref[:, cols], g_ref[...]).astype(o_ref.dtype)
        r0 = h * G_DK
        state[r0:r0 + G_DK, :] = dec_col[r0:r0 + G_DK] * st[r0:r0 + G_DK, :] + upd[r0:r0 + G_DK, cols]

    @pl.when(ci == pl.num_programs(1) - 1)
    def _():
        s_ref[0] = state[...]


def _gla_prompt(gq, gk, gv, glr, gr, wa2, ba, gla_g, batch, seq):
    c = math.gcd(seq, GLA_CHUNK)
    n = seq // c
    body = functools.partial(_gla_prompt_body, c=c)

    def tok(width):
        return pl.BlockSpec((c, width), lambda b, i: (b * n + i, 0))

    def whole(shape):
        return pl.BlockSpec(shape, lambda b, i: (0,) * len(shape))

    return pl.pallas_call(
        body,
        grid=(batch, n),
        in_specs=[tok(G_K_WIDTH), tok(G_K_WIDTH), tok(G_V_WIDTH), tok(LANES), tok(G_V_WIDTH),
                  whole((LANES, G_K_WIDTH)), whole((1, G_K_WIDTH)), whole((1, G_DV))],
        out_specs=[tok(G_V_WIDTH),
                   pl.BlockSpec((1, G_K_WIDTH, G_DV), lambda b, i: (b, 0, 0))],
        out_shape=[jax.ShapeDtypeStruct((batch * seq, G_V_WIDTH), bf16),
                   jax.ShapeDtypeStruct((batch, G_K_WIDTH, G_DV), f32)],
        scratch_shapes=[pltpu.VMEM((G_K_WIDTH, G_DV), f32),
                        pltpu.VMEM((c * GLA_SUB, G_K_WIDTH), bf16)],
        compiler_params=_params(("parallel", "arbitrary")),
    )(gq, gk, gv, glr, gr, wa2, ba, gla_g)


def _gla_sample_body(q_ref, k_ref, v_ref, glr_ref, gr_ref, s_ref, wa2_ref, ba_ref, g_ref, o_ref, sn_ref):
    la = _log_decay(glr_ref[0], wa2_ref[...], ba_ref[...])
    rows = jnp.concatenate([jnp.broadcast_to(jnp.exp(la), (G_DK, G_K_WIDTH)),
                            jnp.broadcast_to(k_ref[0], (G_DK, G_K_WIDTH)),
                            jnp.broadcast_to(q_ref[0], (G_DK, G_K_WIDTH))], axis=0)
    ri = lax.broadcasted_iota(i32, (3 * G_DK, G_K_WIDTH), 0)
    li = lax.broadcasted_iota(i32, (3 * G_DK, G_K_WIDTH), 1)
    picked = jnp.where((ri & (G_DK - 1)) == (li & (G_DK - 1)), rows, 0.0)
    si = lax.broadcasted_iota(i32, (G_K_WIDTH, G_V_WIDTH), 0)
    sj = lax.broadcasted_iota(i32, (G_K_WIDTH, G_V_WIDTH), 1)
    seg = ((si >> 6) == (sj >> 7)).astype(f32)
    cols = jnp.dot(picked, seg, preferred_element_type=f32, precision=lax.Precision.HIGHEST)
    v = v_ref[0]
    for h in range(G_HEADS):
        sl = slice(h * G_DV, (h + 1) * G_DV)
        a_c = cols[0:G_DK, sl]
        k_c = cols[G_DK:2 * G_DK, sl]
        q_c = cols[2 * G_DK:3 * G_DK, sl]
        s_new = a_c * s_ref[0, h] + k_c * v[:, sl]
        sn_ref[0, h] = s_new
        o = jnp.sum(q_c * s_new, axis=0, keepdims=True)
        o_ref[0, :, sl] = _gla_finish(o, gr_ref[0][:, sl], g_ref[...]).astype(o_ref.dtype)


def _gla_sample(gq, gk, gv, glr, gr, s0, wa2, ba, gla_g):
    bd = gq.shape[0]

    def vec(width):
        return pl.BlockSpec((1, 1, width), lambda b: (b, 0, 0))

    def whole(shape):
        return pl.BlockSpec(shape, lambda b: (0,) * len(shape))

    st = pl.BlockSpec((1, G_HEADS, G_DK, G_DV), lambda b: (b, 0, 0, 0))
    return pl.pallas_call(
        _gla_sample_body,
        grid=(bd,),
        in_specs=[vec(G_K_WIDTH), vec(G_K_WIDTH), vec(G_V_WIDTH), vec(LANES), vec(G_V_WIDTH), st,
                  whole((LANES, G_K_WIDTH)), whole((1, G_K_WIDTH)), whole((1, G_DV))],
        out_specs=[vec(G_V_WIDTH), st],
        out_shape=[jax.ShapeDtypeStruct((bd, 1, G_V_WIDTH), bf16),
                   jax.ShapeDtypeStruct((bd, G_HEADS, G_DK, G_DV), f32)],
        compiler_params=_params(("parallel",)),
    )(gq, gk, gv, glr, gr, s0, wa2, ba, gla_g)


def _postmix_body(oa_ref, og_ref, za_ref, zb_ref, x_ref, wpa_ref, wpb_ref, wout_ref, g_ref, b_ref,
                  x1_o, xp_o, *, alpha):
    ya = jnp.dot(oa_ref[...], wpa_ref[...], preferred_element_type=f32)
    yb = jnp.dot(og_ref[...], wpb_ref[...], preferred_element_type=f32)
    merged = _sigmoid(za_ref[...].astype(f32)) * ya + _sigmoid(zb_ref[...].astype(f32)) * yb
    mix = jnp.dot(merged.astype(bf16), wout_ref[...], preferred_element_type=f32)
    x1 = _layer_norm(alpha * x_ref[...] + mix, g_ref[...], b_ref[...])
    x1_o[...] = x1
    xp_o[...] = _pack_rows(x1)


def _postmix(oa, og, za, zb, x, wpa, wpb, wout, g, b, alpha):
    t = x.shape[0]
    tm = _row_tile(t, (384, 256, 128, 64, 32, 16, 8))

    def tok(width):
        return pl.BlockSpec((tm, width), lambda i: (i, 0))

    def whole(shape):
        return pl.BlockSpec(shape, lambda i: (0,) * len(shape))

    return pl.pallas_call(
        functools.partial(_postmix_body, alpha=alpha),
        grid=(t // tm,),
        in_specs=[tok(A_WIDTH), tok(G_V_WIDTH), tok(D_MODEL), tok(D_MODEL), tok(D_MODEL),
                  whole((A_WIDTH, D_MODEL)), whole((G_V_WIDTH, D_MODEL)), whole((D_MODEL, D_MODEL)),
                  whole((1, D_MODEL)), whole((1, D_MODEL))],
        out_specs=[tok(D_MODEL), tok(HALF)],
        out_shape=[jax.ShapeDtypeStruct((t, D_MODEL), f32),
                   jax.ShapeDtypeStruct((t, HALF), u32)],
        compiler_params=_params(("parallel",)),
    )(oa, og, za, zb, x, wpa, wpb, wout, g, b)


def _router_body(x1_ref, wrt_ref, bcol_ref, tri_ref, eidx_o, gate_o, rank_o, cnt_o, cnt, *, tm):
    @pl.when(pl.program_id(0) == 0)
    def _():
        cnt[...] = jnp.zeros(cnt.shape, f32)

    logits = lax.dot_general(wrt_ref[...], x1_ref[...], (((1,), (1,)), ((), ())),
                             preferred_element_type=f32, precision=lax.Precision.HIGHEST)
    scores = _sigmoid(logits)
    biased = scores + bcol_ref[...]
    gsz = N_EXPERTS // N_GROUPS
    neg_inf = -jnp.inf

    gi = lax.broadcasted_iota(i32, (gsz, tm), 0)
    segs, gscore = [], []
    for g in range(N_GROUPS):
        seg = biased[g * gsz:(g + 1) * gsz, :]
        m1 = jnp.max(seg, axis=0, keepdims=True)
        i1 = jnp.min(jnp.where(seg == m1, gi, gsz), axis=0, keepdims=True)
        m2 = jnp.max(jnp.where(gi == i1, neg_inf, seg), axis=0, keepdims=True)
        segs.append(seg)
        gscore.append(m1 + m2)
    parts = []
    for g in range(N_GROUPS):
        beat = jnp.zeros((1, tm), i32)
        for o in range(N_GROUPS):
            if o != g:
                wins = (gscore[o] > gscore[g]) | ((gscore[o] == gscore[g]) & (o < g))
                beat = beat + wins.astype(i32)
        parts.append(jnp.where(beat < TOPK_GROUPS, segs[g], neg_inf))
    masked = jnp.concatenate(parts, axis=0)

    ei = lax.broadcasted_iota(i32, (N_EXPERTS, tm), 0)
    sel_rows, idx_rows = [], []
    chosen = jnp.zeros((N_EXPERTS, tm), f32)
    for _ in range(TOP_K):
        m = jnp.max(masked, axis=0, keepdims=True)
        idx = jnp.min(jnp.where(masked == m, ei, N_EXPERTS), axis=0, keepdims=True)
        hit = ei == idx
        sel_rows.append(jnp.sum(jnp.where(hit, scores, 0.0), axis=0, keepdims=True))
        idx_rows.append(idx)
        chosen = jnp.where(hit, 1.0, chosen)
        masked = jnp.where(hit, neg_inf, masked)
    s_sel = jnp.concatenate(sel_rows, axis=0)
    gate_o[...] = s_sel / jnp.sum(s_sel, axis=0, keepdims=True) * ROUTED_SCALE
    eidx_o[...] = jnp.concatenate(idx_rows, axis=0)
    before = jnp.dot(chosen.astype(bf16), tri_ref[...], preferred_element_type=f32) + cnt[...]
    rank_rows = [jnp.sum(jnp.where(ei == idx, before, 0.0), axis=0, keepdims=True) for idx in idx_rows]
    rank_o[...] = jnp.concatenate(rank_rows, axis=0).astype(i32)
    cnt[...] = cnt[...] + jnp.sum(chosen, axis=1, keepdims=True)
    cnt_o[...] = cnt[...]


def _router(x1, wrt, b_col):
    t = x1.shape[0]
    tm = _row_tile(t, (384, 256, 128))
    tri = (jnp.arange(tm)[:, None] < jnp.arange(tm)[None, :]).astype(bf16)

    def tokcol(dt):
        return pl.BlockSpec((TOP_K, tm), lambda i: (0, i)), jax.ShapeDtypeStruct((TOP_K, t), dt)

    def whole(shape):
        return pl.BlockSpec(shape, lambda i: (0,) * len(shape))

    specs, shapes = zip(tokcol(i32), tokcol(f32), tokcol(i32),
                        (whole((N_EXPERTS, 1)), jax.ShapeDtypeStruct((N_EXPERTS, 1), f32)))
    return pl.pallas_call(
        functools.partial(_router_body, tm=tm),
        grid=(t // tm,),
        in_specs=[pl.BlockSpec((tm, D_MODEL), lambda i: (i, 0)), whole((N_EXPERTS, D_MODEL)),
                  whole((N_EXPERTS, 1)), whole((tm, tm))],
        out_specs=list(specs),
        out_shape=list(shapes),
        scratch_shapes=[pltpu.VMEM((N_EXPERTS, 1), f32)],
        compiler_params=_params(("arbitrary",)),
    )(x1, wrt, b_col, tri)


def _dispatch_body(dest_ref, pad0_ref, pad1_ref, x_ref, xs_out, zbuf, sem, zsem, *, tm):

    @pl.when(pl.program_id(0) == 0)
    def _():
        zbuf[...] = jnp.zeros(zbuf.shape, u32)

        def zero_row(row):
            return pltpu.make_async_copy(zbuf.at[pl.ds(0, 1)], xs_out.at[pl.ds(row, 1)], zsem)

        def zero_group(row):
            return pltpu.make_async_copy(zbuf, xs_out.at[pl.ds(pl.multiple_of(row, SUBLANES), SUBLANES)], zsem)

        def per_expert(e, waiting):
            p0 = pad0_ref[e]
            p1 = pad1_ref[e]
            head = jnp.minimum((-p0) & (SUBLANES - 1), p1 - p0)
            groups = lax.shift_right_logical(p1 - p0 - head, 3)

            def rows(r, c):
                if waiting:
                    zero_row(0).wait()
                else:
                    zero_row(p0 + r).start()
                return c

            def grps(g, c):
                if waiting:
                    zero_group(0).wait()
                else:
                    zero_group(p0 + head + g * SUBLANES).start()
                return c

            lax.fori_loop(0, head, rows, 0)
            lax.fori_loop(0, groups, grps, 0)

        lax.fori_loop(0, N_EXPERTS, lambda e, c: (per_expert(e, False), c)[1], 0)
        lax.fori_loop(0, N_EXPERTS, lambda e, c: (per_expert(e, True), c)[1], 0)

    def copy(grp, sub, dst_row):
        return pltpu.make_async_copy(x_ref.at[grp, pl.ds(sub, 1)], xs_out.at[pl.ds(dst_row, 1)], sem)

    def issue(grp, carry):
        for sub in range(SUBLANES):
            for kk in range(TOP_K):
                copy(grp, sub, dest_ref[(grp * SUBLANES + sub) * TOP_K + kk]).start(priority=kk % 2)
        return carry

    lax.fori_loop(0, tm // SUBLANES, issue, 0)

    def drain(grp, carry):
        for _ in range(SUBLANES * TOP_K):
            copy(0, 0, 0).wait()
        return carry

    lax.fori_loop(0, tm // SUBLANES, drain, 0)


def _dispatch(xp, dest, pad0, pad1, n_rows):
    t = xp.shape[0]
    tm = _row_tile(t, (384, 256, 128, 64, 32, 16, 8))
    whole = pl.BlockSpec((N_EXPERTS,), lambda i: (0,), memory_space=pltpu.SMEM)
    return pl.pallas_call(
        functools.partial(_dispatch_body, tm=tm),
        grid=(t // tm,),
        in_specs=[pl.BlockSpec((tm * TOP_K,), lambda i: (i,), memory_space=pltpu.SMEM), whole, whole,
                  pl.BlockSpec((tm // SUBLANES, SUBLANES, HALF), lambda i: (i, 0, 0))],
        out_specs=pl.BlockSpec(memory_space=pl.ANY),
        out_shape=jax.ShapeDtypeStruct((n_rows, HALF), u32),
        scratch_shapes=[pltpu.VMEM((SUBLANES, HALF), u32), pltpu.SemaphoreType.DMA(()),
                        pltpu.SemaphoreType.DMA(())],
        compiler_params=_params(("arbitrary",)),
    )(dest, pad0, pad1, xp.reshape(t // SUBLANES, SUBLANES, HALF))


def _expert_weight_copies(seq_ref, w_hbm, w_buf, sems, seq_idx, *, layer):
    e = seq_ref[seq_idx]
    slot = lax.rem(seq_idx, WEIGHT_SLOTS)
    return [pltpu.make_async_copy(w_hbm[n].at[layer, e], w_buf[n].at[slot], sems.at[slot, n]) for n in range(3)]


def _expert_body(bs_ref, seq_ref, ns_ref, nu_ref, xs_ref, wg_hbm, wu_hbm, wd_hbm, ys_ref,
                 wgf, wuf, wdf, sems, wgb, wub, wdb, *, layer):
    i = pl.program_id(0)
    j = bs_ref[i]
    copies = functools.partial(_expert_weight_copies, seq_ref, (wg_hbm, wu_hbm, wd_hbm), (wgf, wuf, wdf),
                               sems, layer=layer)

    @pl.when(i == 0)
    def _():
        for c in copies(0):
            c.start()

        @pl.when(ns_ref[0] > 1)
        def _():
            for c in copies(1):
                c.start()

    @pl.when((i == 0) | (j != bs_ref[jnp.maximum(i - 1, 0)]))
    def _():
        for c in copies(j):
            c.wait()
        slot = lax.rem(j, WEIGHT_SLOTS)
        wgb[...] = wgf[slot].astype(bf16)
        wub[...] = wuf[slot].astype(bf16)
        wdb[...] = wdf[slot].astype(bf16)

        @pl.when(j + 2 < ns_ref[0])
        def _():
            for c in copies(j + 2):
                c.start()

    @pl.when(i < nu_ref[0])
    def _():
        lo, hi = _unpack_rows(xs_ref[...])
        lo = lo.astype(bf16)
        hi = hi.astype(bf16)
        g = (jnp.dot(lo, wgb[:HALF, :], preferred_element_type=f32)
             + jnp.dot(hi, wgb[HALF:, :], preferred_element_type=f32))
        u = (jnp.dot(lo, wub[:HALF, :], preferred_element_type=f32)
             + jnp.dot(hi, wub[HALF:, :], preferred_element_type=f32))
        hdn = (g * _sigmoid(g) * u).astype(bf16)
        ys_ref[...] = _pack_rows(jnp.dot(hdn, wdb[...], preferred_element_type=f32))

    @pl.when(i >= nu_ref[0])
    def _():
        ys_ref[...] = jnp.zeros(ys_ref.shape, u32)


def _experts(xs, blk_seq, seq_exp, n_seq, n_used, w_gate, w_up, w_down, layer):
    n_rows = xs.shape[0]
    n_blocks = n_rows // EXPERT_BLOCK
    hbm = pl.BlockSpec(memory_space=pl.ANY)
    grid_spec = pltpu.PrefetchScalarGridSpec(
        num_scalar_prefetch=4,
        grid=(n_blocks,),
        in_specs=[pl.BlockSpec((EXPERT_BLOCK, HALF), lambda i, bs, sq, ns, nu: (jnp.minimum(i, nu[0] - 1), 0)),
                  hbm, hbm, hbm],
        out_specs=pl.BlockSpec((EXPERT_BLOCK, HALF), lambda i, bs, sq, ns, nu: (i, 0)),
        scratch_shapes=[pltpu.VMEM((WEIGHT_SLOTS, D_MODEL, D_EXPERT), f32),
                        pltpu.VMEM((WEIGHT_SLOTS, D_MODEL, D_EXPERT), f32),
                        pltpu.VMEM((WEIGHT_SLOTS, D_EXPERT, D_MODEL), f32),
                        pltpu.SemaphoreType.DMA((WEIGHT_SLOTS, 3)),
                        pltpu.VMEM((D_MODEL, D_EXPERT), bf16), pltpu.VMEM((D_MODEL, D_EXPERT), bf16),
                        pltpu.VMEM((D_EXPERT, D_MODEL), bf16)],
    )
    return pl.pallas_call(
        functools.partial(_expert_body, layer=layer),
        grid_spec=grid_spec,
        out_shape=jax.ShapeDtypeStruct((n_rows, HALF), u32),
        compiler_params=_params(("arbitrary",)),
    )(blk_seq, seq_exp, n_seq, n_used, xs, w_gate, w_up, w_down)


def _combine_body(dest_ref, gate_ref, x1_ref, xp_ref, ys_hbm, wsg_ref, wsu_ref, wsd_ref, g_ref, b_ref,
                  o_ref, buf, sem, *, tm, alpha):
    def copy(src_row, kk, grp, sub):
        return pltpu.make_async_copy(ys_hbm.at[pl.ds(src_row, 1)], buf.at[kk, grp, pl.ds(sub, 1)], sem)

    def issue(grp, carry):
        for sub in range(SUBLANES):
            for kk in range(TOP_K):
                copy(dest_ref[(grp * SUBLANES + sub) * TOP_K + kk], kk, grp, sub).start(priority=kk % 2)
        return carry

    lax.fori_loop(0, tm // SUBLANES, issue, 0)

    lo, hi = _unpack_rows(xp_ref[...])
    lo = lo.astype(bf16)
    hi = hi.astype(bf16)
    sg = (jnp.dot(lo, wsg_ref[:HALF, :], preferred_element_type=f32)
          + jnp.dot(hi, wsg_ref[HALF:, :], preferred_element_type=f32))
    su = (jnp.dot(lo, wsu_ref[:HALF, :], preferred_element_type=f32)
          + jnp.dot(hi, wsu_ref[HALF:, :], preferred_element_type=f32))
    shared = jnp.dot((sg * _sigmoid(sg) * su).astype(bf16), wsd_ref[...], preferred_element_type=f32)

    def drain(grp, carry):
        for _ in range(SUBLANES * TOP_K):
            copy(0, 0, 0, 0).wait()
        return carry

    lax.fori_loop(0, tm // SUBLANES, drain, 0)

    gates = gate_ref[...]
    acc_lo = jnp.zeros((tm, HALF), f32)
    acc_hi = jnp.zeros((tm, HALF), f32)
    for kk in range(TOP_K):
        ylo, yhi = _unpack_rows(buf[kk].reshape(tm, HALF))
        gk = gates[:, kk:kk + 1]
        acc_lo = acc_lo + gk * ylo
        acc_hi = acc_hi + gk * yhi
    moe = jnp.concatenate([acc_lo, acc_hi], axis=-1) + shared
    o_ref[...] = _layer_norm(alpha * x1_ref[...] + moe, g_ref[...], b_ref[...])


def _combine(dest, gates, x1, xp, ys, wsg, wsu, wsd, g, b, alpha):
    t = x1.shape[0]
    tm = _row_tile(t, (384, 256, 128, 64, 32, 16, 8))

    def tok(width):
        return pl.BlockSpec((tm, width), lambda i: (i, 0))

    def whole(shape):
        return pl.BlockSpec(shape, lambda i: (0,) * len(shape))

    return pl.pallas_call(
        functools.partial(_combine_body, tm=tm, alpha=alpha),
        grid=(t // tm,),
        in_specs=[pl.BlockSpec((tm * TOP_K,), lambda i: (i,), memory_space=pltpu.SMEM),
                  tok(TOP_K), tok(D_MODEL), tok(HALF),
                  pl.BlockSpec(memory_space=pl.ANY),
                  whole((D_MODEL, D_EXPERT)), whole((D_MODEL, D_EXPERT)), whole((D_EXPERT, D_MODEL)),
                  whole((1, D_MODEL)), whole((1, D_MODEL))],
        out_specs=tok(D_MODEL),
        out_shape=jax.ShapeDtypeStruct((t, D_MODEL), f32),
        scratch_shapes=[pltpu.VMEM((TOP_K, tm // SUBLANES, SUBLANES, HALF), u32),
                        pltpu.SemaphoreType.DMA(())],
        compiler_params=_params(("arbitrary",)),
    )(dest, gates, x1, xp, ys, wsg, wsu, wsd, g, b)


def _dest_body(eidx_ref, rank_ref, ps_ref, dest_o, *, tm):
    ei = lax.broadcasted_iota(i32, (N_EXPERTS, tm), 0)
    ps = ps_ref[...]
    rows = [jnp.sum(jnp.where(ei == eidx_ref[kk:kk + 1, :], ps, 0.0), axis=0, keepdims=True)
            for kk in range(TOP_K)]
    dest_o[...] = jnp.concatenate(rows, axis=0).astype(i32) + rank_ref[...]


def _dest_rows(eidx_t, rank_t, pad_start):
    t = eidx_t.shape[1]
    tm = _row_tile(t, (384, 256, 128))
    blk = pl.BlockSpec((TOP_K, tm), lambda i: (0, i))
    return pl.pallas_call(
        functools.partial(_dest_body, tm=tm),
        grid=(t // tm,),
        in_specs=[blk, blk, pl.BlockSpec((N_EXPERTS, 1), lambda i: (0, 0))],
        out_specs=blk,
        out_shape=jax.ShapeDtypeStruct((TOP_K, t), i32),
        compiler_params=_params(("parallel",)),
    )(eidx_t, rank_t, pad_start.astype(f32).reshape(N_EXPERTS, 1))


def _layout(eidx_t, rank_t, counts):
    t = eidx_t.shape[1]
    blk = EXPERT_BLOCK
    padded = (counts + blk - 1) // blk * blk
    pad_end = jnp.cumsum(padded)
    pad_start = pad_end - padded
    dest = _dest_rows(eidx_t, rank_t, pad_start)
    n_rows = -(-(t * TOP_K + N_EXPERTS * (blk - 1)) // blk) * blk
    n_blocks = n_rows // blk
    n_used = (pad_end[-1] // blk).astype(i32)
    first_row = jnp.minimum(jnp.arange(n_blocks), n_used - 1) * blk
    blk_exp = jnp.sum(pad_end[None, :] <= first_row[:, None], axis=1).astype(i32)
    blk_exp = jnp.minimum(blk_exp, N_EXPERTS - 1)
    used = counts > 0
    seq_of = jnp.cumsum(used.astype(i32)) - 1
    ids = jnp.arange(N_EXPERTS, dtype=i32)
    seq_exp = jnp.sum(jnp.where(used[None, :] & (seq_of[None, :] == ids[:, None]), ids[None, :], 0), axis=1)
    blk_seq = jnp.sum(jnp.where(blk_exp[:, None] == ids[None, :], seq_of[None, :], 0), axis=1)
    n_seq = jnp.sum(used.astype(i32))
    pads = ((pad_start + counts).astype(i32), pad_end.astype(i32))
    return dest, pads, blk_seq.astype(i32), seq_exp.astype(i32), n_seq.reshape(1), n_used.reshape(1), n_rows


def kernel(x_prompt, x_sample, cache_k, cache_v, state_gla, page_table, w_in, w_a2, b_a, lam_q1, lam_k1,
           lam_q2, lam_k2, sub_g, gla_g, w_pa, w_pb, w_out, ln1_g, ln1_b, w_router, b_router, w_gate, w_up,
           w_down, ws_gate, ws_up, ws_down, ln2_g, ln2_b):
    depth = w_in.shape[0]
    batch, seq, _ = x_prompt.shape
    bd = x_sample.shape[0]
    tp = batch * seq
    alpha = (2 * depth) ** 0.25
    n_pool = cache_k.shape[1]
    cache_k4 = cache_k.reshape(depth, n_pool, PAGE_SIZE * A_HEADS, 2 * A_HEAD_DIM)
    cache_v4 = cache_v.reshape(depth, n_pool, PAGE_SIZE * A_HEADS, A_VDIM)

    x = jnp.concatenate([x_prompt.reshape(tp, D_MODEL), x_sample.reshape(bd, D_MODEL)], axis=0)
    t_all = tp + bd
    kv_all = ()
    sp_l, ss_l = [], []
    for l in range(depth):
        lam_init = 0.8 - 0.6 * math.exp(-0.3 * l)
        wl = w_in[l]
        w_re = jnp.concatenate([wl[:, :3072], wl[:, 3088:5136], wl[:, 3072:3088],
                                jnp.zeros((D_MODEL, LANES - G_GATE_RANK), f32)], axis=1).astype(bf16)
        wa2 = jnp.concatenate([w_a2[l], jnp.zeros((LANES - G_GATE_RANK, G_K_WIDTH), f32)], axis=0)
        ba = b_a[l].reshape(1, G_K_WIDTH)
        lam_vecs = jnp.stack([lam_q1[l], lam_k1[l], lam_q2[l], lam_k2[l]]).astype(f32)
        subg = sub_g[l].reshape(1, A_VDIM)
        glag = gla_g[l].reshape(1, G_DV)

        wvt = wl[:, C_V:C_GQ].T.astype(bf16)
        q, kf_all, kb, vf_all, vt, gq, gk, gv, gr, za, zb, glr = _inproj(x, w_re, wvt, l, depth, kv_all)
        kv_all = (kf_all, vf_all)
        kf = kf_all[l * t_all + tp:(l + 1) * t_all]
        vf = vf_all[l * t_all + tp:(l + 1) * t_all]

        oa_p = _attn_prompt(q, kb, vt, lam_vecs, sub_g[l].reshape(A_VDIM, 1), batch, seq, lam_init)
        tail_pad = ((0, 0), (0, 16 - A_HEADS), (0, 0))
        oa_s = _attn_decode(q[tp:].reshape(bd, 1, A_WIDTH),
                            jnp.pad(kf.reshape(bd, A_HEADS, 2 * A_HEAD_DIM), tail_pad),
                            jnp.pad(vf.reshape(bd, A_HEADS, A_VDIM), tail_pad),
                            lam_vecs, subg, cache_k4, cache_v4, page_table, l, lam_init)
        og_p, s_p = _gla_prompt(gq, gk, gv, glr, gr, wa2, ba, glag, batch, seq)
        og_s, s_s = _gla_sample(gq[tp:].reshape(bd, 1, -1), gk[tp:].reshape(bd, 1, -1),
                                gv[tp:].reshape(bd, 1, -1), glr[tp:].reshape(bd, 1, -1),
                                gr[tp:].reshape(bd, 1, -1), state_gla[l], wa2, ba, glag)
        oa = jnp.concatenate([oa_p, oa_s.reshape(bd, A_WIDTH)], axis=0)
        og = jnp.concatenate([og_p, og_s.reshape(bd, G_V_WIDTH)], axis=0)

        x1, xp = _postmix(oa, og, za, zb, x, w_pa[l].astype(bf16), w_pb[l].astype(bf16),
                          w_out[l].astype(bf16), ln1_g[l].reshape(1, -1), ln1_b[l].reshape(1, -1), alpha)
        eidx_t, gates_t, rank_t, counts = _router(x1, w_router[l].T, b_router[l].reshape(N_EXPERTS, 1))
        dest_t, pads, blk_seq, seq_exp, n_seq, n_used, n_rows = _layout(eidx_t, rank_t,
                                                                        counts.reshape(-1).astype(i32))
        dest = dest_t.T.reshape(-1)
        xs = _dispatch(xp, dest, pads[0], pads[1], n_rows)
        ys = _experts(xs, blk_seq, seq_exp, n_seq, n_used, w_gate, w_up, w_down, l)
        x = _combine(dest, gates_t.T, x1, xp, ys, ws_gate[l].astype(bf16), ws_up[l].astype(bf16),
                     ws_down[l].astype(bf16), ln2_g[l].reshape(1, -1), ln2_b[l].reshape(1, -1), alpha)

        sp_l.append(s_p.reshape(batch, G_HEADS, G_DK, G_DV))
        ss_l.append(s_s)

    y_prompt = x[:tp].reshape(batch, seq, D_MODEL)
    y_sample = x[tp:].reshape(bd, 1, D_MODEL)
    k3 = kv_all[0].reshape(depth, t_all, A_WIDTH)
    v3 = kv_all[1].reshape(depth, t_all, A_WIDTH)
    pages = (depth, batch, seq // PAGE_SIZE, PAGE_SIZE, A_HEADS, A_VDIM)
    return (y_prompt, y_sample, k3[:, :tp].reshape(pages), v3[:, :tp].reshape(pages),
            k3[:, tp:].reshape(depth, bd, 1, A_HEADS, 2 * A_HEAD_DIM),
            v3[:, tp:].reshape(depth, bd, 1, A_HEADS, A_VDIM), jnp.stack(sp_l), jnp.stack(ss_l))
```

```python
import functools
import math

import jax
import jax.numpy as jnp
from jax import lax
from jax.experimental import pallas as pl
from jax.experimental.pallas import tpu as pltpu

f32 = jnp.float32
bf16 = jnp.bfloat16
u32 = jnp.uint32
i32 = jnp.int32

D_MODEL = 1024
A_HEADS = 4
A_HEAD_DIM = 64
A_VDIM = 128
A_WIDTH = A_HEADS * A_VDIM
G_HEADS = 4
G_DK = 64
G_DV = 128
G_K_WIDTH = G_HEADS * G_DK
G_V_WIDTH = G_HEADS * G_DV
G_GATE_RANK = 16
G_TAU = 16.0
N_EXPERTS = 256
TOP_K = 8
N_GROUPS = 8
TOPK_GROUPS = 4
D_EXPERT = 256
ROUTED_SCALE = 2.5
PAGE_SIZE = 128
LN_EPS = 1e-5
RMS_EPS = 1e-6

LANES = 128
SUBLANES = 8
VMEM_LIMIT = 56 * 1024 * 1024

NEG_BIG = -1e30
HALF = D_MODEL // 2
EXPERT_BLOCK = 256
WEIGHT_SLOTS = 3
GLA_CHUNK = 64
GLA_SUB = 16
ATTN_TQ = 1024
ATTN_TK = 1024
ONES_ROWS = 16
LOG2E = math.log2(math.e)

C_Q, C_K, C_V, C_GQ, C_GK, C_GV, C_GR, C_ZA, C_ZB, C_GLR, C_END = (
    0, 512, 1024, 1536, 1792, 2048, 2560, 3072, 4096, 5120, 5248)


def _params(sem, vmem=VMEM_LIMIT):
    return pltpu.CompilerParams(dimension_semantics=sem, vmem_limit_bytes=vmem)


def _row_tile(n, cands=(512, 384, 256, 128, 64, 32, 16, 8)):
    for c in cands:
        if n % c == 0:
            return c
    raise ValueError(f"no row tile for {n}")


def _sigmoid(x):
    return 1.0 / (1.0 + jnp.exp(-x))


def _pack_rows(x):
    lo = lax.bitcast_convert_type(x[:, :HALF].astype(bf16).astype(f32), u32) >> 16
    hi = lax.bitcast_convert_type(x[:, HALF:].astype(bf16).astype(f32), u32) & jnp.uint32(0xFFFF0000)
    return lo | hi


def _unpack_rows(w):
    lo = lax.bitcast_convert_type(w << 16, f32)
    hi = lax.bitcast_convert_type(w & jnp.uint32(0xFFFF0000), f32)
    return lo, hi


def _layer_norm(h, g, b):
    mu = jnp.mean(h, axis=-1, keepdims=True)
    d = h - mu
    var = jnp.mean(d * d, axis=-1, keepdims=True)
    return d * lax.rsqrt(var + LN_EPS) * g + b


def _rms_norm(o, g):
    return o * lax.rsqrt(jnp.mean(o * o, axis=-1, keepdims=True) + RMS_EPS) * g


def _lam_value(lam_ref, lam_init):
    l = lam_ref[...]
    s1 = jnp.sum(l[0:1] * l[1:2], axis=-1, keepdims=True)
    s2 = jnp.sum(l[2:3] * l[3:4], axis=-1, keepdims=True)
    return jnp.exp(s1) - jnp.exp(s2) + lam_init


def _inproj_body(x_ref, w_ref, wvt_ref, *refs):
    q_o, kf_o, kb_o, vf_o, vt_o, gq_o, gk_o, gv_o, gr_o, za_o, zb_o, glr_o = refs[-12:]
    xb = x_ref[...].astype(bf16)

    def mm(c0, c1):
        return jnp.dot(xb, w_ref[:, c0:c1], preferred_element_type=f32)

    q_o[...] = (mm(C_Q, C_K) * (A_HEAD_DIM ** -0.5 * LOG2E)).astype(bf16)
    k = mm(C_K, C_V)
    kf_o[...] = k
    kb_o[...] = k.astype(bf16)
    vf_o[...] = mm(C_V, C_GQ)
    vt_o[...] = lax.dot_general(wvt_ref[...], xb, (((1,), (1,)), ((), ())),
                                preferred_element_type=f32).astype(bf16)
    gq_o[...] = mm(C_GQ, C_GK) * (G_DK ** -0.5)
    gk_o[...] = mm(C_GK, C_GV)
    gv_o[...] = mm(C_GV, C_GR)
    gr_o[...] = mm(C_GR, C_ZA)
    za_o[...] = mm(C_ZA, C_ZB).astype(bf16)
    zb_o[...] = mm(C_ZB, C_GLR).astype(bf16)
    glr_o[...] = mm(C_GLR, C_END)


def _inproj(x, w, wvt, layer, depth, kv_prev):
    t = x.shape[0]
    tm = _row_tile(t, (384, 256, 128))
    nb = t // tm
    outs = [(512, bf16), (512, f32), (512, bf16), (512, f32), None, (256, f32), (256, f32),
            (512, f32), (512, f32), (1024, bf16), (1024, bf16), (LANES, f32)]
    shared = (1, 3)
    out_specs, out_shape = [], []
    for n, o in enumerate(outs):
        if o is None:
            out_specs.append(pl.BlockSpec((A_WIDTH, tm), lambda i: (0, i)))
            out_shape.append(jax.ShapeDtypeStruct((A_WIDTH, t), bf16))
        elif n in shared:
            out_specs.append(pl.BlockSpec((tm, o[0]), lambda i: (layer * nb + i, 0)))
            out_shape.append(jax.ShapeDtypeStruct((depth * t, o[0]), o[1]))
        else:
            out_specs.append(pl.BlockSpec((tm, o[0]), lambda i: (i, 0)))
            out_shape.append(jax.ShapeDtypeStruct((t, o[0]), o[1]))
    return pl.pallas_call(
        _inproj_body,
        grid=(nb,),
        in_specs=[pl.BlockSpec((tm, D_MODEL), lambda i: (i, 0)),
                  pl.BlockSpec((D_MODEL, C_END), lambda i: (0, 0)),
                  pl.BlockSpec((A_WIDTH, D_MODEL), lambda i: (0, 0))]
        + [pl.BlockSpec(memory_space=pl.ANY)] * len(kv_prev),
        out_specs=out_specs,
        out_shape=out_shape,
        input_output_aliases={3 + n: pos for n, pos in enumerate(shared[:len(kv_prev)])},
        compiler_params=_params(("parallel",)),
    )(x, w, wvt, *kv_prev)


def _attn_body(qi_tab, kj_tab, diag_tab, last_tab, q_ref, k_ref, vt_ref, lam_ref, subg_ref, o_ref,
               m, a, *, tq, tk, lam_init):
    p = pl.program_id(2)
    qi = qi_tab[p]
    kj = kj_tab[p]

    @pl.when(kj == 0)
    def _():
        m[...] = jnp.full(m.shape, NEG_BIG, f32)
        a[...] = jnp.zeros(a.shape, f32)

    q = q_ref[...]
    k = k_ref[...]
    vt = jnp.concatenate([vt_ref[...], jnp.ones((ONES_ROWS, tk), bf16)], axis=0)
    lane = lax.broadcasted_iota(i32, (1, LANES), 1)
    zero = jnp.zeros_like(q)
    qq = jnp.concatenate([jnp.where(lane < A_HEAD_DIM, q, zero), jnp.where(lane >= A_HEAD_DIM, q, zero)],
                         axis=0)

    def step(masked):
        s = lax.dot_general(k, qq, (((1,), (1,)), ((), ())), preferred_element_type=f32)
        if masked:
            kpos = kj * tk + lax.broadcasted_iota(i32, (tk, 2 * tq), 0)
            qpos = qi * tq + (lax.broadcasted_iota(i32, (tk, 2 * tq), 1) & (tq - 1))
            s = jnp.where(kpos <= qpos, s, NEG_BIG)
        m_prev = m[...]
        m_new = jnp.maximum(m_prev, jnp.max(s, axis=0, keepdims=True))
        alpha = jnp.exp2(m_prev - m_new)
        pr = jnp.exp2(s - m_new).astype(bf16)
        a[...] = alpha * a[...] + jnp.dot(vt, pr, preferred_element_type=f32)
        m[...] = m_new

    @pl.when(diag_tab[p] == 1)
    def _():
        step(True)

    @pl.when(diag_tab[p] == 0)
    def _():
        step(False)

    @pl.when(last_tab[p] == 1)
    def _():
        lam = _lam_value(lam_ref, lam_init)
        ot = (a[:A_VDIM, :tq] / a[A_VDIM:A_VDIM + 1, :tq]
              - lam * (a[:A_VDIM, tq:] / a[A_VDIM:A_VDIM + 1, tq:]))
        ms = jnp.mean(ot * ot, axis=0, keepdims=True)
        on = ot * lax.rsqrt(ms + RMS_EPS) * subg_ref[...] * (1.0 - lam_init)
        o_ref[...] = on.T.astype(o_ref.dtype)


def _attn_prompt(q, k, vt, lam_vecs, sub_g_col, batch, seq, lam_init):
    tq = min(ATTN_TQ, seq)
    tk = min(ATTN_TK, seq)
    nq, nk = seq // tq, seq // tk
    qi_l, kj_l, dg_l, ls_l = [], [], [], []
    for qi in range(nq):
        last = ((qi + 1) * tq - 1) // tk
        for kj in range(last + 1):
            qi_l.append(qi)
            kj_l.append(kj)
            dg_l.append(1 if (kj + 1) * tk - 1 > qi * tq else 0)
            ls_l.append(1 if kj == last else 0)
    tabs = [jnp.asarray(t, i32) for t in (qi_l, kj_l, dg_l, ls_l)]
    n_pairs = len(qi_l)
    body = functools.partial(_attn_body, tq=tq, tk=tk, lam_init=lam_init)
    grid_spec = pltpu.PrefetchScalarGridSpec(
        num_scalar_prefetch=4,
        grid=(batch, A_HEADS, n_pairs),
        in_specs=[
            pl.BlockSpec((tq, LANES), lambda b, h, p, qt, kt, dt, lt: (b * nq + qt[p], h)),
            pl.BlockSpec((tk, LANES), lambda b, h, p, qt, kt, dt, lt: (b * nk + kt[p], h)),
            pl.BlockSpec((A_VDIM, tk), lambda b, h, p, qt, kt, dt, lt: (h, b * nk + kt[p])),
            pl.BlockSpec((4, A_HEAD_DIM), lambda b, h, p, *_: (0, 0)),
            pl.BlockSpec((A_VDIM, 1), lambda b, h, p, *_: (0, 0)),
        ],
        out_specs=pl.BlockSpec((tq, LANES), lambda b, h, p, qt, kt, dt, lt: (b * nq + qt[p], h)),
        scratch_shapes=[pltpu.VMEM((1, 2 * tq), f32), pltpu.VMEM((A_VDIM + ONES_ROWS, 2 * tq), f32)],
    )
    return pl.pallas_call(
        body,
        grid_spec=grid_spec,
        out_shape=jax.ShapeDtypeStruct((batch * seq, A_WIDTH), bf16),
        compiler_params=_params(("parallel", "parallel", "arbitrary")),
    )(*tabs, q, k, vt, lam_vecs, sub_g_col)


def _decode_body(pt_ref, q_ref, kn_ref, vn_ref, lam_ref, subg_ref, *refs, n_pages, lam_init):
    k_refs = refs[:n_pages]
    v_refs = refs[n_pages:2 * n_pages]
    o_ref = refs[2 * n_pages]
    kbuf, vbuf = refs[2 * n_pages + 1:]
    rows_pg = PAGE_SIZE * A_HEADS
    past = n_pages * rows_pg
    tail = 16
    n_col = past + tail

    for p in range(n_pages):
        kbuf[p * rows_pg:(p + 1) * rows_pg, :] = k_refs[p][...].astype(bf16)
        vbuf[p * rows_pg:(p + 1) * rows_pg, :] = v_refs[p][...].astype(bf16)
    kbuf[past:, :] = kn_ref[0].astype(bf16)
    vbuf[past:, :] = vn_ref[0].astype(bf16)

    qrow = q_ref[0].astype(f32)
    row = lax.broadcasted_iota(i32, (16, LANES), 0)
    lane = lax.broadcasted_iota(i32, (16, LANES), 1)
    qmat = jnp.zeros((16, LANES), f32)
    for h in range(A_HEADS):
        qh = jnp.broadcast_to(qrow[:, h * LANES:(h + 1) * LANES], (16, LANES))
        sel = ((row >> 1) == h) & ((lane >= A_HEAD_DIM) == ((row & 1) == 1))
        qmat = jnp.where(sel, qh, qmat)
    s = lax.dot_general(qmat.astype(bf16), kbuf[...], (((1,), (1,)), ((), ())), preferred_element_type=f32)
    srow = lax.broadcasted_iota(i32, (16, n_col), 0)
    scol = lax.broadcasted_iota(i32, (16, n_col), 1)
    valid = ((scol & (A_HEADS - 1)) == (srow >> 1)) & (scol < past + A_HEADS) & (srow < 2 * A_HEADS)
    s = jnp.where(valid, s, NEG_BIG)
    m = jnp.max(s, axis=-1, keepdims=True)
    pr = jnp.where(valid, jnp.exp2(s - m), 0.0)
    den = jnp.maximum(jnp.sum(pr, axis=-1, keepdims=True), 1e-30)
    pn = pr / den
    o8 = jnp.dot(pn.astype(bf16), vbuf[...], preferred_element_type=f32)
    lam = _lam_value(lam_ref, lam_init)
    for h in range(A_HEADS):
        o = o8[2 * h:2 * h + 1, :] - lam * o8[2 * h + 1:2 * h + 2, :]
        o_ref[0, :, h * LANES:(h + 1) * LANES] = (
            _rms_norm(o, subg_ref[...]) * (1.0 - lam_init)).astype(o_ref.dtype)


def _attn_decode(q_s, k_new, v_new, lam_vecs, sub_g, cache_k4, cache_v4, page_table, layer, lam_init):
    bd, n_pages = page_table.shape
    rows_pg = PAGE_SIZE * A_HEADS
    body = functools.partial(_decode_body, n_pages=n_pages, lam_init=lam_init)

    def page_spec(p):
        return pl.BlockSpec((None, None, rows_pg, LANES),
                            lambda b, pt, p=p: (layer, pt[b * n_pages + p], 0, 0))

    grid_spec = pltpu.PrefetchScalarGridSpec(
        num_scalar_prefetch=1,
        grid=(bd,),
        in_specs=[pl.BlockSpec((1, 1, A_WIDTH), lambda b, pt: (b, 0, 0)),
                  pl.BlockSpec((1, 16, LANES), lambda b, pt: (b, 0, 0)),
                  pl.BlockSpec((1, 16, LANES), lambda b, pt: (b, 0, 0)),
                  pl.BlockSpec((4, A_HEAD_DIM), lambda b, pt: (0, 0)),
                  pl.BlockSpec((1, A_VDIM), lambda b, pt: (0, 0))]
        + [page_spec(p) for p in range(n_pages)] * 2,
        out_specs=pl.BlockSpec((1, 1, A_WIDTH), lambda b, pt: (b, 0, 0)),
        scratch_shapes=[pltpu.VMEM((n_pages * rows_pg + 16, LANES), bf16),
                        pltpu.VMEM((n_pages * rows_pg + 16, LANES), bf16)],
    )
    return pl.pallas_call(
        body,
        grid_spec=grid_spec,
        out_shape=jax.ShapeDtypeStruct((bd, 1, A_WIDTH), bf16),
        compiler_params=_params(("arbitrary",)),
    )(page_table.reshape(-1), q_s, k_new, v_new, lam_vecs, sub_g,
      *([cache_k4] * n_pages), *([cache_v4] * n_pages))


def _log_decay(glr, wa2, ba):
    z = jnp.dot(glr, wa2, preferred_element_type=f32, precision=lax.Precision.HIGHEST) + ba
    return (jnp.minimum(z, 0.0) - jnp.log(1.0 + jnp.exp(-jnp.abs(z)))) * (1.0 / G_TAU)


def _gla_finish(o, gr, g):
    return _rms_norm(o, g) * (gr * _sigmoid(gr))


def _gla_prompt_body(q_ref, k_ref, v_ref, glr_ref, gr_ref, wa2_ref, ba_ref, g_ref, o_ref, s_ref, state, tmp,
                     *, c):
    ci = pl.program_id(1)
    nsub = c // GLA_SUB

    @pl.when(ci == 0)
    def _():
        state[...] = jnp.zeros(state.shape, f32)

    la = _log_decay(glr_ref[...], wa2_ref[...], ba_ref[...])
    ri = lax.broadcasted_iota(i32, (c, c), 0)
    cj = lax.broadcasted_iota(i32, (c, c), 1)
    b = jnp.dot((ri >= cj).astype(f32), la, preferred_element_type=f32, precision=lax.Precision.HIGHEST)
    b_last = b[c - 1:c, :]
    q = q_ref[...]
    k = k_ref[...]
    v = v_ref[...]
    vb = v.astype(bf16)
    lane = lax.broadcasted_iota(i32, (1, G_K_WIDTH), 1)
    heads = [(lane >> 6) == h for h in range(G_HEADS)]

    def stack_heads(x):
        return jnp.concatenate([jnp.where(m, x, 0.0) for m in heads], axis=0).astype(bf16)

    sub_i = lax.broadcasted_iota(i32, (GLA_SUB, G_K_WIDTH), 0)
    for blk in range(nsub):
        r0 = blk * GLA_SUB
        q_b = q[r0:r0 + GLA_SUB]
        b_b = b[r0:r0 + GLA_SUB]
        for j in range(GLA_SUB):
            w = jnp.where(sub_i >= j, jnp.exp(jnp.minimum(b_b - b_b[j:j + 1], 0.0)), 0.0)
            t0 = (r0 + j) * GLA_SUB
            tmp[t0:t0 + GLA_SUB, :] = (q_b * w * k[r0 + j:r0 + j + 1]).astype(bf16)
    si = lax.broadcasted_iota(i32, (G_K_WIDTH, G_V_WIDTH), 0)
    sj = lax.broadcasted_iota(i32, (G_K_WIDTH, G_V_WIDTH), 1)
    seg = ((si >> 6) == (sj >> 7)).astype(bf16)
    pair = jnp.dot(tmp[...], seg, preferred_element_type=f32)
    o_rows = []
    for blk in range(nsub):
        r0 = blk * GLA_SUB
        acc = jnp.zeros((GLA_SUB, G_V_WIDTH), f32)
        for j in range(GLA_SUB):
            t0 = (r0 + j) * GLA_SUB
            acc = acc + pair[t0:t0 + GLA_SUB, :] * v[r0 + j:r0 + j + 1]
        o_rows.append(acc)
    o_diag = jnp.concatenate(o_rows, axis=0)

    att_rows = [[jnp.zeros((GLA_SUB, c), f32)] for _ in range(G_HEADS)]
    col = lax.broadcasted_iota(i32, (G_HEADS * GLA_SUB, c), 1)
    for blk in range(1, nsub):
        r0 = blk * GLA_SUB
        ref = b[r0 - 1:r0]
        q_b = q[r0:r0 + GLA_SUB] * jnp.exp(b[r0:r0 + GLA_SUB] - ref)
        k_b = (k * jnp.exp(jnp.minimum(ref - b, 0.0))).astype(bf16)
        a = lax.dot_general(stack_heads(q_b), k_b, (((1,), (1,)), ((), ())), preferred_element_type=f32)
        a = jnp.where(col < r0, a, 0.0)
        for h in range(G_HEADS):
            att_rows[h].append(a[h * GLA_SUB:(h + 1) * GLA_SUB])

    st = state[...]
    o_inter = jnp.dot(stack_heads(q * jnp.exp(b)), st.astype(bf16), preferred_element_type=f32)
    kdt = (k * jnp.exp(b_last - b)).T.astype(bf16)
    upd = jnp.dot(kdt, vb, preferred_element_type=f32)
    dec_col = jnp.exp(jnp.broadcast_to(b_last, (8, G_K_WIDTH)).T[:, 0:1])
    for h in range(G_HEADS):
        cols = slice(h * G_DV, (h + 1) * G_DV)
        att = jnp.concatenate(att_rows[h], axis=0).astype(bf16)
        o = (o_diag[:, cols] + jnp.dot(att, vb[:, cols], preferred_element_type=f32)
             + o_inter[h * c:(h + 1) * c])
        o_ref[:, cols] = _gla_finish(o, gr_ref[:, cols], g_ref[...]).astype(o_ref.dtype)
        r0 = h * G_DK
        state[r0:r0 + G_DK, :] = dec_col[r0:r0 + G_DK] * st[r0:r0 + G_DK, :] + upd[r0:r0 + G_DK, cols]

    @pl.when(ci == pl.num_programs(1) - 1)
    def _():
        s_ref[0] = state[...]


def _gla_prompt(gq, gk, gv, glr, gr, wa2, ba, gla_g, batch, seq):
    c = math.gcd(seq, GLA_CHUNK)
    n = seq // c
    body = functools.partial(_gla_prompt_body, c=c)

    def tok(width):
        return pl.BlockSpec((c, width), lambda b, i: (b * n + i, 0))

    def whole(shape):
        return pl.BlockSpec(shape, lambda b, i: (0,) * len(shape))

    return pl.pallas_call(
        body,
        grid=(batch, n),
        in_specs=[tok(G_K_WIDTH), tok(G_K_WIDTH), tok(G_V_WIDTH), tok(LANES), tok(G_V_WIDTH),
                  whole((LANES, G_K_WIDTH)), whole((1, G_K_WIDTH)), whole((1, G_DV))],
        out_specs=[tok(G_V_WIDTH),
                   pl.BlockSpec((1, G_K_WIDTH, G_DV), lambda b, i: (b, 0, 0))],
        out_shape=[jax.ShapeDtypeStruct((batch * seq, G_V_WIDTH), bf16),
                   jax.ShapeDtypeStruct((batch, G_K_WIDTH, G_DV), f32)],
        scratch_shapes=[pltpu.VMEM((G_K_WIDTH, G_DV), f32),
                        pltpu.VMEM((c * GLA_SUB, G_K_WIDTH), bf16)],
        compiler_params=_params(("parallel", "arbitrary")),
    )(gq, gk, gv, glr, gr, wa2, ba, gla_g)


def _gla_sample_body(q_ref, k_ref, v_ref, glr_ref, gr_ref, s_ref, wa2_ref, ba_ref, g_ref, o_ref, sn_ref):
    la = _log_decay(glr_ref[0], wa2_ref[...], ba_ref[...])
    rows = jnp.concatenate([jnp.broadcast_to(jnp.exp(la), (G_DK, G_K_WIDTH)),
                            jnp.broadcast_to(k_ref[0], (G_DK, G_K_WIDTH)),
                            jnp.broadcast_to(q_ref[0], (G_DK, G_K_WIDTH))], axis=0)
    ri = lax.broadcasted_iota(i32, (3 * G_DK, G_K_WIDTH), 0)
    li = lax.broadcasted_iota(i32, (3 * G_DK, G_K_WIDTH), 1)
    picked = jnp.where((ri & (G_DK - 1)) == (li & (G_DK - 1)), rows, 0.0)
    si = lax.broadcasted_iota(i32, (G_K_WIDTH, G_V_WIDTH), 0)
    sj = lax.broadcasted_iota(i32, (G_K_WIDTH, G_V_WIDTH), 1)
    seg = ((si >> 6) == (sj >> 7)).astype(f32)
    cols = jnp.dot(picked, seg, preferred_element_type=f32, precision=lax.Precision.HIGHEST)
    v = v_ref[0]
    for h in range(G_HEADS):
        sl = slice(h * G_DV, (h + 1) * G_DV)
        a_c = cols[0:G_DK, sl]
        k_c = cols[G_DK:2 * G_DK, sl]
        q_c = cols[2 * G_DK:3 * G_DK, sl]
        s_new = a_c * s_ref[0, h] + k_c * v[:, sl]
        sn_ref[0, h] = s_new
        o = jnp.sum(q_c * s_new, axis=0, keepdims=True)
        o_ref[0, :, sl] = _gla_finish(o, gr_ref[0][:, sl], g_ref[...]).astype(o_ref.dtype)


def _gla_sample(gq, gk, gv, glr, gr, s0, wa2, ba, gla_g):
    bd = gq.shape[0]

    def vec(width):
        return pl.BlockSpec((1, 1, width), lambda b: (b, 0, 0))

    def whole(shape):
        return pl.BlockSpec(shape, lambda b: (0,) * len(shape))

    st = pl.BlockSpec((1, G_HEADS, G_DK, G_DV), lambda b: (b, 0, 0, 0))
    return pl.pallas_call(
        _gla_sample_body,
        grid=(bd,),
        in_specs=[vec(G_K_WIDTH), vec(G_K_WIDTH), vec(G_V_WIDTH), vec(LANES), vec(G_V_WIDTH), st,
                  whole((LANES, G_K_WIDTH)), whole((1, G_K_WIDTH)), whole((1, G_DV))],
        out_specs=[vec(G_V_WIDTH), st],
        out_shape=[jax.ShapeDtypeStruct((bd, 1, G_V_WIDTH), bf16),
                   jax.ShapeDtypeStruct((bd, G_HEADS, G_DK, G_DV), f32)],
        compiler_params=_params(("parallel",)),
    )(gq, gk, gv, glr, gr, s0, wa2, ba, gla_g)


def _postmix_body(oa_ref, og_ref, za_ref, zb_ref, x_ref, wpa_ref, wpb_ref, wout_ref, g_ref, b_ref,
                  x1_o, xp_o, *, alpha):
    ya = jnp.dot(oa_ref[...], wpa_ref[...], preferred_element_type=f32)
    yb = jnp.dot(og_ref[...], wpb_ref[...], preferred_element_type=f32)
    merged = _sigmoid(za_ref[...].astype(f32)) * ya + _sigmoid(zb_ref[...].astype(f32)) * yb
    mix = jnp.dot(merged.astype(bf16), wout_ref[...], preferred_element_type=f32)
    x1 = _layer_norm(alpha * x_ref[...] + mix, g_ref[...], b_ref[...])
    x1_o[...] = x1
    xp_o[...] = _pack_rows(x1)


def _postmix(oa, og, za, zb, x, wpa, wpb, wout, g, b, alpha):
    t = x.shape[0]
    tm = _row_tile(t, (384, 256, 128, 64, 32, 16, 8))

    def tok(width):
        return pl.BlockSpec((tm, width), lambda i: (i, 0))

    def whole(shape):
        return pl.BlockSpec(shape, lambda i: (0,) * len(shape))

    return pl.pallas_call(
        functools.partial(_postmix_body, alpha=alpha),
        grid=(t // tm,),
        in_specs=[tok(A_WIDTH), tok(G_V_WIDTH), tok(D_MODEL), tok(D_MODEL), tok(D_MODEL),
                  whole((A_WIDTH, D_MODEL)), whole((G_V_WIDTH, D_MODEL)), whole((D_MODEL, D_MODEL)),
                  whole((1, D_MODEL)), whole((1, D_MODEL))],
        out_specs=[tok(D_MODEL), tok(HALF)],
        out_shape=[jax.ShapeDtypeStruct((t, D_MODEL), f32),
                   jax.ShapeDtypeStruct((t, HALF), u32)],
        compiler_params=_params(("parallel",)),
    )(oa, og, za, zb, x, wpa, wpb, wout, g, b)


def _router_body(x1_ref, wrt_ref, bcol_ref, tri_ref, eidx_o, gate_o, rank_o, cnt_o, cnt, *, tm):
    @pl.when(pl.program_id(0) == 0)
    def _():
        cnt[...] = jnp.zeros(cnt.shape, f32)

    logits = lax.dot_general(wrt_ref[...], x1_ref[...], (((1,), (1,)), ((), ())),
                             preferred_element_type=f32, precision=lax.Precision.HIGHEST)
    scores = _sigmoid(logits)
    biased = scores + bcol_ref[...]
    gsz = N_EXPERTS // N_GROUPS
    neg_inf = -jnp.inf

    gi = lax.broadcasted_iota(i32, (gsz, tm), 0)
    segs, gscore = [], []
    for g in range(N_GROUPS):
        seg = biased[g * gsz:(g + 1) * gsz, :]
        m1 = jnp.max(seg, axis=0, keepdims=True)
        i1 = jnp.min(jnp.where(seg == m1, gi, gsz), axis=0, keepdims=True)
        m2 = jnp.max(jnp.where(gi == i1, neg_inf, seg), axis=0, keepdims=True)
        segs.append(seg)
        gscore.append(m1 + m2)
    parts = []
    for g in range(N_GROUPS):
        beat = jnp.zeros((1, tm), i32)
        for o in range(N_GROUPS):
            if o != g:
                wins = (gscore[o] > gscore[g]) | ((gscore[o] == gscore[g]) & (o < g))
                beat = beat + wins.astype(i32)
        parts.append(jnp.where(beat < TOPK_GROUPS, segs[g], neg_inf))
    masked = jnp.concatenate(parts, axis=0)

    ei = lax.broadcasted_iota(i32, (N_EXPERTS, tm), 0)
    sel_rows, idx_rows = [], []
    chosen = jnp.zeros((N_EXPERTS, tm), f32)
    for _ in range(TOP_K):
        m = jnp.max(masked, axis=0, keepdims=True)
        idx = jnp.min(jnp.where(masked == m, ei, N_EXPERTS), axis=0, keepdims=True)
        hit = ei == idx
        sel_rows.append(jnp.sum(jnp.where(hit, scores, 0.0), axis=0, keepdims=True))
        idx_rows.append(idx)
        chosen = jnp.where(hit, 1.0, chosen)
        masked = jnp.where(hit, neg_inf, masked)
    s_sel = jnp.concatenate(sel_rows, axis=0)
    gate_o[...] = s_sel / jnp.sum(s_sel, axis=0, keepdims=True) * ROUTED_SCALE
    eidx_o[...] = jnp.concatenate(idx_rows, axis=0)
    before = jnp.dot(chosen.astype(bf16), tri_ref[...], preferred_element_type=f32) + cnt[...]
    rank_rows = [jnp.sum(jnp.where(ei == idx, before, 0.0), axis=0, keepdims=True) for idx in idx_rows]
    rank_o[...] = jnp.concatenate(rank_rows, axis=0).astype(i32)
    cnt[...] = cnt[...] + jnp.sum(chosen, axis=1, keepdims=True)
    cnt_o[...] = cnt[...]


def _router(x1, wrt, b_col):
    t = x1.shape[0]
    tm = _row_tile(t, (384, 256, 128))
    tri = (jnp.arange(tm)[:, None] < jnp.arange(tm)[None, :]).astype(bf16)

    def tokcol(dt):
        return pl.BlockSpec((TOP_K, tm), lambda i: (0, i)), jax.ShapeDtypeStruct((TOP_K, t), dt)

    def whole(shape):
        return pl.BlockSpec(shape, lambda i: (0,) * len(shape))

    specs, shapes = zip(tokcol(i32), tokcol(f32), tokcol(i32),
                        (whole((N_EXPERTS, 1)), jax.ShapeDtypeStruct((N_EXPERTS, 1), f32)))
    return pl.pallas_call(
        functools.partial(_router_body, tm=tm),
        grid=(t // tm,),
        in_specs=[pl.BlockSpec((tm, D_MODEL), lambda i: (i, 0)), whole((N_EXPERTS, D_MODEL)),
                  whole((N_EXPERTS, 1)), whole((tm, tm))],
        out_specs=list(specs),
        out_shape=list(shapes),
        scratch_shapes=[pltpu.VMEM((N_EXPERTS, 1), f32)],
        compiler_params=_params(("arbitrary",)),
    )(x1, wrt, b_col, tri)


def _dispatch_body(dest_ref, pad0_ref, pad1_ref, x_ref, xs_out, zbuf, sem, zsem, *, tm):

    @pl.when(pl.program_id(0) == 0)
    def _():
        zbuf[...] = jnp.zeros(zbuf.shape, u32)

        def zero_row(row):
            return pltpu.make_async_copy(zbuf.at[pl.ds(0, 1)], xs_out.at[pl.ds(row, 1)], zsem)

        def zero_group(row):
            return pltpu.make_async_copy(zbuf, xs_out.at[pl.ds(pl.multiple_of(row, SUBLANES), SUBLANES)], zsem)

        def per_expert(e, waiting):
            p0 = pad0_ref[e]
            p1 = pad1_ref[e]
            head = jnp.minimum((-p0) & (SUBLANES - 1), p1 - p0)
            groups = lax.shift_right_logical(p1 - p0 - head, 3)

            def rows(r, c):
                if waiting:
                    zero_row(0).wait()
                else:
                    zero_row(p0 + r).start()
                return c

            def grps(g, c):
                if waiting:
                    zero_group(0).wait()
                else:
                    zero_group(p0 + head + g * SUBLANES).start()
                return c

            lax.fori_loop(0, head, rows, 0)
            lax.fori_loop(0, groups, grps, 0)

        lax.fori_loop(0, N_EXPERTS, lambda e, c: (per_expert(e, False), c)[1], 0)
        lax.fori_loop(0, N_EXPERTS, lambda e, c: (per_expert(e, True), c)[1], 0)

    def copy(grp, sub, dst_row):
        return pltpu.make_async_copy(x_ref.at[grp, pl.ds(sub, 1)], xs_out.at[pl.ds(dst_row, 1)], sem)

    def issue(grp, carry):
        for sub in range(SUBLANES):
            for kk in range(TOP_K):
                copy(grp, sub, dest_ref[(grp * SUBLANES + sub) * TOP_K + kk]).start(priority=kk % 2)
        return carry

    lax.fori_loop(0, tm // SUBLANES, issue, 0)

    def drain(grp, carry):
        for _ in range(SUBLANES * TOP_K):
            copy(0, 0, 0).wait()
        return carry

    lax.fori_loop(0, tm // SUBLANES, drain, 0)


def _dispatch(xp, dest, pad0, pad1, n_rows):
    t = xp.shape[0]
    tm = _row_tile(t, (384, 256, 128, 64, 32, 16, 8))
    whole = pl.BlockSpec((N_EXPERTS,), lambda i: (0,), memory_space=pltpu.SMEM)
    return pl.pallas_call(
        functools.partial(_dispatch_body, tm=tm),
        grid=(t // tm,),
        in_specs=[pl.BlockSpec((tm * TOP_K,), lambda i: (i,), memory_space=pltpu.SMEM), whole, whole,
                  pl.BlockSpec((tm // SUBLANES, SUBLANES, HALF), lambda i: (i, 0, 0))],
        out_specs=pl.BlockSpec(memory_space=pl.ANY),
        out_shape=jax.ShapeDtypeStruct((n_rows, HALF), u32),
        scratch_shapes=[pltpu.VMEM((SUBLANES, HALF), u32), pltpu.SemaphoreType.DMA(()),
                        pltpu.SemaphoreType.DMA(())],
        compiler_params=_params(("arbitrary",)),
    )(dest, pad0, pad1, xp.reshape(t // SUBLANES, SUBLANES, HALF))


def _expert_weight_copies(seq_ref, w_hbm, w_buf, sems, seq_idx, *, layer):
    e = seq_ref[seq_idx]
    slot = lax.rem(seq_idx, WEIGHT_SLOTS)
    return [pltpu.make_async_copy(w_hbm[n].at[layer, e], w_buf[n].at[slot], sems.at[slot, n]) for n in range(3)]


def _expert_body(bs_ref, seq_ref, ns_ref, nu_ref, xs_ref, wg_hbm, wu_hbm, wd_hbm, ys_ref,
                 wgf, wuf, wdf, sems, wgb, wub, wdb, *, layer):
    i = pl.program_id(0)
    j = bs_ref[i]
    copies = functools.partial(_expert_weight_copies, seq_ref, (wg_hbm, wu_hbm, wd_hbm), (wgf, wuf, wdf),
                               sems, layer=layer)

    @pl.when(i == 0)
    def _():
        for c in copies(0):
            c.start()

        @pl.when(ns_ref[0] > 1)
        def _():
            for c in copies(1):
                c.start()

    @pl.when((i == 0) | (j != bs_ref[jnp.maximum(i - 1, 0)]))
    def _():
        for c in copies(j):
            c.wait()
        slot = lax.rem(j, WEIGHT_SLOTS)
        wgb[...] = wgf[slot].astype(bf16)
        wub[...] = wuf[slot].astype(bf16)
        wdb[...] = wdf[slot].astype(bf16)

        @pl.when(j + 2 < ns_ref[0])
        def _():
            for c in copies(j + 2):
                c.start()

    @pl.when(i < nu_ref[0])
    def _():
        lo, hi = _unpack_rows(xs_ref[...])
        lo = lo.astype(bf16)
        hi = hi.astype(bf16)
        g = (jnp.dot(lo, wgb[:HALF, :], preferred_element_type=f32)
             + jnp.dot(hi, wgb[HALF:, :], preferred_element_type=f32))
        u = (jnp.dot(lo, wub[:HALF, :], preferred_element_type=f32)
             + jnp.dot(hi, wub[HALF:, :], preferred_element_type=f32))
        hdn = (g * _sigmoid(g) * u).astype(bf16)
        ys_ref[...] = _pack_rows(jnp.dot(hdn, wdb[...], preferred_element_type=f32))

    @pl.when(i >= nu_ref[0])
    def _():
        ys_ref[...] = jnp.zeros(ys_ref.shape, u32)


def _experts(xs, blk_seq, seq_exp, n_seq, n_used, w_gate, w_up, w_down, layer):
    n_rows = xs.shape[0]
    n_blocks = n_rows // EXPERT_BLOCK
    hbm = pl.BlockSpec(memory_space=pl.ANY)
    grid_spec = pltpu.PrefetchScalarGridSpec(
        num_scalar_prefetch=4,
        grid=(n_blocks,),
        in_specs=[pl.BlockSpec((EXPERT_BLOCK, HALF), lambda i, bs, sq, ns, nu: (jnp.minimum(i, nu[0] - 1), 0)),
                  hbm, hbm, hbm],
        out_specs=pl.BlockSpec((EXPERT_BLOCK, HALF), lambda i, bs, sq, ns, nu: (i, 0)),
        scratch_shapes=[pltpu.VMEM((WEIGHT_SLOTS, D_MODEL, D_EXPERT), f32),
                        pltpu.VMEM((WEIGHT_SLOTS, D_MODEL, D_EXPERT), f32),
                        pltpu.VMEM((WEIGHT_SLOTS, D_EXPERT, D_MODEL), f32),
                        pltpu.SemaphoreType.DMA((WEIGHT_SLOTS, 3)),
                        pltpu.VMEM((D_MODEL, D_EXPERT), bf16), pltpu.VMEM((D_MODEL, D_EXPERT), bf16),
                        pltpu.VMEM((D_EXPERT, D_MODEL), bf16)],
    )
    return pl.pallas_call(
        functools.partial(_expert_body, layer=layer),
        grid_spec=grid_spec,
        out_shape=jax.ShapeDtypeStruct((n_rows, HALF), u32),
        compiler_params=_params(("arbitrary",)),
    )(blk_seq, seq_exp, n_seq, n_used, xs, w_gate, w_up, w_down)


def _combine_body(dest_ref, gate_ref, x1_ref, xp_ref, ys_hbm, wsg_ref, wsu_ref, wsd_ref, g_ref, b_ref,
                  o_ref, buf, sem, *, tm, alpha):
    def copy(src_row, kk, grp, sub):
        return pltpu.make_async_copy(ys_hbm.at[pl.ds(src_row, 1)], buf.at[kk, grp, pl.ds(sub, 1)], sem)

    def issue(grp, carry):
        for sub in range(SUBLANES):
            for kk in range(TOP_K):
                copy(dest_ref[(grp * SUBLANES + sub) * TOP_K + kk], kk, grp, sub).start(priority=kk % 2)
        return carry

    lax.fori_loop(0, tm // SUBLANES, issue, 0)

    lo, hi = _unpack_rows(xp_ref[...])
    lo = lo.astype(bf16)
    hi = hi.astype(bf16)
    sg = (jnp.dot(lo, wsg_ref[:HALF, :], preferred_element_type=f32)
          + jnp.dot(hi, wsg_ref[HALF:, :], preferred_element_type=f32))
    su = (jnp.dot(lo, wsu_ref[:HALF, :], preferred_element_type=f32)
          + jnp.dot(hi, wsu_ref[HALF:, :], preferred_element_type=f32))
    shared = jnp.dot((sg * _sigmoid(sg) * su).astype(bf16), wsd_ref[...], preferred_element_type=f32)

    def drain(grp, carry):
        for _ in range(SUBLANES * TOP_K):
            copy(0, 0, 0, 0).wait()
        return carry

    lax.fori_loop(0, tm // SUBLANES, drain, 0)

    gates = gate_ref[...]
    acc_lo = jnp.zeros((tm, HALF), f32)
    acc_hi = jnp.zeros((tm, HALF), f32)
    for kk in range(TOP_K):
        ylo, yhi = _unpack_rows(buf[kk].reshape(tm, HALF))
        gk = gates[:, kk:kk + 1]
        acc_lo = acc_lo + gk * ylo
        acc_hi = acc_hi + gk * yhi
    moe = jnp.concatenate([acc_lo, acc_hi], axis=-1) + shared
    o_ref[...] = _layer_norm(alpha * x1_ref[...] + moe, g_ref[...], b_ref[...])


def _combine(dest, gates, x1, xp, ys, wsg, wsu, wsd, g, b, alpha):
    t = x1.shape[0]
    tm = _row_tile(t, (384, 256, 128, 64, 32, 16, 8))

    def tok(width):
        return pl.BlockSpec((tm, width), lambda i: (i, 0))

    def whole(shape):
        return pl.BlockSpec(shape, lambda i: (0,) * len(shape))

    return pl.pallas_call(
        functools.partial(_combine_body, tm=tm, alpha=alpha),
        grid=(t // tm,),
        in_specs=[pl.BlockSpec((tm * TOP_K,), lambda i: (i,), memory_space=pltpu.SMEM),
                  tok(TOP_K), tok(D_MODEL), tok(HALF),
                  pl.BlockSpec(memory_space=pl.ANY),
                  whole((D_MODEL, D_EXPERT)), whole((D_MODEL, D_EXPERT)), whole((D_EXPERT, D_MODEL)),
                  whole((1, D_MODEL)), whole((1, D_MODEL))],
        out_specs=tok(D_MODEL),
        out_shape=jax.ShapeDtypeStruct((t, D_MODEL), f32),
        scratch_shapes=[pltpu.VMEM((TOP_K, tm // SUBLANES, SUBLANES, HALF), u32),
                        pltpu.SemaphoreType.DMA(())],
        compiler_params=_params(("arbitrary",)),
    )(dest, gates, x1, xp, ys, wsg, wsu, wsd, g, b)


def _dest_body(eidx_ref, rank_ref, ps_ref, dest_o, *, tm):
    ei = lax.broadcasted_iota(i32, (N_EXPERTS, tm), 0)
    ps = ps_ref[...]
    rows = [jnp.sum(jnp.where(ei == eidx_ref[kk:kk + 1, :], ps, 0.0), axis=0, keepdims=True)
            for kk in range(TOP_K)]
    dest_o[...] = jnp.concatenate(rows, axis=0).astype(i32) + rank_ref[...]


def _dest_rows(eidx_t, rank_t, pad_start):
    t = eidx_t.shape[1]
    tm = _row_tile(t, (384, 256, 128))
    blk = pl.BlockSpec((TOP_K, tm), lambda i: (0, i))
    return pl.pallas_call(
        functools.partial(_dest_body, tm=tm),
        grid=(t // tm,),
        in_specs=[blk, blk, pl.BlockSpec((N_EXPERTS, 1), lambda i: (0, 0))],
        out_specs=blk,
        out_shape=jax.ShapeDtypeStruct((TOP_K, t), i32),
        compiler_params=_params(("parallel",)),
    )(eidx_t, rank_t, pad_start.astype(f32).reshape(N_EXPERTS, 1))


def _layout(eidx_t, rank_t, counts):
    t = eidx_t.shape[1]
    blk = EXPERT_BLOCK
    padded = (counts + blk - 1) // blk * blk
    pad_end = jnp.cumsum(padded)
    pad_start = pad_end - padded
    dest = _dest_rows(eidx_t, rank_t, pad_start)
    n_rows = -(-(t * TOP_K + N_EXPERTS * (blk - 1)) // blk) * blk
    n_blocks = n_rows // blk
    n_used = (pad_end[-1] // blk).astype(i32)
    first_row = jnp.minimum(jnp.arange(n_blocks), n_used - 1) * blk
    blk_exp = jnp.sum(pad_end[None, :] <= first_row[:, None], axis=1).astype(i32)
    blk_exp = jnp.minimum(blk_exp, N_EXPERTS - 1)
    used = counts > 0
    seq_of = jnp.cumsum(used.astype(i32)) - 1
    ids = jnp.arange(N_EXPERTS, dtype=i32)
    seq_exp = jnp.sum(jnp.where(used[None, :] & (seq_of[None, :] == ids[:, None]), ids[None, :], 0), axis=1)
    blk_seq = jnp.sum(jnp.where(blk_exp[:, None] == ids[None, :], seq_of[None, :], 0), axis=1)
    n_seq = jnp.sum(used.astype(i32))
    pads = ((pad_start + counts).astype(i32), pad_end.astype(i32))
    return dest, pads, blk_seq.astype(i32), seq_exp.astype(i32), n_seq.reshape(1), n_used.reshape(1), n_rows


def kernel(x_prompt, x_sample, cache_k, cache_v, state_gla, page_table, w_in, w_a2, b_a, lam_q1, lam_k1,
           lam_q2, lam_k2, sub_g, gla_g, w_pa, w_pb, w_out, ln1_g, ln1_b, w_router, b_router, w_gate, w_up,
           w_down, ws_gate, ws_up, ws_down, ln2_g, ln2_b):
    depth = w_in.shape[0]
    batch, seq, _ = x_prompt.shape
    bd = x_sample.shape[0]
    tp = batch * seq
    alpha = (2 * depth) ** 0.25
    n_pool = cache_k.shape[1]
    cache_k4 = cache_k.reshape(depth, n_pool, PAGE_SIZE * A_HEADS, 2 * A_HEAD_DIM)
    cache_v4 = cache_v.reshape(depth, n_pool, PAGE_SIZE * A_HEADS, A_VDIM)

    x = jnp.concatenate([x_prompt.reshape(tp, D_MODEL), x_sample.reshape(bd, D_MODEL)], axis=0)
    t_all = tp + bd
    kv_all = ()
    sp_l, ss_l = [], []
    for l in range(depth):
        lam_init = 0.8 - 0.6 * math.exp(-0.3 * l)
        wl = w_in[l]
        w_re = jnp.concatenate([wl[:, :3072], wl[:, 3088:5136], wl[:, 3072:3088],
                                jnp.zeros((D_MODEL, LANES - G_GATE_RANK), f32)], axis=1).astype(bf16)
        wa2 = jnp.concatenate([w_a2[l], jnp.zeros((LANES - G_GATE_RANK, G_K_WIDTH), f32)], axis=0)
        ba = b_a[l].reshape(1, G_K_WIDTH)
        lam_vecs = jnp.stack([lam_q1[l], lam_k1[l], lam_q2[l], lam_k2[l]]).astype(f32)
        subg = sub_g[l].reshape(1, A_VDIM)
        glag = gla_g[l].reshape(1, G_DV)

        wvt = wl[:, C_V:C_GQ].T.astype(bf16)
        q, kf_all, kb, vf_all, vt, gq, gk, gv, gr, za, zb, glr = _inproj(x, w_re, wvt, l, depth, kv_all)
        kv_all = (kf_all, vf_all)
        kf = kf_all[l * t_all + tp:(l + 1) * t_all]
        vf = vf_all[l * t_all + tp:(l + 1) * t_all]

        oa_p = _attn_prompt(q, kb, vt, lam_vecs, sub_g[l].reshape(A_VDIM, 1), batch, seq, lam_init)
        tail_pad = ((0, 0), (0, 16 - A_HEADS), (0, 0))
        oa_s = _attn_decode(q[tp:].reshape(bd, 1, A_WIDTH),
                            jnp.pad(kf.reshape(bd, A_HEADS, 2 * A_HEAD_DIM), tail_pad),
                            jnp.pad(vf.reshape(bd, A_HEADS, A_VDIM), tail_pad),
                            lam_vecs, subg, cache_k4, cache_v4, page_table, l, lam_init)
        og_p, s_p = _gla_prompt(gq, gk, gv, glr, gr, wa2, ba, glag, batch, seq)
        og_s, s_s = _gla_sample(gq[tp:].reshape(bd, 1, -1), gk[tp:].reshape(bd, 1, -1),
                                gv[tp:].reshape(bd, 1, -1), glr[tp:].reshape(bd, 1, -1),
                                gr[tp:].reshape(bd, 1, -1), state_gla[l], wa2, ba, glag)
        oa = jnp.concatenate([oa_p, oa_s.reshape(bd, A_WIDTH)], axis=0)
        og = jnp.concatenate([og_p, og_s.reshape(bd, G_V_WIDTH)], axis=0)

        x1, xp = _postmix(oa, og, za, zb, x, w_pa[l].astype(bf16), w_pb[l].astype(bf16),
                          w_out[l].astype(bf16), ln1_g[l].reshape(1, -1), ln1_b[l].reshape(1, -1), alpha)
        eidx_t, gates_t, rank_t, counts = _router(x1, w_router[l].T, b_router[l].reshape(N_EXPERTS, 1))
        dest_t, pads, blk_seq, seq_exp, n_seq, n_used, n_rows = _layout(eidx_t, rank_t,
                                                                        counts.reshape(-1).astype(i32))
        dest = dest_t.T.reshape(-1)
        xs = _dispatch(xp, dest, pads[0], pads[1], n_rows)
        ys = _experts(xs, blk_seq, seq_exp, n_seq, n_used, w_gate, w_up, w_down, l)
        x = _combine(dest, gates_t.T, x1, xp, ys, ws_gate[l].astype(bf16), ws_up[l].astype(bf16),
                     ws_down[l].astype(bf16), ln2_g[l].reshape(1, -1), ln2_b[l].reshape(1, -1), alpha)

        sp_l.append(s_p.reshape(batch, G_HEADS, G_DK, G_DV))
        ss_l.append(s_s)

    y_prompt = x[:tp].reshape(batch, seq, D_MODEL)
    y_sample = x[tp:].reshape(bd, 1, D_MODEL)
    k3 = kv_all[0].reshape(depth, t_all, A_WIDTH)
    v3 = kv_all[1].reshape(depth, t_all, A_WIDTH)
    pages = (depth, batch, seq // PAGE_SIZE, PAGE_SIZE, A_HEADS, A_VDIM)
    return (y_prompt, y_sample, k3[:, :tp].reshape(pages), v3[:, :tp].reshape(pages),
            k3[:, tp:].reshape(depth, bd, 1, A_HEADS, 2 * A_HEAD_DIM),
            v3[:, tp:].reshape(depth, bd, 1, A_HEADS, A_VDIM), jnp.stack(sp_l), jnp.stack(ss_l))
```

```python
import functools
import math

import jax
import jax.numpy as jnp
from jax import lax
from jax.experimental import pallas as pl
from jax.experimental.pallas import tpu as pltpu

f32 = jnp.float32
bf16 = jnp.bfloat16
u32 = jnp.uint32
i32 = jnp.int32

D_MODEL = 1024
A_HEADS = 4
A_HEAD_DIM = 64
A_VDIM = 128
A_WIDTH = A_HEADS * A_VDIM
G_HEADS = 4
G_DK = 64
G_DV = 128
G_K_WIDTH = G_HEADS * G_DK
G_V_WIDTH = G_HEADS * G_DV
G_GATE_RANK = 16
G_TAU = 16.0
N_EXPERTS = 256
TOP_K = 8
N_GROUPS = 8
TOPK_GROUPS = 4
D_EXPERT = 256
ROUTED_SCALE = 2.5
PAGE_SIZE = 128
LN_EPS = 1e-5
RMS_EPS = 1e-6

LANES = 128
SUBLANES = 8
VMEM_LIMIT = 56 * 1024 * 1024

NEG_BIG = -1e30
HALF = D_MODEL // 2
EXPERT_BLOCK = 256
WEIGHT_SLOTS = 3
WEIGHT_PARTS = 4
GLA_CHUNK = 64
GLA_SUB = 16
ATTN_TQ = 1024
ATTN_TK = 1024
ONES_ROWS = 16
LOG2E = math.log2(math.e)

C_Q, C_K, C_V, C_GQ, C_GK, C_GV, C_GR, C_ZA, C_ZB, C_GLR, C_END = (
    0, 512, 1024, 1536, 1792, 2048, 2560, 3072, 4096, 5120, 5248)


def _params(sem, vmem=VMEM_LIMIT):
    return pltpu.CompilerParams(dimension_semantics=sem, vmem_limit_bytes=vmem)


def _row_tile(n, cands=(512, 384, 256, 128, 64, 32, 16, 8)):
    for c in cands:
        if n % c == 0:
            return c
    raise ValueError(f"no row tile for {n}")


def _sigmoid(x):
    return 1.0 / (1.0 + jnp.exp(-x))


def _pack_rows(x):
    lo = lax.bitcast_convert_type(x[:, :HALF].astype(bf16).astype(f32), u32) >> 16
    hi = lax.bitcast_convert_type(x[:, HALF:].astype(bf16).astype(f32), u32) & jnp.uint32(0xFFFF0000)
    return lo | hi


def _unpack_rows(w):
    lo = lax.bitcast_convert_type(w << 16, f32)
    hi = lax.bitcast_convert_type(w & jnp.uint32(0xFFFF0000), f32)
    return lo, hi


def _layer_norm(h, g, b):
    mu = jnp.mean(h, axis=-1, keepdims=True)
    d = h - mu
    var = jnp.mean(d * d, axis=-1, keepdims=True)
    return d * lax.rsqrt(var + LN_EPS) * g + b


def _rms_norm(o, g):
    return o * lax.rsqrt(jnp.mean(o * o, axis=-1, keepdims=True) + RMS_EPS) * g


def _lam_value(lam_ref, lam_init):
    l = lam_ref[...]
    s1 = jnp.sum(l[0:1] * l[1:2], axis=-1, keepdims=True)
    s2 = jnp.sum(l[2:3] * l[3:4], axis=-1, keepdims=True)
    return jnp.exp(s1) - jnp.exp(s2) + lam_init


def _inproj_body(x_ref, w_ref, wvt_ref, *refs):
    q_o, kf_o, kb_o, vf_o, vt_o, gq_o, gk_o, gv_o, gr_o, za_o, zb_o, glr_o = refs[-12:]
    xb = x_ref[...].astype(bf16)

    def mm(c0, c1):
        return jnp.dot(xb, w_ref[:, c0:c1], preferred_element_type=f32)

    q_o[...] = (mm(C_Q, C_K) * (A_HEAD_DIM ** -0.5 * LOG2E)).astype(bf16)
    k = mm(C_K, C_V)
    kf_o[...] = k
    kb_o[...] = k.astype(bf16)
    vf_o[...] = mm(C_V, C_GQ)
    vt_o[...] = lax.dot_general(wvt_ref[...], xb, (((1,), (1,)), ((), ())),
                                preferred_element_type=f32).astype(bf16)
    gq_o[...] = mm(C_GQ, C_GK) * (G_DK ** -0.5)
    gk_o[...] = mm(C_GK, C_GV)
    gv_o[...] = mm(C_GV, C_GR)
    gr_o[...] = mm(C_GR, C_ZA)
    za_o[...] = mm(C_ZA, C_ZB).astype(bf16)
    zb_o[...] = mm(C_ZB, C_GLR).astype(bf16)
    glr_o[...] = mm(C_GLR, C_END)


def _inproj(x, w, wvt, layer, depth, kv_prev):
    t = x.shape[0]
    tm = _row_tile(t, (384, 256, 128))
    nb = t // tm
    outs = [(512, bf16), (512, f32), (512, bf16), (512, f32), None, (256, f32), (256, f32),
            (512, f32), (512, f32), (1024, bf16), (1024, bf16), (LANES, f32)]
    shared = (1, 3)
    out_specs, out_shape = [], []
    for n, o in enumerate(outs):
        if o is None:
            out_specs.append(pl.BlockSpec((A_WIDTH, tm), lambda i: (0, i)))
            out_shape.append(jax.ShapeDtypeStruct((A_WIDTH, t), bf16))
        elif n in shared:
            out_specs.append(pl.BlockSpec((tm, o[0]), lambda i: (layer * nb + i, 0)))
            out_shape.append(jax.ShapeDtypeStruct((depth * t, o[0]), o[1]))
        else:
            out_specs.append(pl.BlockSpec((tm, o[0]), lambda i: (i, 0)))
            out_shape.append(jax.ShapeDtypeStruct((t, o[0]), o[1]))
    return pl.pallas_call(
        _inproj_body,
        grid=(nb,),
        in_specs=[pl.BlockSpec((tm, D_MODEL), lambda i: (i, 0)),
                  pl.BlockSpec((D_MODEL, C_END), lambda i: (0, 0)),
                  pl.BlockSpec((A_WIDTH, D_MODEL), lambda i: (0, 0))]
        + [pl.BlockSpec(memory_space=pl.ANY)] * len(kv_prev),
        out_specs=out_specs,
        out_shape=out_shape,
        input_output_aliases={3 + n: pos for n, pos in enumerate(shared[:len(kv_prev)])},
        compiler_params=_params(("parallel",)),
    )(x, w, wvt, *kv_prev)


def _attn_body(qi_tab, kj_tab, diag_tab, last_tab, q_ref, k_ref, vt_ref, lam_ref, subg_ref, o_ref,
               m, a, *, tq, tk, lam_init):
    p = pl.program_id(2)
    qi = qi_tab[p]
    kj = kj_tab[p]

    @pl.when(kj == 0)
    def _():
        m[...] = jnp.full(m.shape, NEG_BIG, f32)
        a[...] = jnp.zeros(a.shape, f32)

    q = q_ref[...]
    k = k_ref[...]
    vt = jnp.concatenate([vt_ref[...], jnp.ones((ONES_ROWS, tk), bf16)], axis=0)
    lane = lax.broadcasted_iota(i32, (1, LANES), 1)
    zero = jnp.zeros_like(q)
    qq = jnp.concatenate([jnp.where(lane < A_HEAD_DIM, q, zero), jnp.where(lane >= A_HEAD_DIM, q, zero)],
                         axis=0)

    def step(masked):
        s = lax.dot_general(k, qq, (((1,), (1,)), ((), ())), preferred_element_type=f32)
        if masked:
            kpos = kj * tk + lax.broadcasted_iota(i32, (tk, 2 * tq), 0)
            qpos = qi * tq + (lax.broadcasted_iota(i32, (tk, 2 * tq), 1) & (tq - 1))
            s = jnp.where(kpos <= qpos, s, NEG_BIG)
        m_prev = m[...]
        m_new = jnp.maximum(m_prev, jnp.max(s, axis=0, keepdims=True))
        alpha = jnp.exp2(m_prev - m_new)
        pr = jnp.exp2(s - m_new).astype(bf16)
        a[...] = alpha * a[...] + jnp.dot(vt, pr, preferred_element_type=f32)
        m[...] = m_new

    @pl.when(diag_tab[p] == 1)
    def _():
        step(True)

    @pl.when(diag_tab[p] == 0)
    def _():
        step(False)

    @pl.when(last_tab[p] == 1)
    def _():
        lam = _lam_value(lam_ref, lam_init)
        ot = (a[:A_VDIM, :tq] / a[A_VDIM:A_VDIM + 1, :tq]
              - lam * (a[:A_VDIM, tq:] / a[A_VDIM:A_VDIM + 1, tq:]))
        ms = jnp.mean(ot * ot, axis=0, keepdims=True)
        on = ot * lax.rsqrt(ms + RMS_EPS) * subg_ref[...] * (1.0 - lam_init)
        o_ref[...] = on.T.astype(o_ref.dtype)


def _attn_prompt(q, k, vt, lam_vecs, sub_g_col, batch, seq, lam_init):
    tq = min(ATTN_TQ, seq)
    tk = min(ATTN_TK, seq)
    nq, nk = seq // tq, seq // tk
    qi_l, kj_l, dg_l, ls_l = [], [], [], []
    for qi in range(nq):
        last = ((qi + 1) * tq - 1) // tk
        for kj in range(last + 1):
            qi_l.append(qi)
            kj_l.append(kj)
            dg_l.append(1 if (kj + 1) * tk - 1 > qi * tq else 0)
            ls_l.append(1 if kj == last else 0)
    tabs = [jnp.asarray(t, i32) for t in (qi_l, kj_l, dg_l, ls_l)]
    n_pairs = len(qi_l)
    body = functools.partial(_attn_body, tq=tq, tk=tk, lam_init=lam_init)
    grid_spec = pltpu.PrefetchScalarGridSpec(
        num_scalar_prefetch=4,
        grid=(batch, A_HEADS, n_pairs),
        in_specs=[
            pl.BlockSpec((tq, LANES), lambda b, h, p, qt, kt, dt, lt: (b * nq + qt[p], h)),
            pl.BlockSpec((tk, LANES), lambda b, h, p, qt, kt, dt, lt: (b * nk + kt[p], h)),
            pl.BlockSpec((A_VDIM, tk), lambda b, h, p, qt, kt, dt, lt: (h, b * nk + kt[p])),
            pl.BlockSpec((4, A_HEAD_DIM), lambda b, h, p, *_: (0, 0)),
            pl.BlockSpec((A_VDIM, 1), lambda b, h, p, *_: (0, 0)),
        ],
        out_specs=pl.BlockSpec((tq, LANES), lambda b, h, p, qt, kt, dt, lt: (b * nq + qt[p], h)),
        scratch_shapes=[pltpu.VMEM((1, 2 * tq), f32), pltpu.VMEM((A_VDIM + ONES_ROWS, 2 * tq), f32)],
    )
    return pl.pallas_call(
        body,
        grid_spec=grid_spec,
        out_shape=jax.ShapeDtypeStruct((batch * seq, A_WIDTH), bf16),
        compiler_params=_params(("parallel", "parallel", "arbitrary")),
    )(*tabs, q, k, vt, lam_vecs, sub_g_col)


def _decode_body(pt_ref, q_ref, kn_ref, vn_ref, lam_ref, subg_ref, *refs, n_pages, lam_init):
    k_refs = refs[:n_pages]
    v_refs = refs[n_pages:2 * n_pages]
    o_ref = refs[2 * n_pages]
    kbuf, vbuf = refs[2 * n_pages + 1:]
    rows_pg = PAGE_SIZE * A_HEADS
    past = n_pages * rows_pg
    tail = 16
    n_col = past + tail

    for p in range(n_pages):
        kbuf[p * rows_pg:(p + 1) * rows_pg, :] = k_refs[p][...].astype(bf16)
        vbuf[p * rows_pg:(p + 1) * rows_pg, :] = v_refs[p][...].astype(bf16)
    kbuf[past:, :] = kn_ref[0].astype(bf16)
    vbuf[past:, :] = vn_ref[0].astype(bf16)

    qrow = q_ref[0].astype(f32)
    row = lax.broadcasted_iota(i32, (16, LANES), 0)
    lane = lax.broadcasted_iota(i32, (16, LANES), 1)
    qmat = jnp.zeros((16, LANES), f32)
    for h in range(A_HEADS):
        qh = jnp.broadcast_to(qrow[:, h * LANES:(h + 1) * LANES], (16, LANES))
        sel = ((row >> 1) == h) & ((lane >= A_HEAD_DIM) == ((row & 1) == 1))
        qmat = jnp.where(sel, qh, qmat)
    s = lax.dot_general(qmat.astype(bf16), kbuf[...], (((1,), (1,)), ((), ())), preferred_element_type=f32)
    srow = lax.broadcasted_iota(i32, (16, n_col), 0)
    scol = lax.broadcasted_iota(i32, (16, n_col), 1)
    valid = ((scol & (A_HEADS - 1)) == (srow >> 1)) & (scol < past + A_HEADS) & (srow < 2 * A_HEADS)
    s = jnp.where(valid, s, NEG_BIG)
    m = jnp.max(s, axis=-1, keepdims=True)
    pr = jnp.where(valid, jnp.exp2(s - m), 0.0)
    den = jnp.maximum(jnp.sum(pr, axis=-1, keepdims=True), 1e-30)
    pn = pr / den
    o8 = jnp.dot(pn.astype(bf16), vbuf[...], preferred_element_type=f32)
    lam = _lam_value(lam_ref, lam_init)
    for h in range(A_HEADS):
        o = o8[2 * h:2 * h + 1, :] - lam * o8[2 * h + 1:2 * h + 2, :]
        o_ref[0, :, h * LANES:(h + 1) * LANES] = (
            _rms_norm(o, subg_ref[...]) * (1.0 - lam_init)).astype(o_ref.dtype)


def _attn_decode(q_s, k_new, v_new, lam_vecs, sub_g, cache_k4, cache_v4, page_table, layer, lam_init):
    bd, n_pages = page_table.shape
    rows_pg = PAGE_SIZE * A_HEADS
    body = functools.partial(_decode_body, n_pages=n_pages, lam_init=lam_init)

    def page_spec(p):
        return pl.BlockSpec((None, None, rows_pg, LANES),
                            lambda b, pt, p=p: (layer, pt[b * n_pages + p], 0, 0))

    grid_spec = pltpu.PrefetchScalarGridSpec(
        num_scalar_prefetch=1,
        grid=(bd,),
        in_specs=[pl.BlockSpec((1, 1, A_WIDTH), lambda b, pt: (b, 0, 0)),
                  pl.BlockSpec((1, 16, LANES), lambda b, pt: (b, 0, 0)),
                  pl.BlockSpec((1, 16, LANES), lambda b, pt: (b, 0, 0)),
                  pl.BlockSpec((4, A_HEAD_DIM), lambda b, pt: (0, 0)),
                  pl.BlockSpec((1, A_VDIM), lambda b, pt: (0, 0))]
        + [page_spec(p) for p in range(n_pages)] * 2,
        out_specs=pl.BlockSpec((1, 1, A_WIDTH), lambda b, pt: (b, 0, 0)),
        scratch_shapes=[pltpu.VMEM((n_pages * rows_pg + 16, LANES), bf16),
                        pltpu.VMEM((n_pages * rows_pg + 16, LANES), bf16)],
    )
    return pl.pallas_call(
        body,
        grid_spec=grid_spec,
        out_shape=jax.ShapeDtypeStruct((bd, 1, A_WIDTH), bf16),
        compiler_params=_params(("arbitrary",)),
    )(page_table.reshape(-1), q_s, k_new, v_new, lam_vecs, sub_g,
      *([cache_k4] * n_pages), *([cache_v4] * n_pages))


def _log_decay(glr, wa2, ba):
    z = jnp.dot(glr, wa2, preferred_element_type=f32, precision=lax.Precision.HIGHEST) + ba
    return (jnp.minimum(z, 0.0) - jnp.log(1.0 + jnp.exp(-jnp.abs(z)))) * (1.0 / G_TAU)


def _gla_finish(o, gr, g):
    return _rms_norm(o, g) * (gr * _sigmoid(gr))


def _gla_prompt_body(*refs, c, batch):
    n_tok = 5
    tok = refs[:n_tok * batch]
    wa2_ref, ba_ref, g_ref, o_ref, s_ref, state, tmp = refs[n_tok * batch:]
    for bi in range(batch):
        _gla_chunk(*(tok[n * batch + bi] for n in range(n_tok)), wa2_ref, ba_ref, g_ref,
                   o_ref.at[bi], s_ref.at[bi], state.at[bi], tmp.at[bi], c=c)


def _gla_chunk(q_ref, k_ref, v_ref, glr_ref, gr_ref, wa2_ref, ba_ref, g_ref, o_ref, s_ref, state, tmp, *, c):
    ci = pl.program_id(0)
    nsub = c // GLA_SUB

    @pl.when(ci == 0)
    def _():
        state[...] = jnp.zeros(state.shape, f32)

    la = _log_decay(glr_ref[...], wa2_ref[...], ba_ref[...])
    ri = lax.broadcasted_iota(i32, (c, c), 0)
    cj = lax.broadcasted_iota(i32, (c, c), 1)
    b = jnp.dot((ri >= cj).astype(f32), la, preferred_element_type=f32, precision=lax.Precision.HIGHEST)
    b_last = b[c - 1:c, :]
    q = q_ref[...]
    k = k_ref[...]
    v = v_ref[...]
    vb = v.astype(bf16)
    lane = lax.broadcasted_iota(i32, (1, G_K_WIDTH), 1)
    heads = [(lane >> 6) == h for h in range(G_HEADS)]

    def stack_heads(x):
        return jnp.concatenate([jnp.where(m, x, 0.0) for m in heads], axis=0).astype(bf16)

    sub_i = lax.broadcasted_iota(i32, (GLA_SUB, G_K_WIDTH), 0)
    for blk in range(nsub):
        r0 = blk * GLA_SUB
        q_b = q[r0:r0 + GLA_SUB]
        b_b = b[r0:r0 + GLA_SUB]
        for j in range(GLA_SUB):
            w = jnp.where(sub_i >= j, jnp.exp(jnp.minimum(b_b - b_b[j:j + 1], 0.0)), 0.0)
            t0 = (r0 + j) * GLA_SUB
            tmp[t0:t0 + GLA_SUB, :] = (q_b * w * k[r0 + j:r0 + j + 1]).astype(bf16)
    si = lax.broadcasted_iota(i32, (G_K_WIDTH, G_V_WIDTH), 0)
    sj = lax.broadcasted_iota(i32, (G_K_WIDTH, G_V_WIDTH), 1)
    seg = ((si >> 6) == (sj >> 7)).astype(bf16)
    pair = jnp.dot(tmp[...], seg, preferred_element_type=f32)
    o_rows = []
    for blk in range(nsub):
        r0 = blk * GLA_SUB
        acc = jnp.zeros((GLA_SUB, G_V_WIDTH), f32)
        for j in range(GLA_SUB):
            t0 = (r0 + j) * GLA_SUB
            acc = acc + pair[t0:t0 + GLA_SUB, :] * v[r0 + j:r0 + j + 1]
        o_rows.append(acc)
    o_diag = jnp.concatenate(o_rows, axis=0)

    att_rows = [[jnp.zeros((GLA_SUB, c), f32)] for _ in range(G_HEADS)]
    col = lax.broadcasted_iota(i32, (G_HEADS * GLA_SUB, c), 1)
    for blk in range(1, nsub):
        r0 = blk * GLA_SUB
        ref = b[r0 - 1:r0]
        q_b = q[r0:r0 + GLA_SUB] * jnp.exp(b[r0:r0 + GLA_SUB] - ref)
        k_b = (k * jnp.exp(jnp.minimum(ref - b, 0.0))).astype(bf16)
        a = lax.dot_general(stack_heads(q_b), k_b, (((1,), (1,)), ((), ())), preferred_element_type=f32)
        a = jnp.where(col < r0, a, 0.0)
        for h in range(G_HEADS):
            att_rows[h].append(a[h * GLA_SUB:(h + 1) * GLA_SUB])

    st = state[...]
    o_inter = jnp.dot(stack_heads(q * jnp.exp(b)), st.astype(bf16), preferred_element_type=f32)
    kdt = (k * jnp.exp(b_last - b)).T.astype(bf16)
    upd = jnp.dot(kdt, vb, preferred_element_type=f32)
    dec_col = jnp.exp(jnp.broadcast_to(b_last, (8, G_K_WIDTH)).T[:, 0:1])
    for h in range(G_HEADS):
        cols = slice(h * G_DV, (h + 1) * G_DV)
        att = jnp.concatenate(att_rows[h], axis=0).astype(bf16)
        o = (o_diag[:, cols] + jnp.dot(att, vb[:, cols], preferred_element_type=f32)
             + o_inter[h * c:(h + 1) * c])
        o_ref[:, cols] = _gla_finish(o, gr_ref[:, cols], g_ref[...]).astype(o_ref.dtype)
        r0 = h * G_DK
        state[r0:r0 + G_DK, :] = dec_col[r0:r0 + G_DK] * st[r0:r0 + G_DK, :] + upd[r0:r0 + G_DK, cols]

    @pl.when(ci == pl.num_programs(0) - 1)
    def _():
        s_ref[...] = state[...]


def _gla_prompt(gq, gk, gv, glr, gr, wa2, ba, gla_g, batch, seq):
    c = math.gcd(seq, GLA_CHUNK)
    n = seq // c
    body = functools.partial(_gla_prompt_body, c=c, batch=batch)

    def tok(width):
        return [pl.BlockSpec((c, width), lambda i, b=b: (b * n + i, 0)) for b in range(batch)]

    def whole(shape):
        return pl.BlockSpec(shape, lambda i: (0,) * len(shape))

    tok_arrays = [a for arr in (gq, gk, gv, glr, gr) for a in [arr] * batch]
    og, s_out = pl.pallas_call(
        body,
        grid=(n,),
        in_specs=tok(G_K_WIDTH) + tok(G_K_WIDTH) + tok(G_V_WIDTH) + tok(LANES) + tok(G_V_WIDTH)
        + [whole((LANES, G_K_WIDTH)), whole((1, G_K_WIDTH)), whole((1, G_DV))],
        out_specs=[pl.BlockSpec((batch, c, G_V_WIDTH), lambda i: (0, i, 0)),
                   pl.BlockSpec((batch, G_K_WIDTH, G_DV), lambda i: (0, 0, 0))],
        out_shape=[jax.ShapeDtypeStruct((batch, seq, G_V_WIDTH), bf16),
                   jax.ShapeDtypeStruct((batch, G_K_WIDTH, G_DV), f32)],
        scratch_shapes=[pltpu.VMEM((batch, G_K_WIDTH, G_DV), f32),
                        pltpu.VMEM((batch, c * GLA_SUB, G_K_WIDTH), bf16)],
        compiler_params=_params(("arbitrary",)),
    )(*tok_arrays, wa2, ba, gla_g)
    return og.reshape(batch * seq, G_V_WIDTH), s_out


def _gla_sample_body(q_ref, k_ref, v_ref, glr_ref, gr_ref, s_ref, wa2_ref, ba_ref, g_ref, o_ref, sn_ref):
    la = _log_decay(glr_ref[0], wa2_ref[...], ba_ref[...])
    rows = jnp.concatenate([jnp.broadcast_to(jnp.exp(la), (G_DK, G_K_WIDTH)),
                            jnp.broadcast_to(k_ref[0], (G_DK, G_K_WIDTH)),
                            jnp.broadcast_to(q_ref[0], (G_DK, G_K_WIDTH))], axis=0)
    ri = lax.broadcasted_iota(i32, (3 * G_DK, G_K_WIDTH), 0)
    li = lax.broadcasted_iota(i32, (3 * G_DK, G_K_WIDTH), 1)
    picked = jnp.where((ri & (G_DK - 1)) == (li & (G_DK - 1)), rows, 0.0)
    si = lax.broadcasted_iota(i32, (G_K_WIDTH, G_V_WIDTH), 0)
    sj = lax.broadcasted_iota(i32, (G_K_WIDTH, G_V_WIDTH), 1)
    seg = ((si >> 6) == (sj >> 7)).astype(f32)
    cols = jnp.dot(picked, seg, preferred_element_type=f32, precision=lax.Precision.HIGHEST)
    v = v_ref[0]
    for h in range(G_HEADS):
        sl = slice(h * G_DV, (h + 1) * G_DV)
        a_c = cols[0:G_DK, sl]
        k_c = cols[G_DK:2 * G_DK, sl]
        q_c = cols[2 * G_DK:3 * G_DK, sl]
        s_new = a_c * s_ref[0, h] + k_c * v[:, sl]
        sn_ref[0, h] = s_new
        o = jnp.sum(q_c * s_new, axis=0, keepdims=True)
        o_ref[0, :, sl] = _gla_finish(o, gr_ref[0][:, sl], g_ref[...]).astype(o_ref.dtype)


def _gla_sample(gq, gk, gv, glr, gr, s0, wa2, ba, gla_g):
    bd = gq.shape[0]

    def vec(width):
        return pl.BlockSpec((1, 1, width), lambda b: (b, 0, 0))

    def whole(shape):
        return pl.BlockSpec(shape, lambda b: (0,) * len(shape))

    st = pl.BlockSpec((1, G_HEADS, G_DK, G_DV), lambda b: (b, 0, 0, 0))
    return pl.pallas_call(
        _gla_sample_body,
        grid=(bd,),
        in_specs=[vec(G_K_WIDTH), vec(G_K_WIDTH), vec(G_V_WIDTH), vec(LANES), vec(G_V_WIDTH), st,
                  whole((LANES, G_K_WIDTH)), whole((1, G_K_WIDTH)), whole((1, G_DV))],
        out_specs=[vec(G_V_WIDTH), st],
        out_shape=[jax.ShapeDtypeStruct((bd, 1, G_V_WIDTH), bf16),
                   jax.ShapeDtypeStruct((bd, G_HEADS, G_DK, G_DV), f32)],
        compiler_params=_params(("parallel",)),
    )(gq, gk, gv, glr, gr, s0, wa2, ba, gla_g)


def _postmix_body(oa_ref, og_ref, za_ref, zb_ref, x_ref, wpa_ref, wpb_ref, wout_ref, g_ref, b_ref,
                  x1_o, xp_o, *, alpha):
    ya = jnp.dot(oa_ref[...], wpa_ref[...], preferred_element_type=f32)
    yb = jnp.dot(og_ref[...], wpb_ref[...], preferred_element_type=f32)
    merged = _sigmoid(za_ref[...].astype(f32)) * ya + _sigmoid(zb_ref[...].astype(f32)) * yb
    mix = jnp.dot(merged.astype(bf16), wout_ref[...], preferred_element_type=f32)
    x1 = _layer_norm(alpha * x_ref[...] + mix, g_ref[...], b_ref[...])
    x1_o[...] = x1
    xp_o[...] = _pack_rows(x1)


def _postmix(oa, og, za, zb, x, wpa, wpb, wout, g, b, alpha):
    t = x.shape[0]
    tm = _row_tile(t, (384, 256, 128, 64, 32, 16, 8))

    def tok(width):
        return pl.BlockSpec((tm, width), lambda i: (i, 0))

    def whole(shape):
        return pl.BlockSpec(shape, lambda i: (0,) * len(shape))

    return pl.pallas_call(
        functools.partial(_postmix_body, alpha=alpha),
        grid=(t // tm,),
        in_specs=[tok(A_WIDTH), tok(G_V_WIDTH), tok(D_MODEL), tok(D_MODEL), tok(D_MODEL),
                  whole((A_WIDTH, D_MODEL)), whole((G_V_WIDTH, D_MODEL)), whole((D_MODEL, D_MODEL)),
                  whole((1, D_MODEL)), whole((1, D_MODEL))],
        out_specs=[tok(D_MODEL), tok(HALF)],
        out_shape=[jax.ShapeDtypeStruct((t, D_MODEL), f32),
                   jax.ShapeDtypeStruct((t, HALF), u32)],
        compiler_params=_params(("parallel",)),
    )(oa, og, za, zb, x, wpa, wpb, wout, g, b)


def _router_body(x1_ref, wrt_ref, bcol_ref, tri_ref, eidx_o, gate_o, rank_o, cnt_o, cnt, *, tm):
    @pl.when(pl.program_id(0) == 0)
    def _():
        cnt[...] = jnp.zeros(cnt.shape, f32)

    logits = lax.dot_general(wrt_ref[...], x1_ref[...], (((1,), (1,)), ((), ())),
                             preferred_element_type=f32, precision=lax.Precision.HIGHEST)
    scores = _sigmoid(logits)
    biased = scores + bcol_ref[...]
    gsz = N_EXPERTS // N_GROUPS
    neg_inf = -jnp.inf

    gi = lax.broadcasted_iota(i32, (gsz, tm), 0)
    segs, gscore = [], []
    for g in range(N_GROUPS):
        seg = biased[g * gsz:(g + 1) * gsz, :]
        m1 = jnp.max(seg, axis=0, keepdims=True)
        i1 = jnp.min(jnp.where(seg == m1, gi, gsz), axis=0, keepdims=True)
        m2 = jnp.max(jnp.where(gi == i1, neg_inf, seg), axis=0, keepdims=True)
        segs.append(seg)
        gscore.append(m1 + m2)
    parts = []
    for g in range(N_GROUPS):
        beat = jnp.zeros((1, tm), i32)
        for o in range(N_GROUPS):
            if o != g:
                wins = (gscore[o] > gscore[g]) | ((gscore[o] == gscore[g]) & (o < g))
                beat = beat + wins.astype(i32)
        parts.append(jnp.where(beat < TOPK_GROUPS, segs[g], neg_inf))
    masked = jnp.concatenate(parts, axis=0)

    ei = lax.broadcasted_iota(i32, (N_EXPERTS, tm), 0)
    sel_rows, idx_rows = [], []
    chosen = jnp.zeros((N_EXPERTS, tm), f32)
    for _ in range(TOP_K):
        m = jnp.max(masked, axis=0, keepdims=True)
        idx = jnp.min(jnp.where(masked == m, ei, N_EXPERTS), axis=0, keepdims=True)
        hit = ei == idx
        sel_rows.append(jnp.sum(jnp.where(hit, scores, 0.0), axis=0, keepdims=True))
        idx_rows.append(idx)
        chosen = jnp.where(hit, 1.0, chosen)
        masked = jnp.where(hit, neg_inf, masked)
    s_sel = jnp.concatenate(sel_rows, axis=0)
    gate_o[...] = s_sel / jnp.sum(s_sel, axis=0, keepdims=True) * ROUTED_SCALE
    eidx_o[...] = jnp.concatenate(idx_rows, axis=0)
    before = jnp.dot(chosen.astype(bf16), tri_ref[...], preferred_element_type=f32) + cnt[...]
    rank_rows = [jnp.sum(jnp.where(ei == idx, before, 0.0), axis=0, keepdims=True) for idx in idx_rows]
    rank_o[...] = jnp.concatenate(rank_rows, axis=0).astype(i32)
    cnt[...] = cnt[...] + jnp.sum(chosen, axis=1, keepdims=True)
    cnt_o[...] = cnt[...]


def _router(x1, wrt, b_col):
    t = x1.shape[0]
    tm = _row_tile(t, (384, 256, 128))
    tri = (jnp.arange(tm)[:, None] < jnp.arange(tm)[None, :]).astype(bf16)

    def tokcol(dt):
        return pl.BlockSpec((TOP_K, tm), lambda i: (0, i)), jax.ShapeDtypeStruct((TOP_K, t), dt)

    def whole(shape):
        return pl.BlockSpec(shape, lambda i: (0,) * len(shape))

    specs, shapes = zip(tokcol(i32), tokcol(f32), tokcol(i32),
                        (whole((N_EXPERTS, 1)), jax.ShapeDtypeStruct((N_EXPERTS, 1), f32)))
    return pl.pallas_call(
        functools.partial(_router_body, tm=tm),
        grid=(t // tm,),
        in_specs=[pl.BlockSpec((tm, D_MODEL), lambda i: (i, 0)), whole((N_EXPERTS, D_MODEL)),
                  whole((N_EXPERTS, 1)), whole((tm, tm))],
        out_specs=list(specs),
        out_shape=list(shapes),
        scratch_shapes=[pltpu.VMEM((N_EXPERTS, 1), f32)],
        compiler_params=_params(("arbitrary",)),
    )(x1, wrt, b_col, tri)


def _dispatch_body(dest_ref, pad0_ref, pad1_ref, x_ref, xs_out, zbuf, sem, zsem, *, tm):

    @pl.when(pl.program_id(0) == 0)
    def _():
        zbuf[...] = jnp.zeros(zbuf.shape, u32)

        def zero_row(row):
            return pltpu.make_async_copy(zbuf.at[pl.ds(0, 1)], xs_out.at[pl.ds(row, 1)], zsem)

        def zero_group(row):
            return pltpu.make_async_copy(zbuf, xs_out.at[pl.ds(pl.multiple_of(row, SUBLANES), SUBLANES)], zsem)

        def per_expert(e, waiting):
            p0 = pad0_ref[e]
            p1 = pad1_ref[e]
            head = jnp.minimum((-p0) & (SUBLANES - 1), p1 - p0)
            groups = lax.shift_right_logical(p1 - p0 - head, 3)

            def rows(r, c):
                if waiting:
                    zero_row(0).wait()
                else:
                    zero_row(p0 + r).start()
                return c

            def grps(g, c):
                if waiting:
                    zero_group(0).wait()
                else:
                    zero_group(p0 + head + g * SUBLANES).start()
                return c

            lax.fori_loop(0, head, rows, 0)
            lax.fori_loop(0, groups, grps, 0)

        lax.fori_loop(0, N_EXPERTS, lambda e, c: (per_expert(e, False), c)[1], 0)
        lax.fori_loop(0, N_EXPERTS, lambda e, c: (per_expert(e, True), c)[1], 0)

    def copy(grp, sub, dst_row):
        return pltpu.make_async_copy(x_ref.at[grp, pl.ds(sub, 1)], xs_out.at[pl.ds(dst_row, 1)], sem)

    def issue(grp, carry):
        for sub in range(SUBLANES):
            for kk in range(TOP_K):
                copy(grp, sub, dest_ref[(grp * SUBLANES + sub) * TOP_K + kk]).start(priority=kk % 2)
        return carry

    lax.fori_loop(0, tm // SUBLANES, issue, 0)

    def drain(grp, carry):
        for _ in range(SUBLANES * TOP_K):
            copy(0, 0, 0).wait()
        return carry

    lax.fori_loop(0, tm // SUBLANES, drain, 0)


def _dispatch(xp, dest, pad0, pad1, n_rows):
    t = xp.shape[0]
    tm = _row_tile(t, (384, 256, 128, 64, 32, 16, 8))
    whole = pl.BlockSpec((N_EXPERTS,), lambda i: (0,), memory_space=pltpu.SMEM)
    return pl.pallas_call(
        functools.partial(_dispatch_body, tm=tm),
        grid=(t // tm,),
        in_specs=[pl.BlockSpec((tm * TOP_K,), lambda i: (i,), memory_space=pltpu.SMEM), whole, whole,
                  pl.BlockSpec((tm // SUBLANES, SUBLANES, HALF), lambda i: (i, 0, 0))],
        out_specs=pl.BlockSpec(memory_space=pl.ANY),
        out_shape=jax.ShapeDtypeStruct((n_rows, HALF), u32),
        scratch_shapes=[pltpu.VMEM((SUBLANES, HALF), u32), pltpu.SemaphoreType.DMA(()),
                        pltpu.SemaphoreType.DMA(())],
        compiler_params=_params(("arbitrary",)),
    )(dest, pad0, pad1, xp.reshape(t // SUBLANES, SUBLANES, HALF))


def _expert_weight_copies(seq_ref, w_hbm, w_buf, sems, seq_idx, *, layer):
    e = seq_ref[seq_idx]
    slot = lax.rem(seq_idx, WEIGHT_SLOTS)
    out = []
    for n in range(3):
        rows = w_buf[n].shape[1] // WEIGHT_PARTS
        for part in range(WEIGHT_PARTS):
            rng = pl.ds(part * rows, rows)
            out.append(pltpu.make_async_copy(w_hbm[n].at[layer, e, rng], w_buf[n].at[slot, rng], sems.at[slot, n]))
    return out


def _expert_body(bs_ref, seq_ref, ns_ref, nu_ref, xs_ref, wg_hbm, wu_hbm, wd_hbm, ys_ref,
                 wgf, wuf, wdf, sems, wgb, wub, wdb, *, layer):
    i = pl.program_id(0)
    j = bs_ref[i]
    copies = functools.partial(_expert_weight_copies, seq_ref, (wg_hbm, wu_hbm, wd_hbm), (wgf, wuf, wdf),
                               sems, layer=layer)

    @pl.when(i == 0)
    def _():
        for c in copies(0):
            c.start(priority=1)

        @pl.when(ns_ref[0] > 1)
        def _():
            for c in copies(1):
                c.start(priority=1)

    @pl.when((i == 0) | (j != bs_ref[jnp.maximum(i - 1, 0)]))
    def _():
        for c in copies(j):
            c.wait()
        slot = lax.rem(j, WEIGHT_SLOTS)
        wgb[...] = wgf[slot].astype(bf16)
        wub[...] = wuf[slot].astype(bf16)
        wdb[...] = wdf[slot].astype(bf16)

        @pl.when(j + 2 < ns_ref[0])
        def _():
            for c in copies(j + 2):
                c.start(priority=1)

    @pl.when(i < nu_ref[0])
    def _():
        lo, hi = _unpack_rows(xs_ref[...])
        lo = lo.astype(bf16)
        hi = hi.astype(bf16)
        g = (jnp.dot(lo, wgb[:HALF, :], preferred_element_type=f32)
             + jnp.dot(hi, wgb[HALF:, :], preferred_element_type=f32))
        u = (jnp.dot(lo, wub[:HALF, :], preferred_element_type=f32)
             + jnp.dot(hi, wub[HALF:, :], preferred_element_type=f32))
        hdn = (g * _sigmoid(g) * u).astype(bf16)
        ys_ref[...] = _pack_rows(jnp.dot(hdn, wdb[...], preferred_element_type=f32))

    @pl.when(i >= nu_ref[0])
    def _():
        ys_ref[...] = jnp.zeros(ys_ref.shape, u32)


def _experts(xs, blk_seq, seq_exp, n_seq, n_used, w_gate, w_up, w_down, layer):
    n_rows = xs.shape[0]
    n_blocks = n_rows // EXPERT_BLOCK
    hbm = pl.BlockSpec(memory_space=pl.ANY)
    grid_spec = pltpu.PrefetchScalarGridSpec(
        num_scalar_prefetch=4,
        grid=(n_blocks,),
        in_specs=[pl.BlockSpec((EXPERT_BLOCK, HALF), lambda i, bs, sq, ns, nu: (jnp.minimum(i, nu[0] - 1), 0)),
                  hbm, hbm, hbm],
        out_specs=pl.BlockSpec((EXPERT_BLOCK, HALF), lambda i, bs, sq, ns, nu: (i, 0)),
        scratch_shapes=[pltpu.VMEM((WEIGHT_SLOTS, D_MODEL, D_EXPERT), f32),
                        pltpu.VMEM((WEIGHT_SLOTS, D_MODEL, D_EXPERT), f32),
                        pltpu.VMEM((WEIGHT_SLOTS, D_EXPERT, D_MODEL), f32),
                        pltpu.SemaphoreType.DMA((WEIGHT_SLOTS, 3)),
                        pltpu.VMEM((D_MODEL, D_EXPERT), bf16), pltpu.VMEM((D_MODEL, D_EXPERT), bf16),
                        pltpu.VMEM((D_EXPERT, D_MODEL), bf16)],
    )
    return pl.pallas_call(
        functools.partial(_expert_body, layer=layer),
        grid_spec=grid_spec,
        out_shape=jax.ShapeDtypeStruct((n_rows, HALF), u32),
        compiler_params=_params(("arbitrary",)),
    )(blk_seq, seq_exp, n_seq, n_used, xs, w_gate, w_up, w_down)


def _combine_body(dest_ref, gate_ref, x1_ref, xp_ref, ys_hbm, wsg_ref, wsu_ref, wsd_ref, g_ref, b_ref,
                  o_ref, buf, sem, *, tm, alpha):
    def copy(src_row, kk, grp, sub):
        return pltpu.make_async_copy(ys_hbm.at[pl.ds(src_row, 1)], buf.at[kk, grp, pl.ds(sub, 1)], sem)

    def issue(grp, carry):
        for sub in range(SUBLANES):
            for kk in range(TOP_K):
                copy(dest_ref[(grp * SUBLANES + sub) * TOP_K + kk], kk, grp, sub).start(priority=kk % 2)
        return carry

    lax.fori_loop(0, tm // SUBLANES, issue, 0)

    lo, hi = _unpack_rows(xp_ref[...])
    lo = lo.astype(bf16)
    hi = hi.astype(bf16)
    sg = (jnp.dot(lo, wsg_ref[:HALF, :], preferred_element_type=f32)
          + jnp.dot(hi, wsg_ref[HALF:, :], preferred_element_type=f32))
    su = (jnp.dot(lo, wsu_ref[:HALF, :], preferred_element_type=f32)
          + jnp.dot(hi, wsu_ref[HALF:, :], preferred_element_type=f32))
    shared = jnp.dot((sg * _sigmoid(sg) * su).astype(bf16), wsd_ref[...], preferred_element_type=f32)

    def drain(grp, carry):
        for _ in range(SUBLANES * TOP_K):
            copy(0, 0, 0, 0).wait()
        return carry

    lax.fori_loop(0, tm // SUBLANES, drain, 0)

    gates = gate_ref[...]
    acc_lo = jnp.zeros((tm, HALF), f32)
    acc_hi = jnp.zeros((tm, HALF), f32)
    for kk in range(TOP_K):
        ylo, yhi = _unpack_rows(buf[kk].reshape(tm, HALF))
        gk = gates[:, kk:kk + 1]
        acc_lo = acc_lo + gk * ylo
        acc_hi = acc_hi + gk * yhi
    moe = jnp.concatenate([acc_lo, acc_hi], axis=-1) + shared
    o_ref[...] = _layer_norm(alpha * x1_ref[...] + moe, g_ref[...], b_ref[...])


def _combine(dest, gates, x1, xp, ys, wsg, wsu, wsd, g, b, alpha):
    t = x1.shape[0]
    tm = _row_tile(t, (384, 256, 128, 64, 32, 16, 8))

    def tok(width):
        return pl.BlockSpec((tm, width), lambda i: (i, 0))

    def whole(shape):
        return pl.BlockSpec(shape, lambda i: (0,) * len(shape))

    return pl.pallas_call(
        functools.partial(_combine_body, tm=tm, alpha=alpha),
        grid=(t // tm,),
        in_specs=[pl.BlockSpec((tm * TOP_K,), lambda i: (i,), memory_space=pltpu.SMEM),
                  tok(TOP_K), tok(D_MODEL), tok(HALF),
                  pl.BlockSpec(memory_space=pl.ANY),
                  whole((D_MODEL, D_EXPERT)), whole((D_MODEL, D_EXPERT)), whole((D_EXPERT, D_MODEL)),
                  whole((1, D_MODEL)), whole((1, D_MODEL))],
        out_specs=tok(D_MODEL),
        out_shape=jax.ShapeDtypeStruct((t, D_MODEL), f32),
        scratch_shapes=[pltpu.VMEM((TOP_K, tm // SUBLANES, SUBLANES, HALF), u32),
                        pltpu.SemaphoreType.DMA(())],
        compiler_params=_params(("arbitrary",)),
    )(dest, gates, x1, xp, ys, wsg, wsu, wsd, g, b)


def _dest_body(eidx_ref, rank_ref, ps_ref, dest_o, *, tm):
    ei = lax.broadcasted_iota(i32, (N_EXPERTS, tm), 0)
    ps = ps_ref[...]
    rows = [jnp.sum(jnp.where(ei == eidx_ref[kk:kk + 1, :], ps, 0.0), axis=0, keepdims=True)
            for kk in range(TOP_K)]
    dest_o[...] = jnp.concatenate(rows, axis=0).astype(i32) + rank_ref[...]


def _dest_rows(eidx_t, rank_t, pad_start):
    t = eidx_t.shape[1]
    tm = _row_tile(t, (384, 256, 128))
    blk = pl.BlockSpec((TOP_K, tm), lambda i: (0, i))
    return pl.pallas_call(
        functools.partial(_dest_body, tm=tm),
        grid=(t // tm,),
        in_specs=[blk, blk, pl.BlockSpec((N_EXPERTS, 1), lambda i: (0, 0))],
        out_specs=blk,
        out_shape=jax.ShapeDtypeStruct((TOP_K, t), i32),
        compiler_params=_params(("parallel",)),
    )(eidx_t, rank_t, pad_start.astype(f32).reshape(N_EXPERTS, 1))


def _layout(eidx_t, rank_t, counts):
    t = eidx_t.shape[1]
    blk = EXPERT_BLOCK
    padded = (counts + blk - 1) // blk * blk
    pad_end = jnp.cumsum(padded)
    pad_start = pad_end - padded
    dest = _dest_rows(eidx_t, rank_t, pad_start)
    n_rows = -(-(t * TOP_K + N_EXPERTS * (blk - 1)) // blk) * blk
    n_blocks = n_rows // blk
    n_used = (pad_end[-1] // blk).astype(i32)
    first_row = jnp.minimum(jnp.arange(n_blocks), n_used - 1) * blk
    blk_exp = jnp.sum(pad_end[None, :] <= first_row[:, None], axis=1).astype(i32)
    blk_exp = jnp.minimum(blk_exp, N_EXPERTS - 1)
    used = counts > 0
    seq_of = jnp.cumsum(used.astype(i32)) - 1
    ids = jnp.arange(N_EXPERTS, dtype=i32)
    seq_exp = jnp.sum(jnp.where(used[None, :] & (seq_of[None, :] == ids[:, None]), ids[None, :], 0), axis=1)
    blk_seq = jnp.sum(jnp.where(blk_exp[:, None] == ids[None, :], seq_of[None, :], 0), axis=1)
    n_seq = jnp.sum(used.astype(i32))
    pads = ((pad_start + counts).astype(i32), pad_end.astype(i32))
    return dest, pads, blk_seq.astype(i32), seq_exp.astype(i32), n_seq.reshape(1), n_used.reshape(1), n_rows


def kernel(x_prompt, x_sample, cache_k, cache_v, state_gla, page_table, w_in, w_a2, b_a, lam_q1, lam_k1,
           lam_q2, lam_k2, sub_g, gla_g, w_pa, w_pb, w_out, ln1_g, ln1_b, w_router, b_router, w_gate, w_up,
           w_down, ws_gate, ws_up, ws_down, ln2_g, ln2_b):
    depth = w_in.shape[0]
    batch, seq, _ = x_prompt.shape
    bd = x_sample.shape[0]
    tp = batch * seq
    alpha = (2 * depth) ** 0.25
    n_pool = cache_k.shape[1]
    cache_k4 = cache_k.reshape(depth, n_pool, PAGE_SIZE * A_HEADS, 2 * A_HEAD_DIM)
    cache_v4 = cache_v.reshape(depth, n_pool, PAGE_SIZE * A_HEADS, A_VDIM)

    x = jnp.concatenate([x_prompt.reshape(tp, D_MODEL), x_sample.reshape(bd, D_MODEL)], axis=0)
    t_all = tp + bd
    kv_all = ()
    sp_l, ss_l = [], []
    for l in range(depth):
        lam_init = 0.8 - 0.6 * math.exp(-0.3 * l)
        wl = w_in[l]
        w_re = jnp.concatenate([wl[:, :3072], wl[:, 3088:5136], wl[:, 3072:3088],
                                jnp.zeros((D_MODEL, LANES - G_GATE_RANK), f32)], axis=1).astype(bf16)
        wa2 = jnp.concatenate([w_a2[l], jnp.zeros((LANES - G_GATE_RANK, G_K_WIDTH), f32)], axis=0)
        ba = b_a[l].reshape(1, G_K_WIDTH)
        lam_vecs = jnp.stack([lam_q1[l], lam_k1[l], lam_q2[l], lam_k2[l]]).astype(f32)
        subg = sub_g[l].reshape(1, A_VDIM)
        glag = gla_g[l].reshape(1, G_DV)

        wvt = wl[:, C_V:C_GQ].T.astype(bf16)
        q, kf_all, kb, vf_all, vt, gq, gk, gv, gr, za, zb, glr = _inproj(x, w_re, wvt, l, depth, kv_all)
        kv_all = (kf_all, vf_all)
        kf = kf_all[l * t_all + tp:(l + 1) * t_all]
        vf = vf_all[l * t_all + tp:(l + 1) * t_all]

        oa_p = _attn_prompt(q, kb, vt, lam_vecs, sub_g[l].reshape(A_VDIM, 1), batch, seq, lam_init)
        tail_pad = ((0, 0), (0, 16 - A_HEADS), (0, 0))
        oa_s = _attn_decode(q[tp:].reshape(bd, 1, A_WIDTH),
                            jnp.pad(kf.reshape(bd, A_HEADS, 2 * A_HEAD_DIM), tail_pad),
                            jnp.pad(vf.reshape(bd, A_HEADS, A_VDIM), tail_pad),
                            lam_vecs, subg, cache_k4, cache_v4, page_table, l, lam_init)
        og_p, s_p = _gla_prompt(gq, gk, gv, glr, gr, wa2, ba, glag, batch, seq)
        og_s, s_s = _gla_sample(gq[tp:].reshape(bd, 1, -1), gk[tp:].reshape(bd, 1, -1),
                                gv[tp:].reshape(bd, 1, -1), glr[tp:].reshape(bd, 1, -1),
                                gr[tp:].reshape(bd, 1, -1), state_gla[l], wa2, ba, glag)
        oa = jnp.concatenate([oa_p, oa_s.reshape(bd, A_WIDTH)], axis=0)
        og = jnp.concatenate([og_p, og_s.reshape(bd, G_V_WIDTH)], axis=0)

        x1, xp = _postmix(oa, og, za, zb, x, w_pa[l].astype(bf16), w_pb[l].astype(bf16),
                          w_out[l].astype(bf16), ln1_g[l].reshape(1, -1), ln1_b[l].reshape(1, -1), alpha)
        eidx_t, gates_t, rank_t, counts = _router(x1, w_router[l].T, b_router[l].reshape(N_EXPERTS, 1))
        dest_t, pads, blk_seq, seq_exp, n_seq, n_used, n_rows = _layout(eidx_t, rank_t,
                                                                        counts.reshape(-1).astype(i32))
        dest = dest_t.T.reshape(-1)
        xs = _dispatch(xp, dest, pads[0], pads[1], n_rows)
        ys = _experts(xs, blk_seq, seq_exp, n_seq, n_used, w_gate, w_up, w_down, l)
        x = _combine(dest, gates_t.T, x1, xp, ys, ws_gate[l].astype(bf16), ws_up[l].astype(bf16),
                     ws_down[l].astype(bf16), ln2_g[l].reshape(1, -1), ln2_b[l].reshape(1, -1), alpha)

        sp_l.append(s_p.reshape(batch, G_HEADS, G_DK, G_DV))
        ss_l.append(s_s)

    y_prompt = x[:tp].reshape(batch, seq, D_MODEL)
    y_sample = x[tp:].reshape(bd, 1, D_MODEL)
    k3 = kv_all[0].reshape(depth, t_all, A_WIDTH)
    v3 = kv_all[1].reshape(depth, t_all, A_WIDTH)
    pages = (depth, batch, seq // PAGE_SIZE, PAGE_SIZE, A_HEADS, A_VDIM)
    return (y_prompt, y_sample, k3[:, :tp].reshape(pages), v3[:, :tp].reshape(pages),
            k3[:, tp:].reshape(depth, bd, 1, A_HEADS, 2 * A_HEAD_DIM),
            v3[:, tp:].reshape(depth, bd, 1, A_HEADS, A_VDIM), jnp.stack(sp_l), jnp.stack(ss_l))
```

```python
import functools
import math

import jax
import jax.numpy as jnp
from jax import lax
from jax.experimental import pallas as pl
from jax.experimental.pallas import tpu as pltpu

f32 = jnp.float32
bf16 = jnp.bfloat16
u32 = jnp.uint32
i32 = jnp.int32

D_MODEL = 1024
A_HEADS = 4
A_HEAD_DIM = 64
A_VDIM = 128
A_WIDTH = A_HEADS * A_VDIM
G_HEADS = 4
G_DK = 64
G_DV = 128
G_K_WIDTH = G_HEADS * G_DK
G_V_WIDTH = G_HEADS * G_DV
G_GATE_RANK = 16
G_TAU = 16.0
N_EXPERTS = 256
TOP_K = 8
N_GROUPS = 8
TOPK_GROUPS = 4
D_EXPERT = 256
ROUTED_SCALE = 2.5
PAGE_SIZE = 128
LN_EPS = 1e-5
RMS_EPS = 1e-6

LANES = 128
SUBLANES = 8
VMEM_LIMIT = 56 * 1024 * 1024

NEG_BIG = -1e30
HALF = D_MODEL // 2
EXPERT_BLOCK = 256
WEIGHT_SLOTS = 3
GLA_CHUNK = 64
GLA_SUB = 16
ATTN_TQ = 1024
ATTN_TK = 1024
ONES_ROWS = 16
LOG2E = math.log2(math.e)

C_Q, C_K, C_V, C_GQ, C_GK, C_GV, C_GR, C_ZA, C_ZB, C_GLR, C_END = (
    0, 512, 1024, 1536, 1792, 2048, 2560, 3072, 4096, 5120, 5248)


def _params(sem, vmem=VMEM_LIMIT):
    return pltpu.CompilerParams(dimension_semantics=sem, vmem_limit_bytes=vmem)


def _row_tile(n, cands=(512, 384, 256, 128, 64, 32, 16, 8)):
    for c in cands:
        if n % c == 0:
            return c
    raise ValueError(f"no row tile for {n}")


def _sigmoid(x):
    return 1.0 / (1.0 + jnp.exp(-x))


def _pack_rows(x):
    lo = lax.bitcast_convert_type(x[:, :HALF].astype(bf16).astype(f32), u32) >> 16
    hi = lax.bitcast_convert_type(x[:, HALF:].astype(bf16).astype(f32), u32) & jnp.uint32(0xFFFF0000)
    return lo | hi


def _unpack_rows(w):
    lo = lax.bitcast_convert_type(w << 16, f32)
    hi = lax.bitcast_convert_type(w & jnp.uint32(0xFFFF0000), f32)
    return lo, hi


def _layer_norm(h, g, b):
    mu = jnp.mean(h, axis=-1, keepdims=True)
    d = h - mu
    var = jnp.mean(d * d, axis=-1, keepdims=True)
    return d * lax.rsqrt(var + LN_EPS) * g + b


def _rms_norm(o, g):
    return o * lax.rsqrt(jnp.mean(o * o, axis=-1, keepdims=True) + RMS_EPS) * g


def _lam_value(lam_ref, lam_init):
    l = lam_ref[...]
    s1 = jnp.sum(l[0:1] * l[1:2], axis=-1, keepdims=True)
    s2 = jnp.sum(l[2:3] * l[3:4], axis=-1, keepdims=True)
    return jnp.exp(s1) - jnp.exp(s2) + lam_init


def _inproj_body(x_ref, w_ref, wvt_ref, *refs):
    q_o, kf_o, kb_o, vf_o, vt_o, gq_o, gk_o, gv_o, gr_o, za_o, zb_o, glr_o = refs[-12:]
    xb = x_ref[...].astype(bf16)

    def mm(c0, c1):
        return jnp.dot(xb, w_ref[:, c0:c1], preferred_element_type=f32)

    q_o[...] = (mm(C_Q, C_K) * (A_HEAD_DIM ** -0.5 * LOG2E)).astype(bf16)
    k = mm(C_K, C_V)
    kf_o[...] = k
    kb_o[...] = k.astype(bf16)
    vf_o[...] = mm(C_V, C_GQ)
    vt_o[...] = lax.dot_general(wvt_ref[...], xb, (((1,), (1,)), ((), ())),
                                preferred_element_type=f32).astype(bf16)
    gq_o[...] = mm(C_GQ, C_GK) * (G_DK ** -0.5)
    gk_o[...] = mm(C_GK, C_GV)
    gv_o[...] = mm(C_GV, C_GR)
    gr_o[...] = mm(C_GR, C_ZA)
    za_o[...] = mm(C_ZA, C_ZB).astype(bf16)
    zb_o[...] = mm(C_ZB, C_GLR).astype(bf16)
    glr_o[...] = mm(C_GLR, C_END)


def _inproj(x, w, wvt, layer, depth, kv_prev):
    t = x.shape[0]
    tm = _row_tile(t, (384, 256, 128))
    nb = t // tm
    outs = [(512, bf16), (512, f32), (512, bf16), (512, f32), None, (256, f32), (256, f32),
            (512, f32), (512, f32), (1024, bf16), (1024, bf16), (LANES, f32)]
    shared = (1, 3)
    out_specs, out_shape = [], []
    for n, o in enumerate(outs):
        if o is None:
            out_specs.append(pl.BlockSpec((A_WIDTH, tm), lambda i: (0, i)))
            out_shape.append(jax.ShapeDtypeStruct((A_WIDTH, t), bf16))
        elif n in shared:
            out_specs.append(pl.BlockSpec((tm, o[0]), lambda i: (layer * nb + i, 0)))
            out_shape.append(jax.ShapeDtypeStruct((depth * t, o[0]), o[1]))
        else:
            out_specs.append(pl.BlockSpec((tm, o[0]), lambda i: (i, 0)))
            out_shape.append(jax.ShapeDtypeStruct((t, o[0]), o[1]))
    return pl.pallas_call(
        _inproj_body,
        grid=(nb,),
        in_specs=[pl.BlockSpec((tm, D_MODEL), lambda i: (i, 0)),
                  pl.BlockSpec((D_MODEL, C_END), lambda i: (0, 0)),
                  pl.BlockSpec((A_WIDTH, D_MODEL), lambda i: (0, 0))]
        + [pl.BlockSpec(memory_space=pl.ANY)] * len(kv_prev),
        out_specs=out_specs,
        out_shape=out_shape,
        input_output_aliases={3 + n: pos for n, pos in enumerate(shared[:len(kv_prev)])},
        compiler_params=_params(("parallel",)),
    )(x, w, wvt, *kv_prev)


def _attn_body(qi_tab, kj_tab, diag_tab, last_tab, q_ref, k_ref, vt_ref, lam_ref, subg_ref, o_ref,
               m, a, *, tq, tk, lam_init):
    p = pl.program_id(2)
    qi = qi_tab[p]
    kj = kj_tab[p]

    @pl.when(kj == 0)
    def _():
        m[...] = jnp.full(m.shape, NEG_BIG, f32)
        a[...] = jnp.zeros(a.shape, f32)

    q = q_ref[...]
    k = k_ref[...]
    vt = jnp.concatenate([vt_ref[...], jnp.ones((ONES_ROWS, tk), bf16)], axis=0)
    lane = lax.broadcasted_iota(i32, (1, LANES), 1)
    zero = jnp.zeros_like(q)
    qq = jnp.concatenate([jnp.where(lane < A_HEAD_DIM, q, zero), jnp.where(lane >= A_HEAD_DIM, q, zero)],
                         axis=0)

    def step(masked):
        s = lax.dot_general(k, qq, (((1,), (1,)), ((), ())), preferred_element_type=f32)
        if masked:
            kpos = kj * tk + lax.broadcasted_iota(i32, (tk, 2 * tq), 0)
            qpos = qi * tq + (lax.broadcasted_iota(i32, (tk, 2 * tq), 1) & (tq - 1))
            s = jnp.where(kpos <= qpos, s, NEG_BIG)
        m_prev = m[...]
        m_new = jnp.maximum(m_prev, jnp.max(s, axis=0, keepdims=True))
        alpha = jnp.exp2(m_prev - m_new)
        pr = jnp.exp2(s - m_new).astype(bf16)
        a[...] = alpha * a[...] + jnp.dot(vt, pr, preferred_element_type=f32)
        m[...] = m_new

    @pl.when(diag_tab[p] == 1)
    def _():
        step(True)

    @pl.when(diag_tab[p] == 0)
    def _():
        step(False)

    @pl.when(last_tab[p] == 1)
    def _():
        lam = _lam_value(lam_ref, lam_init)
        ot = (a[:A_VDIM, :tq] / a[A_VDIM:A_VDIM + 1, :tq]
              - lam * (a[:A_VDIM, tq:] / a[A_VDIM:A_VDIM + 1, tq:]))
        ms = jnp.mean(ot * ot, axis=0, keepdims=True)
        on = ot * lax.rsqrt(ms + RMS_EPS) * subg_ref[...] * (1.0 - lam_init)
        o_ref[...] = on.T.astype(o_ref.dtype)


def _attn_prompt(q, k, vt, lam_vecs, sub_g_col, batch, seq, lam_init):
    tq = min(ATTN_TQ, seq)
    tk = min(ATTN_TK, seq)
    nq, nk = seq // tq, seq // tk
    qi_l, kj_l, dg_l, ls_l = [], [], [], []
    for qi in range(nq):
        last = ((qi + 1) * tq - 1) // tk
        for kj in range(last + 1):
            qi_l.append(qi)
            kj_l.append(kj)
            dg_l.append(1 if (kj + 1) * tk - 1 > qi * tq else 0)
            ls_l.append(1 if kj == last else 0)
    tabs = [jnp.asarray(t, i32) for t in (qi_l, kj_l, dg_l, ls_l)]
    n_pairs = len(qi_l)
    body = functools.partial(_attn_body, tq=tq, tk=tk, lam_init=lam_init)
    grid_spec = pltpu.PrefetchScalarGridSpec(
        num_scalar_prefetch=4,
        grid=(batch, A_HEADS, n_pairs),
        in_specs=[
            pl.BlockSpec((tq, LANES), lambda b, h, p, qt, kt, dt, lt: (b * nq + qt[p], h)),
            pl.BlockSpec((tk, LANES), lambda b, h, p, qt, kt, dt, lt: (b * nk + kt[p], h)),
            pl.BlockSpec((A_VDIM, tk), lambda b, h, p, qt, kt, dt, lt: (h, b * nk + kt[p])),
            pl.BlockSpec((4, A_HEAD_DIM), lambda b, h, p, *_: (0, 0)),
            pl.BlockSpec((A_VDIM, 1), lambda b, h, p, *_: (0, 0)),
        ],
        out_specs=pl.BlockSpec((tq, LANES), lambda b, h, p, qt, kt, dt, lt: (b * nq + qt[p], h)),
        scratch_shapes=[pltpu.VMEM((1, 2 * tq), f32), pltpu.VMEM((A_VDIM + ONES_ROWS, 2 * tq), f32)],
    )
    return pl.pallas_call(
        body,
        grid_spec=grid_spec,
        out_shape=jax.ShapeDtypeStruct((batch * seq, A_WIDTH), bf16),
        compiler_params=_params(("parallel", "parallel", "arbitrary")),
    )(*tabs, q, k, vt, lam_vecs, sub_g_col)


def _decode_body(pt_ref, q_ref, kn_ref, vn_ref, lam_ref, subg_ref, *refs, n_pages, lam_init):
    k_refs = refs[:n_pages]
    v_refs = refs[n_pages:2 * n_pages]
    o_ref = refs[2 * n_pages]
    kbuf, vbuf = refs[2 * n_pages + 1:]
    rows_pg = PAGE_SIZE * A_HEADS
    past = n_pages * rows_pg
    tail = 16
    n_col = past + tail

    for p in range(n_pages):
        kbuf[p * rows_pg:(p + 1) * rows_pg, :] = k_refs[p][...].astype(bf16)
        vbuf[p * rows_pg:(p + 1) * rows_pg, :] = v_refs[p][...].astype(bf16)
    kbuf[past:, :] = kn_ref[0].astype(bf16)
    vbuf[past:, :] = vn_ref[0].astype(bf16)

    qrow = q_ref[0].astype(f32)
    row = lax.broadcasted_iota(i32, (16, LANES), 0)
    lane = lax.broadcasted_iota(i32, (16, LANES), 1)
    qmat = jnp.zeros((16, LANES), f32)
    for h in range(A_HEADS):
        qh = jnp.broadcast_to(qrow[:, h * LANES:(h + 1) * LANES], (16, LANES))
        sel = ((row >> 1) == h) & ((lane >= A_HEAD_DIM) == ((row & 1) == 1))
        qmat = jnp.where(sel, qh, qmat)
    s = lax.dot_general(qmat.astype(bf16), kbuf[...], (((1,), (1,)), ((), ())), preferred_element_type=f32)
    srow = lax.broadcasted_iota(i32, (16, n_col), 0)
    scol = lax.broadcasted_iota(i32, (16, n_col), 1)
    valid = ((scol & (A_HEADS - 1)) == (srow >> 1)) & (scol < past + A_HEADS) & (srow < 2 * A_HEADS)
    s = jnp.where(valid, s, NEG_BIG)
    m = jnp.max(s, axis=-1, keepdims=True)
    pr = jnp.where(valid, jnp.exp2(s - m), 0.0)
    den = jnp.maximum(jnp.sum(pr, axis=-1, keepdims=True), 1e-30)
    pn = pr / den
    o8 = jnp.dot(pn.astype(bf16), vbuf[...], preferred_element_type=f32)
    lam = _lam_value(lam_ref, lam_init)
    for h in range(A_HEADS):
        o = o8[2 * h:2 * h + 1, :] - lam * o8[2 * h + 1:2 * h + 2, :]
        o_ref[0, :, h * LANES:(h + 1) * LANES] = (
            _rms_norm(o, subg_ref[...]) * (1.0 - lam_init)).astype(o_ref.dtype)


def _attn_decode(q_s, k_new, v_new, lam_vecs, sub_g, cache_k4, cache_v4, page_table, layer, lam_init):
    bd, n_pages = page_table.shape
    rows_pg = PAGE_SIZE * A_HEADS
    body = functools.partial(_decode_body, n_pages=n_pages, lam_init=lam_init)

    def page_spec(p):
        return pl.BlockSpec((None, None, rows_pg, LANES),
                            lambda b, pt, p=p: (layer, pt[b * n_pages + p], 0, 0))

    grid_spec = pltpu.PrefetchScalarGridSpec(
        num_scalar_prefetch=1,
        grid=(bd,),
        in_specs=[pl.BlockSpec((1, 1, A_WIDTH), lambda b, pt: (b, 0, 0)),
                  pl.BlockSpec((1, 16, LANES), lambda b, pt: (b, 0, 0)),
                  pl.BlockSpec((1, 16, LANES), lambda b, pt: (b, 0, 0)),
                  pl.BlockSpec((4, A_HEAD_DIM), lambda b, pt: (0, 0)),
                  pl.BlockSpec((1, A_VDIM), lambda b, pt: (0, 0))]
        + [page_spec(p) for p in range(n_pages)] * 2,
        out_specs=pl.BlockSpec((1, 1, A_WIDTH), lambda b, pt: (b, 0, 0)),
        scratch_shapes=[pltpu.VMEM((n_pages * rows_pg + 16, LANES), bf16),
                        pltpu.VMEM((n_pages * rows_pg + 16, LANES), bf16)],
    )
    return pl.pallas_call(
        body,
        grid_spec=grid_spec,
        out_shape=jax.ShapeDtypeStruct((bd, 1, A_WIDTH), bf16),
        compiler_params=_params(("arbitrary",)),
    )(page_table.reshape(-1), q_s, k_new, v_new, lam_vecs, sub_g,
      *([cache_k4] * n_pages), *([cache_v4] * n_pages))


def _log_decay(glr, wa2, ba):
    z = jnp.dot(glr, wa2, preferred_element_type=f32, precision=lax.Precision.HIGHEST) + ba
    return (jnp.minimum(z, 0.0) - jnp.log(1.0 + jnp.exp(-jnp.abs(z)))) * (1.0 / G_TAU)


def _gla_finish(o, gr, g):
    return _rms_norm(o, g) * (gr * _sigmoid(gr))


def _gla_prompt_body(q_ref, k_ref, v_ref, glr_ref, gr_ref, wa2_ref, ba_ref, g_ref, o_ref, s_ref, state, tmp,
                     *, c):
    ci = pl.program_id(1)
    nsub = c // GLA_SUB

    @pl.when(ci == 0)
    def _():
        state[...] = jnp.zeros(state.shape, f32)

    la = _log_decay(glr_ref[...], wa2_ref[...], ba_ref[...])
    ri = lax.broadcasted_iota(i32, (c, c), 0)
    cj = lax.broadcasted_iota(i32, (c, c), 1)
    b = jnp.dot((ri >= cj).astype(f32), la, preferred_element_type=f32, precision=lax.Precision.HIGHEST)
    b_last = b[c - 1:c, :]
    q = q_ref[...]
    k = k_ref[...]
    v = v_ref[...]
    vb = v.astype(bf16)
    lane = lax.broadcasted_iota(i32, (1, G_K_WIDTH), 1)
    heads = [(lane >> 6) == h for h in range(G_HEADS)]

    def stack_heads(x):
        return jnp.concatenate([jnp.where(m, x, 0.0) for m in heads], axis=0).astype(bf16)

    sub_i = lax.broadcasted_iota(i32, (GLA_SUB, G_K_WIDTH), 0)
    for blk in range(nsub):
        r0 = blk * GLA_SUB
        q_b = q[r0:r0 + GLA_SUB]
        b_b = b[r0:r0 + GLA_SUB]
        for j in range(GLA_SUB):
            w = jnp.where(sub_i >= j, jnp.exp(jnp.minimum(b_b - b_b[j:j + 1], 0.0)), 0.0)
            t0 = (r0 + j) * GLA_SUB
            tmp[t0:t0 + GLA_SUB, :] = (q_b * w * k[r0 + j:r0 + j + 1]).astype(bf16)
    si = lax.broadcasted_iota(i32, (G_K_WIDTH, G_V_WIDTH), 0)
    sj = lax.broadcasted_iota(i32, (G_K_WIDTH, G_V_WIDTH), 1)
    seg = ((si >> 6) == (sj >> 7)).astype(bf16)
    pair = jnp.dot(tmp[...], seg, preferred_element_type=f32)
    o_rows = []
    for blk in range(nsub):
        r0 = blk * GLA_SUB
        acc = jnp.zeros((GLA_SUB, G_V_WIDTH), f32)
        for j in range(GLA_SUB):
            t0 = (r0 + j) * GLA_SUB
            acc = acc + pair[t0:t0 + GLA_SUB, :] * v[r0 + j:r0 + j + 1]
        o_rows.append(acc)
    o_diag = jnp.concatenate(o_rows, axis=0)

    att_rows = [[jnp.zeros((GLA_SUB, c), f32)] for _ in range(G_HEADS)]
    col = lax.broadcasted_iota(i32, (G_HEADS * GLA_SUB, c), 1)
    for blk in range(1, nsub):
        r0 = blk * GLA_SUB
        ref = b[r0 - 1:r0]
        q_b = q[r0:r0 + GLA_SUB] * jnp.exp(b[r0:r0 + GLA_SUB] - ref)
        k_b = (k * jnp.exp(jnp.minimum(ref - b, 0.0))).astype(bf16)
        a = lax.dot_general(stack_heads(q_b), k_b, (((1,), (1,)), ((), ())), preferred_element_type=f32)
        a = jnp.where(col < r0, a, 0.0)
        for h in range(G_HEADS):
            att_rows[h].append(a[h * GLA_SUB:(h + 1) * GLA_SUB])

    st = state[...]
    o_inter = jnp.dot(stack_heads(q * jnp.exp(b)), st.astype(bf16), preferred_element_type=f32)
    kdt = (k * jnp.exp(b_last - b)).T.astype(bf16)
    upd = jnp.dot(kdt, vb, preferred_element_type=f32)
    dec_col = jnp.exp(jnp.broadcast_to(b_last, (8, G_K_WIDTH)).T[:, 0:1])
    for h in range(G_HEADS):
        cols = slice(h * G_DV, (h + 1) * G_DV)
        att = jnp.concatenate(att_rows[h], axis=0).astype(bf16)
        o = (o_diag[:, cols] + jnp.dot(att, vb[:, cols], preferred_element_type=f32)
             + o_inter[h * c:(h + 1) * c])
        o_ref[:, cols] = _gla_finish(o, gr_ref[:, cols], g_ref[...]).astype(o_ref.dtype)
        r0 = h * G_DK
        state[r0:r0 + G_DK, :] = dec_col[r0:r0 + G_DK] * st[r0:r0 + G_DK, :] + upd[r0:r0 + G_DK, cols]

    @pl.when(ci == pl.num_programs(1) - 1)
    def _():
        s_ref[0] = state[...]


def _gla_prompt(gq, gk, gv, glr, gr, wa2, ba, gla_g, batch, seq):
    c = math.gcd(seq, GLA_CHUNK)
    n = seq // c
    body = functools.partial(_gla_prompt_body, c=c)

    def tok(width):
        return pl.BlockSpec((c, width), lambda b, i: (b * n + i, 0))

    def whole(shape):
        return pl.BlockSpec(shape, lambda b, i: (0,) * len(shape))

    return pl.pallas_call(
        body,
        grid=(batch, n),
        in_specs=[tok(G_K_WIDTH), tok(G_K_WIDTH), tok(G_V_WIDTH), tok(LANES), tok(G_V_WIDTH),
                  whole((LANES, G_K_WIDTH)), whole((1, G_K_WIDTH)), whole((1, G_DV))],
        out_specs=[tok(G_V_WIDTH),
                   pl.BlockSpec((1, G_K_WIDTH, G_DV), lambda b, i: (b, 0, 0))],
        out_shape=[jax.ShapeDtypeStruct((batch * seq, G_V_WIDTH), bf16),
                   jax.ShapeDtypeStruct((batch, G_K_WIDTH, G_DV), f32)],
        scratch_shapes=[pltpu.VMEM((G_K_WIDTH, G_DV), f32),
                        pltpu.VMEM((c * GLA_SUB, G_K_WIDTH), bf16)],
        compiler_params=_params(("parallel", "arbitrary")),
    )(gq, gk, gv, glr, gr, wa2, ba, gla_g)


def _gla_sample_body(q_ref, k_ref, v_ref, glr_ref, gr_ref, s_ref, wa2_ref, ba_ref, g_ref, o_ref, sn_ref,
                     *, group):
    ri = lax.broadcasted_iota(i32, (3 * G_DK, G_K_WIDTH), 0)
    li = lax.broadcasted_iota(i32, (3 * G_DK, G_K_WIDTH), 1)
    diag = (ri & (G_DK - 1)) == (li & (G_DK - 1))
    si = lax.broadcasted_iota(i32, (G_K_WIDTH, G_V_WIDTH), 0)
    sj = lax.broadcasted_iota(i32, (G_K_WIDTH, G_V_WIDTH), 1)
    seg = ((si >> 6) == (sj >> 7)).astype(f32)
    for n in range(group):
        la = _log_decay(glr_ref[n], wa2_ref[...], ba_ref[...])
        rows = jnp.concatenate([jnp.broadcast_to(jnp.exp(la), (G_DK, G_K_WIDTH)),
                                jnp.broadcast_to(k_ref[n], (G_DK, G_K_WIDTH)),
                                jnp.broadcast_to(q_ref[n], (G_DK, G_K_WIDTH))], axis=0)
        picked = jnp.where(diag, rows, 0.0)
        cols = jnp.dot(picked, seg, preferred_element_type=f32, precision=lax.Precision.HIGHEST)
        v = v_ref[n]
        for h in range(G_HEADS):
            sl = slice(h * G_DV, (h + 1) * G_DV)
            a_c = cols[0:G_DK, sl]
            k_c = cols[G_DK:2 * G_DK, sl]
            q_c = cols[2 * G_DK:3 * G_DK, sl]
            s_new = a_c * s_ref[n, h] + k_c * v[:, sl]
            sn_ref[n, h] = s_new
            o = jnp.sum(q_c * s_new, axis=0, keepdims=True)
            o_ref[n, :, sl] = _gla_finish(o, gr_ref[n][:, sl], g_ref[...]).astype(o_ref.dtype)


def _gla_sample(gq, gk, gv, glr, gr, s0, wa2, ba, gla_g):
    bd = gq.shape[0]
    group = math.gcd(bd, SUBLANES)

    def vec(width):
        return pl.BlockSpec((group, 1, width), lambda b: (b, 0, 0))

    def whole(shape):
        return pl.BlockSpec(shape, lambda b: (0,) * len(shape))

    st = pl.BlockSpec((group, G_HEADS, G_DK, G_DV), lambda b: (b, 0, 0, 0))
    return pl.pallas_call(
        functools.partial(_gla_sample_body, group=group),
        grid=(bd // group,),
        in_specs=[vec(G_K_WIDTH), vec(G_K_WIDTH), vec(G_V_WIDTH), vec(LANES), vec(G_V_WIDTH), st,
                  whole((LANES, G_K_WIDTH)), whole((1, G_K_WIDTH)), whole((1, G_DV))],
        out_specs=[vec(G_V_WIDTH), st],
        out_shape=[jax.ShapeDtypeStruct((bd, 1, G_V_WIDTH), bf16),
                   jax.ShapeDtypeStruct((bd, G_HEADS, G_DK, G_DV), f32)],
        compiler_params=_params(("parallel",)),
    )(gq, gk, gv, glr, gr, s0, wa2, ba, gla_g)


def _postmix_body(oa_ref, og_ref, za_ref, zb_ref, x_ref, wpa_ref, wpb_ref, wout_ref, g_ref, b_ref,
                  x1_o, xp_o, *, alpha):
    ya = jnp.dot(oa_ref[...], wpa_ref[...], preferred_element_type=f32)
    yb = jnp.dot(og_ref[...], wpb_ref[...], preferred_element_type=f32)
    merged = _sigmoid(za_ref[...].astype(f32)) * ya + _sigmoid(zb_ref[...].astype(f32)) * yb
    mix = jnp.dot(merged.astype(bf16), wout_ref[...], preferred_element_type=f32)
    x1 = _layer_norm(alpha * x_ref[...] + mix, g_ref[...], b_ref[...])
    x1_o[...] = x1
    xp_o[...] = _pack_rows(x1)


def _postmix(oa, og, za, zb, x, wpa, wpb, wout, g, b, alpha):
    t = x.shape[0]
    tm = _row_tile(t, (384, 256, 128, 64, 32, 16, 8))

    def tok(width):
        return pl.BlockSpec((tm, width), lambda i: (i, 0))

    def whole(shape):
        return pl.BlockSpec(shape, lambda i: (0,) * len(shape))

    return pl.pallas_call(
        functools.partial(_postmix_body, alpha=alpha),
        grid=(t // tm,),
        in_specs=[tok(A_WIDTH), tok(G_V_WIDTH), tok(D_MODEL), tok(D_MODEL), tok(D_MODEL),
                  whole((A_WIDTH, D_MODEL)), whole((G_V_WIDTH, D_MODEL)), whole((D_MODEL, D_MODEL)),
                  whole((1, D_MODEL)), whole((1, D_MODEL))],
        out_specs=[tok(D_MODEL), tok(HALF)],
        out_shape=[jax.ShapeDtypeStruct((t, D_MODEL), f32),
                   jax.ShapeDtypeStruct((t, HALF), u32)],
        compiler_params=_params(("parallel",)),
    )(oa, og, za, zb, x, wpa, wpb, wout, g, b)


def _router_body(x1_ref, wrt_ref, bcol_ref, tri_ref, eidx_o, gate_o, rank_o, cnt_o, cnt, *, tm):
    @pl.when(pl.program_id(0) == 0)
    def _():
        cnt[...] = jnp.zeros(cnt.shape, f32)

    logits = lax.dot_general(wrt_ref[...], x1_ref[...], (((1,), (1,)), ((), ())),
                             preferred_element_type=f32, precision=lax.Precision.HIGHEST)
    scores = _sigmoid(logits)
    biased = scores + bcol_ref[...]
    gsz = N_EXPERTS // N_GROUPS
    neg_inf = -jnp.inf

    gi = lax.broadcasted_iota(i32, (gsz, tm), 0)
    segs, gscore = [], []
    for g in range(N_GROUPS):
        seg = biased[g * gsz:(g + 1) * gsz, :]
        m1 = jnp.max(seg, axis=0, keepdims=True)
        i1 = jnp.min(jnp.where(seg == m1, gi, gsz), axis=0, keepdims=True)
        m2 = jnp.max(jnp.where(gi == i1, neg_inf, seg), axis=0, keepdims=True)
        segs.append(seg)
        gscore.append(m1 + m2)
    parts = []
    for g in range(N_GROUPS):
        beat = jnp.zeros((1, tm), i32)
        for o in range(N_GROUPS):
            if o != g:
                wins = (gscore[o] > gscore[g]) | ((gscore[o] == gscore[g]) & (o < g))
                beat = beat + wins.astype(i32)
        parts.append(jnp.where(beat < TOPK_GROUPS, segs[g], neg_inf))
    masked = jnp.concatenate(parts, axis=0)

    ei = lax.broadcasted_iota(i32, (N_EXPERTS, tm), 0)
    sel_rows, idx_rows = [], []
    chosen = jnp.zeros((N_EXPERTS, tm), f32)
    for _ in range(TOP_K):
        m = jnp.max(masked, axis=0, keepdims=True)
        idx = jnp.min(jnp.where(masked == m, ei, N_EXPERTS), axis=0, keepdims=True)
        hit = ei == idx
        sel_rows.append(jnp.sum(jnp.where(hit, scores, 0.0), axis=0, keepdims=True))
        idx_rows.append(idx)
        chosen = jnp.where(hit, 1.0, chosen)
        masked = jnp.where(hit, neg_inf, masked)
    s_sel = jnp.concatenate(sel_rows, axis=0)
    gate_o[...] = s_sel / jnp.sum(s_sel, axis=0, keepdims=True) * ROUTED_SCALE
    eidx_o[...] = jnp.concatenate(idx_rows, axis=0)
    before = jnp.dot(chosen.astype(bf16), tri_ref[...], preferred_element_type=f32) + cnt[...]
    rank_rows = [jnp.sum(jnp.where(ei == idx, before, 0.0), axis=0, keepdims=True) for idx in idx_rows]
    rank_o[...] = jnp.concatenate(rank_rows, axis=0).astype(i32)
    cnt[...] = cnt[...] + jnp.sum(chosen, axis=1, keepdims=True)
    cnt_o[...] = cnt[...]


def _router(x1, wrt, b_col):
    t = x1.shape[0]
    tm = _row_tile(t, (384, 256, 128))
    tri = (jnp.arange(tm)[:, None] < jnp.arange(tm)[None, :]).astype(bf16)

    def tokcol(dt):
        return pl.BlockSpec((TOP_K, tm), lambda i: (0, i)), jax.ShapeDtypeStruct((TOP_K, t), dt)

    def whole(shape):
        return pl.BlockSpec(shape, lambda i: (0,) * len(shape))

    specs, shapes = zip(tokcol(i32), tokcol(f32), tokcol(i32),
                        (whole((N_EXPERTS, 1)), jax.ShapeDtypeStruct((N_EXPERTS, 1), f32)))
    return pl.pallas_call(
        functools.partial(_router_body, tm=tm),
        grid=(t // tm,),
        in_specs=[pl.BlockSpec((tm, D_MODEL), lambda i: (i, 0)), whole((N_EXPERTS, D_MODEL)),
                  whole((N_EXPERTS, 1)), whole((tm, tm))],
        out_specs=list(specs),
        out_shape=list(shapes),
        scratch_shapes=[pltpu.VMEM((N_EXPERTS, 1), f32)],
        compiler_params=_params(("arbitrary",)),
    )(x1, wrt, b_col, tri)


def _dispatch_body(dest_ref, pad0_ref, pad1_ref, x_ref, xs_out, zbuf, sem, zsem, *, tm):

    @pl.when(pl.program_id(0) == 0)
    def _():
        zbuf[...] = jnp.zeros(zbuf.shape, u32)

        def zero_row(row):
            return pltpu.make_async_copy(zbuf.at[pl.ds(0, 1)], xs_out.at[pl.ds(row, 1)], zsem)

        def zero_group(row):
            return pltpu.make_async_copy(zbuf, xs_out.at[pl.ds(pl.multiple_of(row, SUBLANES), SUBLANES)], zsem)

        def per_expert(e, waiting):
            p0 = pad0_ref[e]
            p1 = pad1_ref[e]
            head = jnp.minimum((-p0) & (SUBLANES - 1), p1 - p0)
            groups = lax.shift_right_logical(p1 - p0 - head, 3)

            def rows(r, c):
                if waiting:
                    zero_row(0).wait()
                else:
                    zero_row(p0 + r).start()
                return c

            def grps(g, c):
                if waiting:
                    zero_group(0).wait()
                else:
                    zero_group(p0 + head + g * SUBLANES).start()
                return c

            lax.fori_loop(0, head, rows, 0)
            lax.fori_loop(0, groups, grps, 0)

        lax.fori_loop(0, N_EXPERTS, lambda e, c: (per_expert(e, False), c)[1], 0)
        lax.fori_loop(0, N_EXPERTS, lambda e, c: (per_expert(e, True), c)[1], 0)

    def copy(grp, sub, dst_row):
        return pltpu.make_async_copy(x_ref.at[grp, pl.ds(sub, 1)], xs_out.at[pl.ds(dst_row, 1)], sem)

    def issue(grp, carry):
        for sub in range(SUBLANES):
            for kk in range(TOP_K):
                copy(grp, sub, dest_ref[(grp * SUBLANES + sub) * TOP_K + kk]).start(priority=kk % 2)
        return carry

    lax.fori_loop(0, tm // SUBLANES, issue, 0)

    def drain(grp, carry):
        for _ in range(SUBLANES * TOP_K):
            copy(0, 0, 0).wait()
        return carry

    lax.fori_loop(0, tm // SUBLANES, drain, 0)


def _dispatch(xp, dest, pad0, pad1, n_rows):
    t = xp.shape[0]
    tm = _row_tile(t, (384, 256, 128, 64, 32, 16, 8))
    whole = pl.BlockSpec((N_EXPERTS,), lambda i: (0,), memory_space=pltpu.SMEM)
    return pl.pallas_call(
        functools.partial(_dispatch_body, tm=tm),
        grid=(t // tm,),
        in_specs=[pl.BlockSpec((tm * TOP_K,), lambda i: (i,), memory_space=pltpu.SMEM), whole, whole,
                  pl.BlockSpec((tm // SUBLANES, SUBLANES, HALF), lambda i: (i, 0, 0))],
        out_specs=pl.BlockSpec(memory_space=pl.ANY),
        out_shape=jax.ShapeDtypeStruct((n_rows, HALF), u32),
        scratch_shapes=[pltpu.VMEM((SUBLANES, HALF), u32), pltpu.SemaphoreType.DMA(()),
                        pltpu.SemaphoreType.DMA(())],
        compiler_params=_params(("arbitrary",)),
    )(dest, pad0, pad1, xp.reshape(t // SUBLANES, SUBLANES, HALF))


def _expert_weight_copies(seq_ref, w_hbm, w_buf, sems, seq_idx, *, layer):
    e = seq_ref[seq_idx]
    slot = lax.rem(seq_idx, WEIGHT_SLOTS)
    return [pltpu.make_async_copy(w_hbm[n].at[layer, e], w_buf[n].at[slot], sems.at[slot, n]) for n in range(3)]


def _expert_body(bs_ref, seq_ref, ns_ref, nu_ref, xs_ref, wg_hbm, wu_hbm, wd_hbm, ys_ref,
                 wgf, wuf, wdf, sems, wgb, wub, wdb, *, layer):
    i = pl.program_id(0)
    j = bs_ref[i]
    copies = functools.partial(_expert_weight_copies, seq_ref, (wg_hbm, wu_hbm, wd_hbm), (wgf, wuf, wdf),
                               sems, layer=layer)

    @pl.when(i == 0)
    def _():
        for c in copies(0):
            c.start()

        @pl.when(ns_ref[0] > 1)
        def _():
            for c in copies(1):
                c.start()

    @pl.when((i == 0) | (j != bs_ref[jnp.maximum(i - 1, 0)]))
    def _():
        for c in copies(j):
            c.wait()
        slot = lax.rem(j, WEIGHT_SLOTS)
        wgb[...] = wgf[slot].astype(bf16)
        wub[...] = wuf[slot].astype(bf16)
        wdb[...] = wdf[slot].astype(bf16)

        @pl.when(j + 2 < ns_ref[0])
        def _():
            for c in copies(j + 2):
                c.start()

    @pl.when(i < nu_ref[0])
    def _():
        lo, hi = _unpack_rows(xs_ref[...])
        lo = lo.astype(bf16)
        hi = hi.astype(bf16)
        g = (jnp.dot(lo, wgb[:HALF, :], preferred_element_type=f32)
             + jnp.dot(hi, wgb[HALF:, :], preferred_element_type=f32))
        u = (jnp.dot(lo, wub[:HALF, :], preferred_element_type=f32)
             + jnp.dot(hi, wub[HALF:, :], preferred_element_type=f32))
        hdn = (g * _sigmoid(g) * u).astype(bf16)
        ys_ref[...] = _pack_rows(jnp.dot(hdn, wdb[...], preferred_element_type=f32))

    @pl.when(i >= nu_ref[0])
    def _():
        ys_ref[...] = jnp.zeros(ys_ref.shape, u32)


def _experts(xs, blk_seq, seq_exp, n_seq, n_used, w_gate, w_up, w_down, layer):
    n_rows = xs.shape[0]
    n_blocks = n_rows // EXPERT_BLOCK
    hbm = pl.BlockSpec(memory_space=pl.ANY)
    grid_spec = pltpu.PrefetchScalarGridSpec(
        num_scalar_prefetch=4,
        grid=(n_blocks,),
        in_specs=[pl.BlockSpec((EXPERT_BLOCK, HALF), lambda i, bs, sq, ns, nu: (jnp.minimum(i, nu[0] - 1), 0)),
                  hbm, hbm, hbm],
        out_specs=pl.BlockSpec((EXPERT_BLOCK, HALF), lambda i, bs, sq, ns, nu: (i, 0)),
        scratch_shapes=[pltpu.VMEM((WEIGHT_SLOTS, D_MODEL, D_EXPERT), f32),
                        pltpu.VMEM((WEIGHT_SLOTS, D_MODEL, D_EXPERT), f32),
                        pltpu.VMEM((WEIGHT_SLOTS, D_EXPERT, D_MODEL), f32),
                        pltpu.SemaphoreType.DMA((WEIGHT_SLOTS, 3)),
                        pltpu.VMEM((D_MODEL, D_EXPERT), bf16), pltpu.VMEM((D_MODEL, D_EXPERT), bf16),
                        pltpu.VMEM((D_EXPERT, D_MODEL), bf16)],
    )
    return pl.pallas_call(
        functools.partial(_expert_body, layer=layer),
        grid_spec=grid_spec,
        out_shape=jax.ShapeDtypeStruct((n_rows, HALF), u32),
        compiler_params=_params(("arbitrary",)),
    )(blk_seq, seq_exp, n_seq, n_used, xs, w_gate, w_up, w_down)


def _combine_body(dest_ref, gate_ref, x1_ref, xp_ref, ys_hbm, wsg_ref, wsu_ref, wsd_ref, g_ref, b_ref,
                  o_ref, buf, sem, *, tm, alpha):
    def copy(src_row, kk, grp, sub):
        return pltpu.make_async_copy(ys_hbm.at[pl.ds(src_row, 1)], buf.at[kk, grp, pl.ds(sub, 1)], sem)

    def issue(grp, carry):
        for sub in range(SUBLANES):
            for kk in range(TOP_K):
                copy(dest_ref[(grp * SUBLANES + sub) * TOP_K + kk], kk, grp, sub).start(priority=kk % 2)
        return carry

    lax.fori_loop(0, tm // SUBLANES, issue, 0)

    lo, hi = _unpack_rows(xp_ref[...])
    lo = lo.astype(bf16)
    hi = hi.astype(bf16)
    sg = (jnp.dot(lo, wsg_ref[:HALF, :], preferred_element_type=f32)
          + jnp.dot(hi, wsg_ref[HALF:, :], preferred_element_type=f32))
    su = (jnp.dot(lo, wsu_ref[:HALF, :], preferred_element_type=f32)
          + jnp.dot(hi, wsu_ref[HALF:, :], preferred_element_type=f32))
    shared = jnp.dot((sg * _sigmoid(sg) * su).astype(bf16), wsd_ref[...], preferred_element_type=f32)

    def drain(grp, carry):
        for _ in range(SUBLANES * TOP_K):
            copy(0, 0, 0, 0).wait()
        return carry

    lax.fori_loop(0, tm // SUBLANES, drain, 0)

    gates = gate_ref[...]
    acc_lo = jnp.zeros((tm, HALF), f32)
    acc_hi = jnp.zeros((tm, HALF), f32)
    for kk in range(TOP_K):
        ylo, yhi = _unpack_rows(buf[kk].reshape(tm, HALF))
        gk = gates[:, kk:kk + 1]
        acc_lo = acc_lo + gk * ylo
        acc_hi = acc_hi + gk * yhi
    moe = jnp.concatenate([acc_lo, acc_hi], axis=-1) + shared
    o_ref[...] = _layer_norm(alpha * x1_ref[...] + moe, g_ref[...], b_ref[...])


def _combine(dest, gates, x1, xp, ys, wsg, wsu, wsd, g, b, alpha):
    t = x1.shape[0]
    tm = _row_tile(t, (384, 256, 128, 64, 32, 16, 8))

    def tok(width):
        return pl.BlockSpec((tm, width), lambda i: (i, 0))

    def whole(shape):
        return pl.BlockSpec(shape, lambda i: (0,) * len(shape))

    return pl.pallas_call(
        functools.partial(_combine_body, tm=tm, alpha=alpha),
        grid=(t // tm,),
        in_specs=[pl.BlockSpec((tm * TOP_K,), lambda i: (i,), memory_space=pltpu.SMEM),
                  tok(TOP_K), tok(D_MODEL), tok(HALF),
                  pl.BlockSpec(memory_space=pl.ANY),
                  whole((D_MODEL, D_EXPERT)), whole((D_MODEL, D_EXPERT)), whole((D_EXPERT, D_MODEL)),
                  whole((1, D_MODEL)), whole((1, D_MODEL))],
        out_specs=tok(D_MODEL),
        out_shape=jax.ShapeDtypeStruct((t, D_MODEL), f32),
        scratch_shapes=[pltpu.VMEM((TOP_K, tm // SUBLANES, SUBLANES, HALF), u32),
                        pltpu.SemaphoreType.DMA(())],
        compiler_params=_params(("arbitrary",)),
    )(dest, gates, x1, xp, ys, wsg, wsu, wsd, g, b)


def _dest_body(eidx_ref, rank_ref, ps_ref, dest_o, *, tm):
    ei = lax.broadcasted_iota(i32, (N_EXPERTS, tm), 0)
    ps = ps_ref[...]
    rows = [jnp.sum(jnp.where(ei == eidx_ref[kk:kk + 1, :], ps, 0.0), axis=0, keepdims=True)
            for kk in range(TOP_K)]
    dest_o[...] = jnp.concatenate(rows, axis=0).astype(i32) + rank_ref[...]


def _dest_rows(eidx_t, rank_t, pad_start):
    t = eidx_t.shape[1]
    tm = _row_tile(t, (384, 256, 128))
    blk = pl.BlockSpec((TOP_K, tm), lambda i: (0, i))
    return pl.pallas_call(
        functools.partial(_dest_body, tm=tm),
        grid=(t // tm,),
        in_specs=[blk, blk, pl.BlockSpec((N_EXPERTS, 1), lambda i: (0, 0))],
        out_specs=blk,
        out_shape=jax.ShapeDtypeStruct((TOP_K, t), i32),
        compiler_params=_params(("parallel",)),
    )(eidx_t, rank_t, pad_start.astype(f32).reshape(N_EXPERTS, 1))


def _layout(eidx_t, rank_t, counts):
    t = eidx_t.shape[1]
    blk = EXPERT_BLOCK
    padded = (counts + blk - 1) // blk * blk
    pad_end = jnp.cumsum(padded)
    pad_start = pad_end - padded
    dest = _dest_rows(eidx_t, rank_t, pad_start)
    n_rows = -(-(t * TOP_K + N_EXPERTS * (blk - 1)) // blk) * blk
    n_blocks = n_rows // blk
    n_used = (pad_end[-1] // blk).astype(i32)
    first_row = jnp.minimum(jnp.arange(n_blocks), n_used - 1) * blk
    blk_exp = jnp.sum(pad_end[None, :] <= first_row[:, None], axis=1).astype(i32)
    blk_exp = jnp.minimum(blk_exp, N_EXPERTS - 1)
    used = counts > 0
    seq_of = jnp.cumsum(used.astype(i32)) - 1
    ids = jnp.arange(N_EXPERTS, dtype=i32)
    seq_exp = jnp.sum(jnp.where(used[None, :] & (seq_of[None, :] == ids[:, None]), ids[None, :], 0), axis=1)
    blk_seq = jnp.sum(jnp.where(blk_exp[:, None] == ids[None, :], seq_of[None, :], 0), axis=1)
    n_seq = jnp.sum(used.astype(i32))
    pads = ((pad_start + counts).astype(i32), pad_end.astype(i32))
    return dest, pads, blk_seq.astype(i32), seq_exp.astype(i32), n_seq.reshape(1), n_used.reshape(1), n_rows


def kernel(x_prompt, x_sample, cache_k, cache_v, state_gla, page_table, w_in, w_a2, b_a, lam_q1, lam_k1,
           lam_q2, lam_k2, sub_g, gla_g, w_pa, w_pb, w_out, ln1_g, ln1_b, w_router, b_router, w_gate, w_up,
           w_down, ws_gate, ws_up, ws_down, ln2_g, ln2_b):
    depth = w_in.shape[0]
    batch, seq, _ = x_prompt.shape
    bd = x_sample.shape[0]
    tp = batch * seq
    alpha = (2 * depth) ** 0.25
    n_pool = cache_k.shape[1]
    cache_k4 = cache_k.reshape(depth, n_pool, PAGE_SIZE * A_HEADS, 2 * A_HEAD_DIM)
    cache_v4 = cache_v.reshape(depth, n_pool, PAGE_SIZE * A_HEADS, A_VDIM)

    x = jnp.concatenate([x_prompt.reshape(tp, D_MODEL), x_sample.reshape(bd, D_MODEL)], axis=0)
    t_all = tp + bd
    kv_all = ()
    sp_l, ss_l = [], []
    for l in range(depth):
        lam_init = 0.8 - 0.6 * math.exp(-0.3 * l)
        wl = w_in[l]
        w_re = jnp.concatenate([wl[:, :3072], wl[:, 3088:5136], wl[:, 3072:3088],
                                jnp.zeros((D_MODEL, LANES - G_GATE_RANK), f32)], axis=1).astype(bf16)
        wa2 = jnp.concatenate([w_a2[l], jnp.zeros((LANES - G_GATE_RANK, G_K_WIDTH), f32)], axis=0)
        ba = b_a[l].reshape(1, G_K_WIDTH)
        lam_vecs = jnp.stack([lam_q1[l], lam_k1[l], lam_q2[l], lam_k2[l]]).astype(f32)
        subg = sub_g[l].reshape(1, A_VDIM)
        glag = gla_g[l].reshape(1, G_DV)

        wvt = wl[:, C_V:C_GQ].T.astype(bf16)
        q, kf_all, kb, vf_all, vt, gq, gk, gv, gr, za, zb, glr = _inproj(x, w_re, wvt, l, depth, kv_all)
        kv_all = (kf_all, vf_all)
        kf = kf_all[l * t_all + tp:(l + 1) * t_all]
        vf = vf_all[l * t_all + tp:(l + 1) * t_all]

        oa_p = _attn_prompt(q, kb, vt, lam_vecs, sub_g[l].reshape(A_VDIM, 1), batch, seq, lam_init)
        tail_pad = ((0, 0), (0, 16 - A_HEADS), (0, 0))
        oa_s = _attn_decode(q[tp:].reshape(bd, 1, A_WIDTH),
                            jnp.pad(kf.reshape(bd, A_HEADS, 2 * A_HEAD_DIM), tail_pad),
                            jnp.pad(vf.reshape(bd, A_HEADS, A_VDIM), tail_pad),
                            lam_vecs, subg, cache_k4, cache_v4, page_table, l, lam_init)
        og_p, s_p = _gla_prompt(gq, gk, gv, glr, gr, wa2, ba, glag, batch, seq)
        og_s, s_s = _gla_sample(gq[tp:].reshape(bd, 1, -1), gk[tp:].reshape(bd, 1, -1),
                                gv[tp:].reshape(bd, 1, -1), glr[tp:].reshape(bd, 1, -1),
                                gr[tp:].reshape(bd, 1, -1), state_gla[l], wa2, ba, glag)
        oa = jnp.concatenate([oa_p, oa_s.reshape(bd, A_WIDTH)], axis=0)
        og = jnp.concatenate([og_p, og_s.reshape(bd, G_V_WIDTH)], axis=0)

        x1, xp = _postmix(oa, og, za, zb, x, w_pa[l].astype(bf16), w_pb[l].astype(bf16),
                          w_out[l].astype(bf16), ln1_g[l].reshape(1, -1), ln1_b[l].reshape(1, -1), alpha)
        eidx_t, gates_t, rank_t, counts = _router(x1, w_router[l].T, b_router[l].reshape(N_EXPERTS, 1))
        dest_t, pads, blk_seq, seq_exp, n_seq, n_used, n_rows = _layout(eidx_t, rank_t,
                                                                        counts.reshape(-1).astype(i32))
        dest = dest_t.T.reshape(-1)
        xs = _dispatch(xp, dest, pads[0], pads[1], n_rows)
        ys = _experts(xs, blk_seq, seq_exp, n_seq, n_used, w_gate, w_up, w_down, l)
        x = _combine(dest, gates_t.T, x1, xp, ys, ws_gate[l].astype(bf16), ws_up[l].astype(bf16),
                     ws_down[l].astype(bf16), ln2_g[l].reshape(1, -1), ln2_b[l].reshape(1, -1), alpha)

        sp_l.append(s_p.reshape(batch, G_HEADS, G_DK, G_DV))
        ss_l.append(s_s)

    y_prompt = x[:tp].reshape(batch, seq, D_MODEL)
    y_sample = x[tp:].reshape(bd, 1, D_MODEL)
    k3 = kv_all[0].reshape(depth, t_all, A_WIDTH)
    v3 = kv_all[1].reshape(depth, t_all, A_WIDTH)
    pages = (depth, batch, seq // PAGE_SIZE, PAGE_SIZE, A_HEADS, A_VDIM)
    return (y_prompt, y_sample, k3[:, :tp].reshape(pages), v3[:, :tp].reshape(pages),
            k3[:, tp:].reshape(depth, bd, 1, A_HEADS, 2 * A_HEAD_DIM),
            v3[:, tp:].reshape(depth, bd, 1, A_HEADS, A_VDIM), jnp.stack(sp_l), jnp.stack(ss_l))
```

```python
import functools
import math

import jax
import jax.numpy as jnp
from jax import lax
from jax.experimental import pallas as pl
from jax.experimental.pallas import tpu as pltpu

f32 = jnp.float32
bf16 = jnp.bfloat16
u32 = jnp.uint32
i32 = jnp.int32

D_MODEL = 1024
A_HEADS = 4
A_HEAD_DIM = 64
A_VDIM = 128
A_WIDTH = A_HEADS * A_VDIM
G_HEADS = 4
G_DK = 64
G_DV = 128
G_K_WIDTH = G_HEADS * G_DK
G_V_WIDTH = G_HEADS * G_DV
G_GATE_RANK = 16
G_TAU = 16.0
N_EXPERTS = 256
TOP_K = 8
N_GROUPS = 8
TOPK_GROUPS = 4
D_EXPERT = 256
ROUTED_SCALE = 2.5
PAGE_SIZE = 128
LN_EPS = 1e-5
RMS_EPS = 1e-6

LANES = 128
SUBLANES = 8
VMEM_LIMIT = 56 * 1024 * 1024

NEG_BIG = -1e30
HALF = D_MODEL // 2
EXPERT_BLOCK = 256
WEIGHT_SLOTS = 3
GLA_CHUNK = 64
GLA_SUB = 16
ATTN_TQ = 1024
ATTN_TK = 1024
ONES_ROWS = 16
LOG2E = math.log2(math.e)

C_Q, C_K, C_V, C_GQ, C_GK, C_GV, C_GR, C_ZA, C_ZB, C_GLR, C_END = (
    0, 512, 1024, 1536, 1792, 2048, 2560, 3072, 4096, 5120, 5248)


def _params(sem, vmem=VMEM_LIMIT):
    return pltpu.CompilerParams(dimension_semantics=sem, vmem_limit_bytes=vmem)


def _row_tile(n, cands=(512, 384, 256, 128, 64, 32, 16, 8)):
    for c in cands:
        if n % c == 0:
            return c
    raise ValueError(f"no row tile for {n}")


def _sigmoid(x):
    return 1.0 / (1.0 + jnp.exp(-x))


def _pack_rows(x):
    lo = lax.bitcast_convert_type(x[:, :HALF].astype(bf16).astype(f32), u32) >> 16
    hi = lax.bitcast_convert_type(x[:, HALF:].astype(bf16).astype(f32), u32) & jnp.uint32(0xFFFF0000)
    return lo | hi


def _unpack_rows(w):
    lo = lax.bitcast_convert_type(w << 16, f32)
    hi = lax.bitcast_convert_type(w & jnp.uint32(0xFFFF0000), f32)
    return lo, hi


def _layer_norm(h, g, b):
    mu = jnp.mean(h, axis=-1, keepdims=True)
    d = h - mu
    var = jnp.mean(d * d, axis=-1, keepdims=True)
    return d * lax.rsqrt(var + LN_EPS) * g + b


def _rms_norm(o, g):
    return o * lax.rsqrt(jnp.mean(o * o, axis=-1, keepdims=True) + RMS_EPS) * g


def _lam_value(lam_ref, lam_init):
    l = lam_ref[...]
    s1 = jnp.sum(l[0:1] * l[1:2], axis=-1, keepdims=True)
    s2 = jnp.sum(l[2:3] * l[3:4], axis=-1, keepdims=True)
    return jnp.exp(s1) - jnp.exp(s2) + lam_init


def _inproj_body(x_ref, w_ref, wvt_ref, *refs):
    q_o, kf_o, kb_o, vf_o, vt_o, gq_o, gk_o, gv_o, gr_o, za_o, zb_o, glr_o = refs[-12:]
    xb = x_ref[...].astype(bf16)

    def mm(c0, c1):
        return jnp.dot(xb, w_ref[:, c0:c1], preferred_element_type=f32)

    q_o[...] = (mm(C_Q, C_K) * (A_HEAD_DIM ** -0.5 * LOG2E)).astype(bf16)
    k = mm(C_K, C_V)
    kf_o[...] = k
    kb_o[...] = k.astype(bf16)
    vf_o[...] = mm(C_V, C_GQ)
    vt_o[...] = lax.dot_general(wvt_ref[...], xb, (((1,), (1,)), ((), ())),
                                preferred_element_type=f32).astype(bf16)
    gq_o[...] = mm(C_GQ, C_GK) * (G_DK ** -0.5)
    gk_o[...] = mm(C_GK, C_GV)
    gv_o[...] = mm(C_GV, C_GR)
    gr_o[...] = mm(C_GR, C_ZA)
    za_o[...] = mm(C_ZA, C_ZB).astype(bf16)
    zb_o[...] = mm(C_ZB, C_GLR).astype(bf16)
    glr_o[...] = mm(C_GLR, C_END)


def _inproj(x, w, wvt, layer, depth, kv_prev):
    t = x.shape[0]
    tm = _row_tile(t, (384, 256, 128))
    nb = t // tm
    outs = [(512, bf16), (512, f32), (512, bf16), (512, f32), None, (256, f32), (256, f32),
            (512, f32), (512, f32), (1024, bf16), (1024, bf16), (LANES, f32)]
    shared = (1, 3)
    out_specs, out_shape = [], []
    for n, o in enumerate(outs):
        if o is None:
            out_specs.append(pl.BlockSpec((A_WIDTH, tm), lambda i: (0, i)))
            out_shape.append(jax.ShapeDtypeStruct((A_WIDTH, t), bf16))
        elif n in shared:
            out_specs.append(pl.BlockSpec((tm, o[0]), lambda i: (layer * nb + i, 0)))
            out_shape.append(jax.ShapeDtypeStruct((depth * t, o[0]), o[1]))
        else:
            out_specs.append(pl.BlockSpec((tm, o[0]), lambda i: (i, 0)))
            out_shape.append(jax.ShapeDtypeStruct((t, o[0]), o[1]))
    return pl.pallas_call(
        _inproj_body,
        grid=(nb,),
        in_specs=[pl.BlockSpec((tm, D_MODEL), lambda i: (i, 0)),
                  pl.BlockSpec((D_MODEL, C_END), lambda i: (0, 0)),
                  pl.BlockSpec((A_WIDTH, D_MODEL), lambda i: (0, 0))]
        + [pl.BlockSpec(memory_space=pl.ANY)] * len(kv_prev),
        out_specs=out_specs,
        out_shape=out_shape,
        input_output_aliases={3 + n: pos for n, pos in enumerate(shared[:len(kv_prev)])},
        compiler_params=_params(("parallel",)),
    )(x, w, wvt, *kv_prev)


def _attn_body(qi_tab, kj_tab, diag_tab, last_tab, q_ref, k_ref, vt_ref, lam_ref, subg_ref, o_ref,
               m, a, *, tq, tk, lam_init):
    p = pl.program_id(2)
    qi = qi_tab[p]
    kj = kj_tab[p]

    @pl.when(kj == 0)
    def _():
        m[...] = jnp.full(m.shape, NEG_BIG, f32)
        a[...] = jnp.zeros(a.shape, f32)

    q = q_ref[...]
    k = k_ref[...]
    vt = jnp.concatenate([vt_ref[...], jnp.ones((ONES_ROWS, tk), bf16)], axis=0)
    lane = lax.broadcasted_iota(i32, (1, LANES), 1)
    zero = jnp.zeros_like(q)
    qq = jnp.concatenate([jnp.where(lane < A_HEAD_DIM, q, zero), jnp.where(lane >= A_HEAD_DIM, q, zero)],
                         axis=0)

    def step(masked):
        s = lax.dot_general(k, qq, (((1,), (1,)), ((), ())), preferred_element_type=f32)
        if masked:
            kpos = kj * tk + lax.broadcasted_iota(i32, (tk, 2 * tq), 0)
            qpos = qi * tq + (lax.broadcasted_iota(i32, (tk, 2 * tq), 1) & (tq - 1))
            s = jnp.where(kpos <= qpos, s, NEG_BIG)
        m_prev = m[...]
        m_new = jnp.maximum(m_prev, jnp.max(s, axis=0, keepdims=True))
        alpha = jnp.exp2(m_prev - m_new)
        pr = jnp.exp2(s - m_new).astype(bf16)
        a[...] = alpha * a[...] + jnp.dot(vt, pr, preferred_element_type=f32)
        m[...] = m_new

    def update(cols, s, vt_part):
        m_prev = m[:, cols]
        m_new = jnp.maximum(m_prev, jnp.max(s, axis=0, keepdims=True))
        alpha = jnp.exp2(m_prev - m_new)
        pr = jnp.exp2(s - m_new).astype(bf16)
        a[:, cols] = alpha * a[:, cols] + jnp.dot(vt_part, pr, preferred_element_type=f32)
        m[:, cols] = m_new

    def diagonal_square():
        h = tk // 2
        nt = (((1,), (1,)), ((), ()))
        s_lo = lax.dot_general(k[:h], qq, nt, preferred_element_type=f32)
        kpos = lax.broadcasted_iota(i32, (h, 2 * tq), 0)
        qpos = lax.broadcasted_iota(i32, (h, 2 * tq), 1) & (tq - 1)
        update(slice(0, 2 * tq), jnp.where(kpos <= qpos, s_lo, NEG_BIG), vt[:, :h])
        q_hi = jnp.concatenate([qq[h:tq], qq[tq + h:]], axis=0)
        s_hi = lax.dot_general(k[h:], q_hi, nt, preferred_element_type=f32)
        kpos = lax.broadcasted_iota(i32, (h, tq), 0)
        qpos = lax.broadcasted_iota(i32, (h, tq), 1) & (h - 1)
        s_hi = jnp.where(kpos <= qpos, s_hi, NEG_BIG)
        for mp in range(2):
            update(slice(mp * tq + h, (mp + 1) * tq), s_hi[:, mp * h:(mp + 1) * h], vt[:, h:])

    @pl.when(diag_tab[p] == 1)
    def _():
        if tq == tk:
            diagonal_square()
        else:
            step(True)

    @pl.when(diag_tab[p] == 0)
    def _():
        step(False)

    @pl.when(last_tab[p] == 1)
    def _():
        lam = _lam_value(lam_ref, lam_init)
        ot = (a[:A_VDIM, :tq] / a[A_VDIM:A_VDIM + 1, :tq]
              - lam * (a[:A_VDIM, tq:] / a[A_VDIM:A_VDIM + 1, tq:]))
        ms = jnp.mean(ot * ot, axis=0, keepdims=True)
        on = ot * lax.rsqrt(ms + RMS_EPS) * subg_ref[...] * (1.0 - lam_init)
        o_ref[...] = on.T.astype(o_ref.dtype)


def _attn_prompt(q, k, vt, lam_vecs, sub_g_col, batch, seq, lam_init):
    tq = min(ATTN_TQ, seq)
    tk = min(ATTN_TK, seq)
    nq, nk = seq // tq, seq // tk
    qi_l, kj_l, dg_l, ls_l = [], [], [], []
    for qi in range(nq):
        last = ((qi + 1) * tq - 1) // tk
        for kj in range(last + 1):
            qi_l.append(qi)
            kj_l.append(kj)
            dg_l.append(1 if (kj + 1) * tk - 1 > qi * tq else 0)
            ls_l.append(1 if kj == last else 0)
    tabs = [jnp.asarray(t, i32) for t in (qi_l, kj_l, dg_l, ls_l)]
    n_pairs = len(qi_l)
    body = functools.partial(_attn_body, tq=tq, tk=tk, lam_init=lam_init)
    grid_spec = pltpu.PrefetchScalarGridSpec(
        num_scalar_prefetch=4,
        grid=(batch, A_HEADS, n_pairs),
        in_specs=[
            pl.BlockSpec((tq, LANES), lambda b, h, p, qt, kt, dt, lt: (b * nq + qt[p], h)),
            pl.BlockSpec((tk, LANES), lambda b, h, p, qt, kt, dt, lt: (b * nk + kt[p], h)),
            pl.BlockSpec((A_VDIM, tk), lambda b, h, p, qt, kt, dt, lt: (h, b * nk + kt[p])),
            pl.BlockSpec((4, A_HEAD_DIM), lambda b, h, p, *_: (0, 0)),
            pl.BlockSpec((A_VDIM, 1), lambda b, h, p, *_: (0, 0)),
        ],
        out_specs=pl.BlockSpec((tq, LANES), lambda b, h, p, qt, kt, dt, lt: (b * nq + qt[p], h)),
        scratch_shapes=[pltpu.VMEM((1, 2 * tq), f32), pltpu.VMEM((A_VDIM + ONES_ROWS, 2 * tq), f32)],
    )
    return pl.pallas_call(
        body,
        grid_spec=grid_spec,
        out_shape=jax.ShapeDtypeStruct((batch * seq, A_WIDTH), bf16),
        compiler_params=_params(("parallel", "parallel", "arbitrary")),
    )(*tabs, q, k, vt, lam_vecs, sub_g_col)


def _decode_body(pt_ref, q_ref, kn_ref, vn_ref, lam_ref, subg_ref, *refs, n_pages, lam_init):
    k_refs = refs[:n_pages]
    v_refs = refs[n_pages:2 * n_pages]
    o_ref = refs[2 * n_pages]
    kbuf, vbuf = refs[2 * n_pages + 1:]
    rows_pg = PAGE_SIZE * A_HEADS
    past = n_pages * rows_pg
    tail = 16
    n_col = past + tail

    for p in range(n_pages):
        kbuf[p * rows_pg:(p + 1) * rows_pg, :] = k_refs[p][...].astype(bf16)
        vbuf[p * rows_pg:(p + 1) * rows_pg, :] = v_refs[p][...].astype(bf16)
    kbuf[past:, :] = kn_ref[0].astype(bf16)
    vbuf[past:, :] = vn_ref[0].astype(bf16)

    qrow = q_ref[0].astype(f32)
    row = lax.broadcasted_iota(i32, (16, LANES), 0)
    lane = lax.broadcasted_iota(i32, (16, LANES), 1)
    qmat = jnp.zeros((16, LANES), f32)
    for h in range(A_HEADS):
        qh = jnp.broadcast_to(qrow[:, h * LANES:(h + 1) * LANES], (16, LANES))
        sel = ((row >> 1) == h) & ((lane >= A_HEAD_DIM) == ((row & 1) == 1))
        qmat = jnp.where(sel, qh, qmat)
    s = lax.dot_general(qmat.astype(bf16), kbuf[...], (((1,), (1,)), ((), ())), preferred_element_type=f32)
    srow = lax.broadcasted_iota(i32, (16, n_col), 0)
    scol = lax.broadcasted_iota(i32, (16, n_col), 1)
    valid = ((scol & (A_HEADS - 1)) == (srow >> 1)) & (scol < past + A_HEADS) & (srow < 2 * A_HEADS)
    s = jnp.where(valid, s, NEG_BIG)
    m = jnp.max(s, axis=-1, keepdims=True)
    pr = jnp.where(valid, jnp.exp2(s - m), 0.0)
    den = jnp.maximum(jnp.sum(pr, axis=-1, keepdims=True), 1e-30)
    pn = pr / den
    o8 = jnp.dot(pn.astype(bf16), vbuf[...], preferred_element_type=f32)
    lam = _lam_value(lam_ref, lam_init)
    for h in range(A_HEADS):
        o = o8[2 * h:2 * h + 1, :] - lam * o8[2 * h + 1:2 * h + 2, :]
        o_ref[0, :, h * LANES:(h + 1) * LANES] = (
            _rms_norm(o, subg_ref[...]) * (1.0 - lam_init)).astype(o_ref.dtype)


def _attn_decode(q_s, k_new, v_new, lam_vecs, sub_g, cache_k4, cache_v4, page_table, layer, lam_init):
    bd, n_pages = page_table.shape
    rows_pg = PAGE_SIZE * A_HEADS
    body = functools.partial(_decode_body, n_pages=n_pages, lam_init=lam_init)

    def page_spec(p):
        return pl.BlockSpec((None, None, rows_pg, LANES),
                            lambda b, pt, p=p: (layer, pt[b * n_pages + p], 0, 0))

    grid_spec = pltpu.PrefetchScalarGridSpec(
        num_scalar_prefetch=1,
        grid=(bd,),
        in_specs=[pl.BlockSpec((1, 1, A_WIDTH), lambda b, pt: (b, 0, 0)),
                  pl.BlockSpec((1, 16, LANES), lambda b, pt: (b, 0, 0)),
                  pl.BlockSpec((1, 16, LANES), lambda b, pt: (b, 0, 0)),
                  pl.BlockSpec((4, A_HEAD_DIM), lambda b, pt: (0, 0)),
                  pl.BlockSpec((1, A_VDIM), lambda b, pt: (0, 0))]
        + [page_spec(p) for p in range(n_pages)] * 2,
        out_specs=pl.BlockSpec((1, 1, A_WIDTH), lambda b, pt: (b, 0, 0)),
        scratch_shapes=[pltpu.VMEM((n_pages * rows_pg + 16, LANES), bf16),
                        pltpu.VMEM((n_pages * rows_pg + 16, LANES), bf16)],
    )
    return pl.pallas_call(
        body,
        grid_spec=grid_spec,
        out_shape=jax.ShapeDtypeStruct((bd, 1, A_WIDTH), bf16),
        compiler_params=_params(("arbitrary",)),
    )(page_table.reshape(-1), q_s, k_new, v_new, lam_vecs, sub_g,
      *([cache_k4] * n_pages), *([cache_v4] * n_pages))


def _log_decay(glr, wa2, ba):
    z = jnp.dot(glr, wa2, preferred_element_type=f32, precision=lax.Precision.HIGHEST) + ba
    return (jnp.minimum(z, 0.0) - jnp.log(1.0 + jnp.exp(-jnp.abs(z)))) * (1.0 / G_TAU)


def _gla_finish(o, gr, g):
    return _rms_norm(o, g) * (gr * _sigmoid(gr))


def _gla_prompt_body(q_ref, k_ref, v_ref, glr_ref, gr_ref, wa2_ref, ba_ref, g_ref, o_ref, s_ref, state, tmp,
                     *, c):
    ci = pl.program_id(1)
    nsub = c // GLA_SUB

    @pl.when(ci == 0)
    def _():
        state[...] = jnp.zeros(state.shape, f32)

    la = _log_decay(glr_ref[...], wa2_ref[...], ba_ref[...])
    ri = lax.broadcasted_iota(i32, (c, c), 0)
    cj = lax.broadcasted_iota(i32, (c, c), 1)
    b = jnp.dot((ri >= cj).astype(f32), la, preferred_element_type=f32, precision=lax.Precision.HIGHEST)
    b_last = b[c - 1:c, :]
    q = q_ref[...]
    k = k_ref[...]
    v = v_ref[...]
    vb = v.astype(bf16)
    lane = lax.broadcasted_iota(i32, (1, G_K_WIDTH), 1)
    heads = [(lane >> 6) == h for h in range(G_HEADS)]

    def stack_heads(x):
        return jnp.concatenate([jnp.where(m, x, 0.0) for m in heads], axis=0).astype(bf16)

    sub_i = lax.broadcasted_iota(i32, (GLA_SUB, G_K_WIDTH), 0)
    for blk in range(nsub):
        r0 = blk * GLA_SUB
        q_b = q[r0:r0 + GLA_SUB]
        b_b = b[r0:r0 + GLA_SUB]
        for j in range(GLA_SUB):
            w = jnp.where(sub_i >= j, jnp.exp(jnp.minimum(b_b - b_b[j:j + 1], 0.0)), 0.0)
            t0 = (r0 + j) * GLA_SUB
            tmp[t0:t0 + GLA_SUB, :] = (q_b * w * k[r0 + j:r0 + j + 1]).astype(bf16)
    si = lax.broadcasted_iota(i32, (G_K_WIDTH, G_V_WIDTH), 0)
    sj = lax.broadcasted_iota(i32, (G_K_WIDTH, G_V_WIDTH), 1)
    seg = ((si >> 6) == (sj >> 7)).astype(bf16)
    pair = jnp.dot(tmp[...], seg, preferred_element_type=f32)
    o_rows = []
    for blk in range(nsub):
        r0 = blk * GLA_SUB
        acc = jnp.zeros((GLA_SUB, G_V_WIDTH), f32)
        for j in range(GLA_SUB):
            t0 = (r0 + j) * GLA_SUB
            acc = acc + pair[t0:t0 + GLA_SUB, :] * v[r0 + j:r0 + j + 1]
        o_rows.append(acc)
    o_diag = jnp.concatenate(o_rows, axis=0)

    att_rows = [[jnp.zeros((GLA_SUB, c), f32)] for _ in range(G_HEADS)]
    col = lax.broadcasted_iota(i32, (G_HEADS * GLA_SUB, c), 1)
    for blk in range(1, nsub):
        r0 = blk * GLA_SUB
        ref = b[r0 - 1:r0]
        q_b = q[r0:r0 + GLA_SUB] * jnp.exp(b[r0:r0 + GLA_SUB] - ref)
        k_b = (k * jnp.exp(jnp.minimum(ref - b, 0.0))).astype(bf16)
        a = lax.dot_general(stack_heads(q_b), k_b, (((1,), (1,)), ((), ())), preferred_element_type=f32)
        a = jnp.where(col < r0, a, 0.0)
        for h in range(G_HEADS):
            att_rows[h].append(a[h * GLA_SUB:(h + 1) * GLA_SUB])

    st = state[...]
    o_inter = jnp.dot(stack_heads(q * jnp.exp(b)), st.astype(bf16), preferred_element_type=f32)
    kdt = (k * jnp.exp(b_last - b)).T.astype(bf16)
    upd = jnp.dot(kdt, vb, preferred_element_type=f32)
    dec_col = jnp.exp(jnp.broadcast_to(b_last, (8, G_K_WIDTH)).T[:, 0:1])
    for h in range(G_HEADS):
        cols = slice(h * G_DV, (h + 1) * G_DV)
        att = jnp.concatenate(att_rows[h], axis=0).astype(bf16)
        o = (o_diag[:, cols] + jnp.dot(att, vb[:, cols], preferred_element_type=f32)
             + o_inter[h * c:(h + 1) * c])
        o_ref[:, cols] = _gla_finish(o, gr_ref[:, cols], g_ref[...]).astype(o_ref.dtype)
        r0 = h * G_DK
        state[r0:r0 + G_DK, :] = dec_col[r0:r0 + G_DK] * st[r0:r0 + G_DK, :] + upd[r0:r0 + G_DK, cols]

    @pl.when(ci == pl.num_programs(1) - 1)
    def _():
        s_ref[0] = state[...]


def _gla_prompt(gq, gk, gv, glr, gr, wa2, ba, gla_g, batch, seq):
    c = math.gcd(seq, GLA_CHUNK)
    n = seq // c
    body = functools.partial(_gla_prompt_body, c=c)

    def tok(width):
        return pl.BlockSpec((c, width), lambda b, i: (b * n + i, 0))

    def whole(shape):
        return pl.BlockSpec(shape, lambda b, i: (0,) * len(shape))

    return pl.pallas_call(
        body,
        grid=(batch, n),
        in_specs=[tok(G_K_WIDTH), tok(G_K_WIDTH), tok(G_V_WIDTH), tok(LANES), tok(G_V_WIDTH),
                  whole((LANES, G_K_WIDTH)), whole((1, G_K_WIDTH)), whole((1, G_DV))],
        out_specs=[tok(G_V_WIDTH),
                   pl.BlockSpec((1, G_K_WIDTH, G_DV), lambda b, i: (b, 0, 0))],
        out_shape=[jax.ShapeDtypeStruct((batch * seq, G_V_WIDTH), bf16),
                   jax.ShapeDtypeStruct((batch, G_K_WIDTH, G_DV), f32)],
        scratch_shapes=[pltpu.VMEM((G_K_WIDTH, G_DV), f32),
                        pltpu.VMEM((c * GLA_SUB, G_K_WIDTH), bf16)],
        compiler_params=_params(("parallel", "arbitrary")),
    )(gq, gk, gv, glr, gr, wa2, ba, gla_g)


def _gla_sample_body(q_ref, k_ref, v_ref, glr_ref, gr_ref, s_ref, wa2_ref, ba_ref, g_ref, o_ref, sn_ref,
                     *, group):
    ri = lax.broadcasted_iota(i32, (3 * G_DK, G_K_WIDTH), 0)
    li = lax.broadcasted_iota(i32, (3 * G_DK, G_K_WIDTH), 1)
    diag = (ri & (G_DK - 1)) == (li & (G_DK - 1))
    si = lax.broadcasted_iota(i32, (G_K_WIDTH, G_V_WIDTH), 0)
    sj = lax.broadcasted_iota(i32, (G_K_WIDTH, G_V_WIDTH), 1)
    seg = ((si >> 6) == (sj >> 7)).astype(f32)
    for n in range(group):
        la = _log_decay(glr_ref[n], wa2_ref[...], ba_ref[...])
        rows = jnp.concatenate([jnp.broadcast_to(jnp.exp(la), (G_DK, G_K_WIDTH)),
                                jnp.broadcast_to(k_ref[n], (G_DK, G_K_WIDTH)),
                                jnp.broadcast_to(q_ref[n], (G_DK, G_K_WIDTH))], axis=0)
        picked = jnp.where(diag, rows, 0.0)
        cols = jnp.dot(picked, seg, preferred_element_type=f32, precision=lax.Precision.HIGHEST)
        v = v_ref[n]
        for h in range(G_HEADS):
            sl = slice(h * G_DV, (h + 1) * G_DV)
            a_c = cols[0:G_DK, sl]
            k_c = cols[G_DK:2 * G_DK, sl]
            q_c = cols[2 * G_DK:3 * G_DK, sl]
            s_new = a_c * s_ref[n, h] + k_c * v[:, sl]
            sn_ref[n, h] = s_new
            o = jnp.sum(q_c * s_new, axis=0, keepdims=True)
            o_ref[n, :, sl] = _gla_finish(o, gr_ref[n][:, sl], g_ref[...]).astype(o_ref.dtype)


def _gla_sample(gq, gk, gv, glr, gr, s0, wa2, ba, gla_g):
    bd = gq.shape[0]
    group = math.gcd(bd, SUBLANES)

    def vec(width):
        return pl.BlockSpec((group, 1, width), lambda b: (b, 0, 0))

    def whole(shape):
        return pl.BlockSpec(shape, lambda b: (0,) * len(shape))

    st = pl.BlockSpec((group, G_HEADS, G_DK, G_DV), lambda b: (b, 0, 0, 0))
    return pl.pallas_call(
        functools.partial(_gla_sample_body, group=group),
        grid=(bd // group,),
        in_specs=[vec(G_K_WIDTH), vec(G_K_WIDTH), vec(G_V_WIDTH), vec(LANES), vec(G_V_WIDTH), st,
                  whole((LANES, G_K_WIDTH)), whole((1, G_K_WIDTH)), whole((1, G_DV))],
        out_specs=[vec(G_V_WIDTH), st],
        out_shape=[jax.ShapeDtypeStruct((bd, 1, G_V_WIDTH), bf16),
                   jax.ShapeDtypeStruct((bd, G_HEADS, G_DK, G_DV), f32)],
        compiler_params=_params(("parallel",)),
    )(gq, gk, gv, glr, gr, s0, wa2, ba, gla_g)


def _postmix_body(oa_ref, og_ref, za_ref, zb_ref, x_ref, wpa_ref, wpb_ref, wout_ref, g_ref, b_ref,
                  x1_o, xp_o, *, alpha):
    ya = jnp.dot(oa_ref[...], wpa_ref[...], preferred_element_type=f32)
    yb = jnp.dot(og_ref[...], wpb_ref[...], preferred_element_type=f32)
    merged = _sigmoid(za_ref[...].astype(f32)) * ya + _sigmoid(zb_ref[...].astype(f32)) * yb
    mix = jnp.dot(merged.astype(bf16), wout_ref[...], preferred_element_type=f32)
    x1 = _layer_norm(alpha * x_ref[...] + mix, g_ref[...], b_ref[...])
    x1_o[...] = x1
    xp_o[...] = _pack_rows(x1)


def _postmix(oa, og, za, zb, x, wpa, wpb, wout, g, b, alpha):
    t = x.shape[0]
    tm = _row_tile(t, (384, 256, 128, 64, 32, 16, 8))

    def tok(width):
        return pl.BlockSpec((tm, width), lambda i: (i, 0))

    def whole(shape):
        return pl.BlockSpec(shape, lambda i: (0,) * len(shape))

    return pl.pallas_call(
        functools.partial(_postmix_body, alpha=alpha),
        grid=(t // tm,),
        in_specs=[tok(A_WIDTH), tok(G_V_WIDTH), tok(D_MODEL), tok(D_MODEL), tok(D_MODEL),
                  whole((A_WIDTH, D_MODEL)), whole((G_V_WIDTH, D_MODEL)), whole((D_MODEL, D_MODEL)),
                  whole((1, D_MODEL)), whole((1, D_MODEL))],
        out_specs=[tok(D_MODEL), tok(HALF)],
        out_shape=[jax.ShapeDtypeStruct((t, D_MODEL), f32),
                   jax.ShapeDtypeStruct((t, HALF), u32)],
        compiler_params=_params(("parallel",)),
    )(oa, og, za, zb, x, wpa, wpb, wout, g, b)


def _router_body(x1_ref, wrt_ref, bcol_ref, tri_ref, eidx_o, gate_o, rank_o, cnt_o, cnt, *, tm):
    @pl.when(pl.program_id(0) == 0)
    def _():
        cnt[...] = jnp.zeros(cnt.shape, f32)

    logits = lax.dot_general(wrt_ref[...], x1_ref[...], (((1,), (1,)), ((), ())),
                             preferred_element_type=f32, precision=lax.Precision.HIGHEST)
    scores = _sigmoid(logits)
    biased = scores + bcol_ref[...]
    gsz = N_EXPERTS // N_GROUPS
    neg_inf = -jnp.inf

    gi = lax.broadcasted_iota(i32, (gsz, tm), 0)
    segs, gscore = [], []
    for g in range(N_GROUPS):
        seg = biased[g * gsz:(g + 1) * gsz, :]
        m1 = jnp.max(seg, axis=0, keepdims=True)
        i1 = jnp.min(jnp.where(seg == m1, gi, gsz), axis=0, keepdims=True)
        m2 = jnp.max(jnp.where(gi == i1, neg_inf, seg), axis=0, keepdims=True)
        segs.append(seg)
        gscore.append(m1 + m2)
    parts = []
    for g in range(N_GROUPS):
        beat = jnp.zeros((1, tm), i32)
        for o in range(N_GROUPS):
            if o != g:
                wins = (gscore[o] > gscore[g]) | ((gscore[o] == gscore[g]) & (o < g))
                beat = beat + wins.astype(i32)
        parts.append(jnp.where(beat < TOPK_GROUPS, segs[g], neg_inf))
    masked = jnp.concatenate(parts, axis=0)

    ei = lax.broadcasted_iota(i32, (N_EXPERTS, tm), 0)
    sel_rows, idx_rows = [], []
    chosen = jnp.zeros((N_EXPERTS, tm), f32)
    for _ in range(TOP_K):
        m = jnp.max(masked, axis=0, keepdims=True)
        idx = jnp.min(jnp.where(masked == m, ei, N_EXPERTS), axis=0, keepdims=True)
        hit = ei == idx
        sel_rows.append(jnp.sum(jnp.where(hit, scores, 0.0), axis=0, keepdims=True))
        idx_rows.append(idx)
        chosen = jnp.where(hit, 1.0, chosen)
        masked = jnp.where(hit, neg_inf, masked)
    s_sel = jnp.concatenate(sel_rows, axis=0)
    gate_o[...] = s_sel / jnp.sum(s_sel, axis=0, keepdims=True) * ROUTED_SCALE
    eidx_o[...] = jnp.concatenate(idx_rows, axis=0)
    before = jnp.dot(chosen.astype(bf16), tri_ref[...], preferred_element_type=f32) + cnt[...]
    rank_rows = [jnp.sum(jnp.where(ei == idx, before, 0.0), axis=0, keepdims=True) for idx in idx_rows]
    rank_o[...] = jnp.concatenate(rank_rows, axis=0).astype(i32)
    cnt[...] = cnt[...] + jnp.sum(chosen, axis=1, keepdims=True)
    cnt_o[...] = cnt[...]


def _router(x1, wrt, b_col):
    t = x1.shape[0]
    tm = _row_tile(t, (384, 256, 128))
    tri = (jnp.arange(tm)[:, None] < jnp.arange(tm)[None, :]).astype(bf16)

    def tokcol(dt):
        return pl.BlockSpec((TOP_K, tm), lambda i: (0, i)), jax.ShapeDtypeStruct((TOP_K, t), dt)

    def whole(shape):
        return pl.BlockSpec(shape, lambda i: (0,) * len(shape))

    specs, shapes = zip(tokcol(i32), tokcol(f32), tokcol(i32),
                        (whole((N_EXPERTS, 1)), jax.ShapeDtypeStruct((N_EXPERTS, 1), f32)))
    return pl.pallas_call(
        functools.partial(_router_body, tm=tm),
        grid=(t // tm,),
        in_specs=[pl.BlockSpec((tm, D_MODEL), lambda i: (i, 0)), whole((N_EXPERTS, D_MODEL)),
                  whole((N_EXPERTS, 1)), whole((tm, tm))],
        out_specs=list(specs),
        out_shape=list(shapes),
        scratch_shapes=[pltpu.VMEM((N_EXPERTS, 1), f32)],
        compiler_params=_params(("arbitrary",)),
    )(x1, wrt, b_col, tri)


def _dispatch_body(dest_ref, pad0_ref, pad1_ref, x_ref, xs_out, zbuf, sem, zsem, *, tm):

    @pl.when(pl.program_id(0) == 0)
    def _():
        zbuf[...] = jnp.zeros(zbuf.shape, u32)

        def zero_row(row):
            return pltpu.make_async_copy(zbuf.at[pl.ds(0, 1)], xs_out.at[pl.ds(row, 1)], zsem)

        def zero_group(row):
            return pltpu.make_async_copy(zbuf, xs_out.at[pl.ds(pl.multiple_of(row, SUBLANES), SUBLANES)], zsem)

        def per_expert(e, waiting):
            p0 = pad0_ref[e]
            p1 = pad1_ref[e]
            head = jnp.minimum((-p0) & (SUBLANES - 1), p1 - p0)
            groups = lax.shift_right_logical(p1 - p0 - head, 3)

            def rows(r, c):
                if waiting:
                    zero_row(0).wait()
                else:
                    zero_row(p0 + r).start()
                return c

            def grps(g, c):
                if waiting:
                    zero_group(0).wait()
                else:
                    zero_group(p0 + head + g * SUBLANES).start()
                return c

            lax.fori_loop(0, head, rows, 0)
            lax.fori_loop(0, groups, grps, 0)

        lax.fori_loop(0, N_EXPERTS, lambda e, c: (per_expert(e, False), c)[1], 0)
        lax.fori_loop(0, N_EXPERTS, lambda e, c: (per_expert(e, True), c)[1], 0)

    def copy(grp, sub, dst_row):
        return pltpu.make_async_copy(x_ref.at[grp, pl.ds(sub, 1)], xs_out.at[pl.ds(dst_row, 1)], sem)

    def issue(grp, carry):
        for sub in range(SUBLANES):
            for kk in range(TOP_K):
                copy(grp, sub, dest_ref[(grp * SUBLANES + sub) * TOP_K + kk]).start(priority=kk % 2)
        return carry

    lax.fori_loop(0, tm // SUBLANES, issue, 0)

    def drain(grp, carry):
        for _ in range(SUBLANES * TOP_K):
            copy(0, 0, 0).wait()
        return carry

    lax.fori_loop(0, tm // SUBLANES, drain, 0)


def _dispatch(xp, dest, pad0, pad1, n_rows):
    t = xp.shape[0]
    tm = _row_tile(t, (384, 256, 128, 64, 32, 16, 8))
    whole = pl.BlockSpec((N_EXPERTS,), lambda i: (0,), memory_space=pltpu.SMEM)
    return pl.pallas_call(
        functools.partial(_dispatch_body, tm=tm),
        grid=(t // tm,),
        in_specs=[pl.BlockSpec((tm * TOP_K,), lambda i: (i,), memory_space=pltpu.SMEM), whole, whole,
                  pl.BlockSpec((tm // SUBLANES, SUBLANES, HALF), lambda i: (i, 0, 0))],
        out_specs=pl.BlockSpec(memory_space=pl.ANY),
        out_shape=jax.ShapeDtypeStruct((n_rows, HALF), u32),
        scratch_shapes=[pltpu.VMEM((SUBLANES, HALF), u32), pltpu.SemaphoreType.DMA(()),
                        pltpu.SemaphoreType.DMA(())],
        compiler_params=_params(("arbitrary",)),
    )(dest, pad0, pad1, xp.reshape(t // SUBLANES, SUBLANES, HALF))


def _expert_weight_copies(seq_ref, w_hbm, w_buf, sems, seq_idx, *, layer):
    e = seq_ref[seq_idx]
    slot = lax.rem(seq_idx, WEIGHT_SLOTS)
    return [pltpu.make_async_copy(w_hbm[n].at[layer, e], w_buf[n].at[slot], sems.at[slot, n]) for n in range(3)]


def _expert_body(bs_ref, seq_ref, ns_ref, nu_ref, xs_ref, wg_hbm, wu_hbm, wd_hbm, ys_ref,
                 wgf, wuf, wdf, sems, wgb, wub, wdb, *, layer):
    i = pl.program_id(0)
    j = bs_ref[i]
    copies = functools.partial(_expert_weight_copies, seq_ref, (wg_hbm, wu_hbm, wd_hbm), (wgf, wuf, wdf),
                               sems, layer=layer)

    @pl.when(i == 0)
    def _():
        for c in copies(0):
            c.start()

        @pl.when(ns_ref[0] > 1)
        def _():
            for c in copies(1):
                c.start()

    @pl.when((i == 0) | (j != bs_ref[jnp.maximum(i - 1, 0)]))
    def _():
        for c in copies(j):
            c.wait()
        slot = lax.rem(j, WEIGHT_SLOTS)
        wgb[...] = wgf[slot].astype(bf16)
        wub[...] = wuf[slot].astype(bf16)
        wdb[...] = wdf[slot].astype(bf16)

        @pl.when(j + 2 < ns_ref[0])
        def _():
            for c in copies(j + 2):
                c.start()

    @pl.when(i < nu_ref[0])
    def _():
        lo, hi = _unpack_rows(xs_ref[...])
        lo = lo.astype(bf16)
        hi = hi.astype(bf16)
        g = (jnp.dot(lo, wgb[:HALF, :], preferred_element_type=f32)
             + jnp.dot(hi, wgb[HALF:, :], preferred_element_type=f32))
        u = (jnp.dot(lo, wub[:HALF, :], preferred_element_type=f32)
             + jnp.dot(hi, wub[HALF:, :], preferred_element_type=f32))
        hdn = (g * _sigmoid(g) * u).astype(bf16)
        ys_ref[...] = _pack_rows(jnp.dot(hdn, wdb[...], preferred_element_type=f32))

    @pl.when(i >= nu_ref[0])
    def _():
        ys_ref[...] = jnp.zeros(ys_ref.shape, u32)


def _experts(xs, blk_seq, seq_exp, n_seq, n_used, w_gate, w_up, w_down, layer):
    n_rows = xs.shape[0]
    n_blocks = n_rows // EXPERT_BLOCK
    hbm = pl.BlockSpec(memory_space=pl.ANY)
    grid_spec = pltpu.PrefetchScalarGridSpec(
        num_scalar_prefetch=4,
        grid=(n_blocks,),
        in_specs=[pl.BlockSpec((EXPERT_BLOCK, HALF), lambda i, bs, sq, ns, nu: (jnp.minimum(i, nu[0] - 1), 0)),
                  hbm, hbm, hbm],
        out_specs=pl.BlockSpec((EXPERT_BLOCK, HALF), lambda i, bs, sq, ns, nu: (i, 0)),
        scratch_shapes=[pltpu.VMEM((WEIGHT_SLOTS, D_MODEL, D_EXPERT), f32),
                        pltpu.VMEM((WEIGHT_SLOTS, D_MODEL, D_EXPERT), f32),
                        pltpu.VMEM((WEIGHT_SLOTS, D_EXPERT, D_MODEL), f32),
                        pltpu.SemaphoreType.DMA((WEIGHT_SLOTS, 3)),
                        pltpu.VMEM((D_MODEL, D_EXPERT), bf16), pltpu.VMEM((D_MODEL, D_EXPERT), bf16),
                        pltpu.VMEM((D_EXPERT, D_MODEL), bf16)],
    )
    return pl.pallas_call(
        functools.partial(_expert_body, layer=layer),
        grid_spec=grid_spec,
        out_shape=jax.ShapeDtypeStruct((n_rows, HALF), u32),
        compiler_params=_params(("arbitrary",)),
    )(blk_seq, seq_exp, n_seq, n_used, xs, w_gate, w_up, w_down)


def _combine_body(dest_ref, gate_ref, x1_ref, xp_ref, ys_hbm, wsg_ref, wsu_ref, wsd_ref, g_ref, b_ref,
                  o_ref, buf, sem, *, tm, alpha):
    def copy(src_row, kk, grp, sub):
        return pltpu.make_async_copy(ys_hbm.at[pl.ds(src_row, 1)], buf.at[kk, grp, pl.ds(sub, 1)], sem)

    def issue(grp, carry):
        for sub in range(SUBLANES):
            for kk in range(TOP_K):
                copy(dest_ref[(grp * SUBLANES + sub) * TOP_K + kk], kk, grp, sub).start(priority=kk % 2)
        return carry

    lax.fori_loop(0, tm // SUBLANES, issue, 0)

    lo, hi = _unpack_rows(xp_ref[...])
    lo = lo.astype(bf16)
    hi = hi.astype(bf16)
    sg = (jnp.dot(lo, wsg_ref[:HALF, :], preferred_element_type=f32)
          + jnp.dot(hi, wsg_ref[HALF:, :], preferred_element_type=f32))
    su = (jnp.dot(lo, wsu_ref[:HALF, :], preferred_element_type=f32)
          + jnp.dot(hi, wsu_ref[HALF:, :], preferred_element_type=f32))
    shared = jnp.dot((sg * _sigmoid(sg) * su).astype(bf16), wsd_ref[...], preferred_element_type=f32)

    def drain(grp, carry):
        for _ in range(SUBLANES * TOP_K):
            copy(0, 0, 0, 0).wait()
        return carry

    lax.fori_loop(0, tm // SUBLANES, drain, 0)

    gates = gate_ref[...]
    acc_lo = jnp.zeros((tm, HALF), f32)
    acc_hi = jnp.zeros((tm, HALF), f32)
    for kk in range(TOP_K):
        ylo, yhi = _unpack_rows(buf[kk].reshape(tm, HALF))
        gk = gates[:, kk:kk + 1]
        acc_lo = acc_lo + gk * ylo
        acc_hi = acc_hi + gk * yhi
    moe = jnp.concatenate([acc_lo, acc_hi], axis=-1) + shared
    o_ref[...] = _layer_norm(alpha * x1_ref[...] + moe, g_ref[...], b_ref[...])


def _combine(dest, gates, x1, xp, ys, wsg, wsu, wsd, g, b, alpha):
    t = x1.shape[0]
    tm = _row_tile(t, (384, 256, 128, 64, 32, 16, 8))

    def tok(width):
        return pl.BlockSpec((tm, width), lambda i: (i, 0))

    def whole(shape):
        return pl.BlockSpec(shape, lambda i: (0,) * len(shape))

    return pl.pallas_call(
        functools.partial(_combine_body, tm=tm, alpha=alpha),
        grid=(t // tm,),
        in_specs=[pl.BlockSpec((tm * TOP_K,), lambda i: (i,), memory_space=pltpu.SMEM),
                  tok(TOP_K), tok(D_MODEL), tok(HALF),
                  pl.BlockSpec(memory_space=pl.ANY),
                  whole((D_MODEL, D_EXPERT)), whole((D_MODEL, D_EXPERT)), whole((D_EXPERT, D_MODEL)),
                  whole((1, D_MODEL)), whole((1, D_MODEL))],
        out_specs=tok(D_MODEL),
        out_shape=jax.ShapeDtypeStruct((t, D_MODEL), f32),
        scratch_shapes=[pltpu.VMEM((TOP_K, tm // SUBLANES, SUBLANES, HALF), u32),
                        pltpu.SemaphoreType.DMA(())],
        compiler_params=_params(("arbitrary",)),
    )(dest, gates, x1, xp, ys, wsg, wsu, wsd, g, b)


def _dest_body(eidx_ref, rank_ref, ps_ref, dest_o, *, tm):
    ei = lax.broadcasted_iota(i32, (N_EXPERTS, tm), 0)
    ps = ps_ref[...]
    rows = [jnp.sum(jnp.where(ei == eidx_ref[kk:kk + 1, :], ps, 0.0), axis=0, keepdims=True)
            for kk in range(TOP_K)]
    dest_o[...] = jnp.concatenate(rows, axis=0).astype(i32) + rank_ref[...]


def _dest_rows(eidx_t, rank_t, pad_start):
    t = eidx_t.shape[1]
    tm = _row_tile(t, (384, 256, 128))
    blk = pl.BlockSpec((TOP_K, tm), lambda i: (0, i))
    return pl.pallas_call(
        functools.partial(_dest_body, tm=tm),
        grid=(t // tm,),
        in_specs=[blk, blk, pl.BlockSpec((N_EXPERTS, 1), lambda i: (0, 0))],
        out_specs=blk,
        out_shape=jax.ShapeDtypeStruct((TOP_K, t), i32),
        compiler_params=_params(("parallel",)),
    )(eidx_t, rank_t, pad_start.astype(f32).reshape(N_EXPERTS, 1))


def _layout(eidx_t, rank_t, counts):
    t = eidx_t.shape[1]
    blk = EXPERT_BLOCK
    padded = (counts + blk - 1) // blk * blk
    pad_end = jnp.cumsum(padded)
    pad_start = pad_end - padded
    dest = _dest_rows(eidx_t, rank_t, pad_start)
    n_rows = -(-(t * TOP_K + N_EXPERTS * (blk - 1)) // blk) * blk
    n_blocks = n_rows // blk
    n_used = (pad_end[-1] // blk).astype(i32)
    first_row = jnp.minimum(jnp.arange(n_blocks), n_used - 1) * blk
    blk_exp = jnp.sum(pad_end[None, :] <= first_row[:, None], axis=1).astype(i32)
    blk_exp = jnp.minimum(blk_exp, N_EXPERTS - 1)
    used = counts > 0
    seq_of = jnp.cumsum(used.astype(i32)) - 1
    ids = jnp.arange(N_EXPERTS, dtype=i32)
    seq_exp = jnp.sum(jnp.where(used[None, :] & (seq_of[None, :] == ids[:, None]), ids[None, :], 0), axis=1)
    blk_seq = jnp.sum(jnp.where(blk_exp[:, None] == ids[None, :], seq_of[None, :], 0), axis=1)
    n_seq = jnp.sum(used.astype(i32))
    pads = ((pad_start + counts).astype(i32), pad_end.astype(i32))
    return dest, pads, blk_seq.astype(i32), seq_exp.astype(i32), n_seq.reshape(1), n_used.reshape(1), n_rows


def kernel(x_prompt, x_sample, cache_k, cache_v, state_gla, page_table, w_in, w_a2, b_a, lam_q1, lam_k1,
           lam_q2, lam_k2, sub_g, gla_g, w_pa, w_pb, w_out, ln1_g, ln1_b, w_router, b_router, w_gate, w_up,
           w_down, ws_gate, ws_up, ws_down, ln2_g, ln2_b):
    depth = w_in.shape[0]
    batch, seq, _ = x_prompt.shape
    bd = x_sample.shape[0]
    tp = batch * seq
    alpha = (2 * depth) ** 0.25
    n_pool = cache_k.shape[1]
    cache_k4 = cache_k.reshape(depth, n_pool, PAGE_SIZE * A_HEADS, 2 * A_HEAD_DIM)
    cache_v4 = cache_v.reshape(depth, n_pool, PAGE_SIZE * A_HEADS, A_VDIM)

    x = jnp.concatenate([x_prompt.reshape(tp, D_MODEL), x_sample.reshape(bd, D_MODEL)], axis=0)
    t_all = tp + bd
    kv_all = ()
    sp_l, ss_l = [], []
    for l in range(depth):
        lam_init = 0.8 - 0.6 * math.exp(-0.3 * l)
        wl = w_in[l]
        w_re = jnp.concatenate([wl[:, :3072], wl[:, 3088:5136], wl[:, 3072:3088],
                                jnp.zeros((D_MODEL, LANES - G_GATE_RANK), f32)], axis=1).astype(bf16)
        wa2 = jnp.concatenate([w_a2[l], jnp.zeros((LANES - G_GATE_RANK, G_K_WIDTH), f32)], axis=0)
        ba = b_a[l].reshape(1, G_K_WIDTH)
        lam_vecs = jnp.stack([lam_q1[l], lam_k1[l], lam_q2[l], lam_k2[l]]).astype(f32)
        subg = sub_g[l].reshape(1, A_VDIM)
        glag = gla_g[l].reshape(1, G_DV)

        wvt = wl[:, C_V:C_GQ].T.astype(bf16)
        q, kf_all, kb, vf_all, vt, gq, gk, gv, gr, za, zb, glr = _inproj(x, w_re, wvt, l, depth, kv_all)
        kv_all = (kf_all, vf_all)
        kf = kf_all[l * t_all + tp:(l + 1) * t_all]
        vf = vf_all[l * t_all + tp:(l + 1) * t_all]

        oa_p = _attn_prompt(q, kb, vt, lam_vecs, sub_g[l].reshape(A_VDIM, 1), batch, seq, lam_init)
        tail_pad = ((0, 0), (0, 16 - A_HEADS), (0, 0))
        oa_s = _attn_decode(q[tp:].reshape(bd, 1, A_WIDTH),
                            jnp.pad(kf.reshape(bd, A_HEADS, 2 * A_HEAD_DIM), tail_pad),
                            jnp.pad(vf.reshape(bd, A_HEADS, A_VDIM), tail_pad),
                            lam_vecs, subg, cache_k4, cache_v4, page_table, l, lam_init)
        og_p, s_p = _gla_prompt(gq, gk, gv, glr, gr, wa2, ba, glag, batch, seq)
        og_s, s_s = _gla_sample(gq[tp:].reshape(bd, 1, -1), gk[tp:].reshape(bd, 1, -1),
                                gv[tp:].reshape(bd, 1, -1), glr[tp:].reshape(bd, 1, -1),
                                gr[tp:].reshape(bd, 1, -1), state_gla[l], wa2, ba, glag)
        oa = jnp.concatenate([oa_p, oa_s.reshape(bd, A_WIDTH)], axis=0)
        og = jnp.concatenate([og_p, og_s.reshape(bd, G_V_WIDTH)], axis=0)

        x1, xp = _postmix(oa, og, za, zb, x, w_pa[l].astype(bf16), w_pb[l].astype(bf16),
                          w_out[l].astype(bf16), ln1_g[l].reshape(1, -1), ln1_b[l].reshape(1, -1), alpha)
        eidx_t, gates_t, rank_t, counts = _router(x1, w_router[l].T, b_router[l].reshape(N_EXPERTS, 1))
        dest_t, pads, blk_seq, seq_exp, n_seq, n_used, n_rows = _layout(eidx_t, rank_t,
                                                                        counts.reshape(-1).astype(i32))
        dest = dest_t.T.reshape(-1)
        xs = _dispatch(xp, dest, pads[0], pads[1], n_rows)
        ys = _experts(xs, blk_seq, seq_exp, n_seq, n_used, w_gate, w_up, w_down, l)
        x = _combine(dest, gates_t.T, x1, xp, ys, ws_gate[l].astype(bf16), ws_up[l].astype(bf16),
                     ws_down[l].astype(bf16), ln2_g[l].reshape(1, -1), ln2_b[l].reshape(1, -1), alpha)

        sp_l.append(s_p.reshape(batch, G_HEADS, G_DK, G_DV))
        ss_l.append(s_s)

    y_prompt = x[:tp].reshape(batch, seq, D_MODEL)
    y_sample = x[tp:].reshape(bd, 1, D_MODEL)
    k3 = kv_all[0].reshape(depth, t_all, A_WIDTH)
    v3 = kv_all[1].reshape(depth, t_all, A_WIDTH)
    pages = (depth, batch, seq // PAGE_SIZE, PAGE_SIZE, A_HEADS, A_VDIM)
    return (y_prompt, y_sample, k3[:, :tp].reshape(pages), v3[:, :tp].reshape(pages),
            k3[:, tp:].reshape(depth, bd, 1, A_HEADS, 2 * A_HEAD_DIM),
            v3[:, tp:].reshape(depth, bd, 1, A_HEADS, A_VDIM), jnp.stack(sp_l), jnp.stack(ss_l))
```

```python
import functools
import math

import jax
import jax.numpy as jnp
from jax import lax
from jax.experimental import pallas as pl
from jax.experimental.pallas import tpu as pltpu

f32 = jnp.float32
bf16 = jnp.bfloat16
u32 = jnp.uint32
i32 = jnp.int32

D_MODEL = 1024
A_HEADS = 4
A_HEAD_DIM = 64
A_VDIM = 128
A_WIDTH = A_HEADS * A_VDIM
G_HEADS = 4
G_DK = 64
G_DV = 128
G_K_WIDTH = G_HEADS * G_DK
G_V_WIDTH = G_HEADS * G_DV
G_GATE_RANK = 16
G_TAU = 16.0
N_EXPERTS = 256
TOP_K = 8
N_GROUPS = 8
TOPK_GROUPS = 4
D_EXPERT = 256
ROUTED_SCALE = 2.5
PAGE_SIZE = 128
LN_EPS = 1e-5
RMS_EPS = 1e-6

LANES = 128
SUBLANES = 8
VMEM_LIMIT = 56 * 1024 * 1024

NEG_BIG = -1e30
HALF = D_MODEL // 2
EXPERT_BLOCK = 256
WEIGHT_SLOTS = 3
GLA_CHUNK = 64
GLA_SUB = 16
ATTN_TQ = 1024
ATTN_TK = 1024
ONES_ROWS = 16
LOG2E = math.log2(math.e)

C_Q, C_K, C_V, C_GQ, C_GK, C_GV, C_GR, C_ZA, C_ZB, C_GLR, C_END = (
    0, 512, 1024, 1536, 1792, 2048, 2560, 3072, 4096, 5120, 5248)


def _params(sem, vmem=VMEM_LIMIT):
    return pltpu.CompilerParams(dimension_semantics=sem, vmem_limit_bytes=vmem)


def _row_tile(n, cands=(512, 384, 256, 128, 64, 32, 16, 8)):
    for c in cands:
        if n % c == 0:
            return c
    raise ValueError(f"no row tile for {n}")


def _sigmoid(x):
    return 1.0 / (1.0 + jnp.exp(-x))


def _pack_rows(x):
    lo = lax.bitcast_convert_type(x[:, :HALF].astype(bf16).astype(f32), u32) >> 16
    hi = lax.bitcast_convert_type(x[:, HALF:].astype(bf16).astype(f32), u32) & jnp.uint32(0xFFFF0000)
    return lo | hi


def _unpack_rows(w):
    lo = lax.bitcast_convert_type(w << 16, f32)
    hi = lax.bitcast_convert_type(w & jnp.uint32(0xFFFF0000), f32)
    return lo, hi


def _layer_norm(h, g, b):
    mu = jnp.mean(h, axis=-1, keepdims=True)
    d = h - mu
    var = jnp.mean(d * d, axis=-1, keepdims=True)
    return d * lax.rsqrt(var + LN_EPS) * g + b


def _rms_norm(o, g):
    return o * lax.rsqrt(jnp.mean(o * o, axis=-1, keepdims=True) + RMS_EPS) * g


def _lam_value(lam_ref, lam_init):
    l = lam_ref[...]
    s1 = jnp.sum(l[0:1] * l[1:2], axis=-1, keepdims=True)
    s2 = jnp.sum(l[2:3] * l[3:4], axis=-1, keepdims=True)
    return jnp.exp(s1) - jnp.exp(s2) + lam_init


def _inproj_body(x_ref, w_ref, wvt_ref, *refs):
    q_o, kf_o, kb_o, vf_o, vt_o, gq_o, gk_o, gv_o, gr_o, za_o, zb_o, glr_o = refs[-12:]
    xb = x_ref[...].astype(bf16)

    def mm(c0, c1):
        return jnp.dot(xb, w_ref[:, c0:c1], preferred_element_type=f32)

    q_o[...] = (mm(C_Q, C_K) * (A_HEAD_DIM ** -0.5 * LOG2E)).astype(bf16)
    k = mm(C_K, C_V)
    kf_o[...] = k
    kb_o[...] = k.astype(bf16)
    vf_o[...] = mm(C_V, C_GQ)
    vt_o[...] = lax.dot_general(wvt_ref[...], xb, (((1,), (1,)), ((), ())),
                                preferred_element_type=f32).astype(bf16)
    gq_o[...] = mm(C_GQ, C_GK) * (G_DK ** -0.5)
    gk_o[...] = mm(C_GK, C_GV)
    gv_o[...] = mm(C_GV, C_GR)
    gr_o[...] = mm(C_GR, C_ZA)
    za_o[...] = mm(C_ZA, C_ZB).astype(bf16)
    zb_o[...] = mm(C_ZB, C_GLR).astype(bf16)
    glr_o[...] = mm(C_GLR, C_END)


def _inproj(x, w, wvt, layer, depth, kv_prev):
    t = x.shape[0]
    tm = _row_tile(t, (384, 256, 128))
    nb = t // tm
    outs = [(512, bf16), (512, f32), (512, bf16), (512, f32), None, (256, f32), (256, f32),
            (512, f32), (512, f32), (1024, bf16), (1024, bf16), (LANES, f32)]
    shared = (1, 3)
    out_specs, out_shape = [], []
    for n, o in enumerate(outs):
        if o is None:
            out_specs.append(pl.BlockSpec((A_WIDTH, tm), lambda i: (0, i)))
            out_shape.append(jax.ShapeDtypeStruct((A_WIDTH, t), bf16))
        elif n in shared:
            out_specs.append(pl.BlockSpec((tm, o[0]), lambda i: (layer * nb + i, 0)))
            out_shape.append(jax.ShapeDtypeStruct((depth * t, o[0]), o[1]))
        else:
            out_specs.append(pl.BlockSpec((tm, o[0]), lambda i: (i, 0)))
            out_shape.append(jax.ShapeDtypeStruct((t, o[0]), o[1]))
    return pl.pallas_call(
        _inproj_body,
        grid=(nb,),
        in_specs=[pl.BlockSpec((tm, D_MODEL), lambda i: (i, 0)),
                  pl.BlockSpec((D_MODEL, C_END), lambda i: (0, 0)),
                  pl.BlockSpec((A_WIDTH, D_MODEL), lambda i: (0, 0))]
        + [pl.BlockSpec(memory_space=pl.ANY)] * len(kv_prev),
        out_specs=out_specs,
        out_shape=out_shape,
        input_output_aliases={3 + n: pos for n, pos in enumerate(shared[:len(kv_prev)])},
        compiler_params=_params(("parallel",)),
    )(x, w, wvt, *kv_prev)


def _attn_body(qi_tab, kj_tab, diag_tab, last_tab, q_ref, k_ref, vt_ref, lam_ref, subg_ref, o_ref,
               m, a, *, tq, tk, lam_init):
    p = pl.program_id(2)
    qi = qi_tab[p]
    kj = kj_tab[p]

    @pl.when(kj == 0)
    def _():
        m[...] = jnp.full(m.shape, NEG_BIG, f32)
        a[...] = jnp.zeros(a.shape, f32)

    q = q_ref[...]
    k = k_ref[...]
    vt = jnp.concatenate([vt_ref[...], jnp.ones((ONES_ROWS, tk), bf16)], axis=0)
    lane = lax.broadcasted_iota(i32, (1, LANES), 1)
    zero = jnp.zeros_like(q)
    qq = jnp.concatenate([jnp.where(lane < A_HEAD_DIM, q, zero), jnp.where(lane >= A_HEAD_DIM, q, zero)],
                         axis=0)

    def step(masked):
        s = lax.dot_general(k, qq, (((1,), (1,)), ((), ())), preferred_element_type=f32)
        if masked:
            kpos = kj * tk + lax.broadcasted_iota(i32, (tk, 2 * tq), 0)
            qpos = qi * tq + (lax.broadcasted_iota(i32, (tk, 2 * tq), 1) & (tq - 1))
            s = jnp.where(kpos <= qpos, s, NEG_BIG)
        m_prev = m[...]
        m_new = jnp.maximum(m_prev, jnp.max(s, axis=0, keepdims=True))
        alpha = jnp.exp2(m_prev - m_new)
        pr = jnp.exp2(s - m_new).astype(bf16)
        a[...] = alpha * a[...] + jnp.dot(vt, pr, preferred_element_type=f32)
        m[...] = m_new

    def update(cols, s, vt_part):
        m_prev = m[:, cols]
        m_new = jnp.maximum(m_prev, jnp.max(s, axis=0, keepdims=True))
        alpha = jnp.exp2(m_prev - m_new)
        pr = jnp.exp2(s - m_new).astype(bf16)
        a[:, cols] = alpha * a[:, cols] + jnp.dot(vt_part, pr, preferred_element_type=f32)
        m[:, cols] = m_new

    def diagonal_square():
        h = tk // 2
        nt = (((1,), (1,)), ((), ()))
        s_lo = lax.dot_general(k[:h], qq, nt, preferred_element_type=f32)
        q_hi = jnp.concatenate([qq[h:tq], qq[tq + h:]], axis=0)
        s_hi = lax.dot_general(k[h:], q_hi, nt, preferred_element_type=f32)
        causal = lax.broadcasted_iota(i32, (h, h), 0) <= lax.broadcasted_iota(i32, (h, h), 1)
        for mp in range(2):
            lo_q = slice(mp * tq, mp * tq + h)
            hi_q = slice(mp * tq + h, (mp + 1) * tq)
            update(lo_q, jnp.where(causal, s_lo[:, lo_q], NEG_BIG), vt[:, :h])
            update(hi_q, s_lo[:, hi_q], vt[:, :h])
            update(hi_q, jnp.where(causal, s_hi[:, mp * h:(mp + 1) * h], NEG_BIG), vt[:, h:])

    @pl.when(diag_tab[p] == 1)
    def _():
        if tq == tk:
            diagonal_square()
        else:
            step(True)

    @pl.when(diag_tab[p] == 0)
    def _():
        step(False)

    @pl.when(last_tab[p] == 1)
    def _():
        lam = _lam_value(lam_ref, lam_init)
        ot = (a[:A_VDIM, :tq] / a[A_VDIM:A_VDIM + 1, :tq]
              - lam * (a[:A_VDIM, tq:] / a[A_VDIM:A_VDIM + 1, tq:]))
        ms = jnp.mean(ot * ot, axis=0, keepdims=True)
        on = ot * lax.rsqrt(ms + RMS_EPS) * subg_ref[...] * (1.0 - lam_init)
        o_ref[...] = on.T.astype(o_ref.dtype)


def _attn_prompt(q, k, vt, lam_vecs, sub_g_col, batch, seq, lam_init):
    tq = min(ATTN_TQ, seq)
    tk = min(ATTN_TK, seq)
    nq, nk = seq // tq, seq // tk
    qi_l, kj_l, dg_l, ls_l = [], [], [], []
    for qi in range(nq):
        last = ((qi + 1) * tq - 1) // tk
        for kj in range(last + 1):
            qi_l.append(qi)
            kj_l.append(kj)
            dg_l.append(1 if (kj + 1) * tk - 1 > qi * tq else 0)
            ls_l.append(1 if kj == last else 0)
    tabs = [jnp.asarray(t, i32) for t in (qi_l, kj_l, dg_l, ls_l)]
    n_pairs = len(qi_l)
    body = functools.partial(_attn_body, tq=tq, tk=tk, lam_init=lam_init)
    grid_spec = pltpu.PrefetchScalarGridSpec(
        num_scalar_prefetch=4,
        grid=(batch, A_HEADS, n_pairs),
        in_specs=[
            pl.BlockSpec((tq, LANES), lambda b, h, p, qt, kt, dt, lt: (b * nq + qt[p], h)),
            pl.BlockSpec((tk, LANES), lambda b, h, p, qt, kt, dt, lt: (b * nk + kt[p], h)),
            pl.BlockSpec((A_VDIM, tk), lambda b, h, p, qt, kt, dt, lt: (h, b * nk + kt[p])),
            pl.BlockSpec((4, A_HEAD_DIM), lambda b, h, p, *_: (0, 0)),
            pl.BlockSpec((A_VDIM, 1), lambda b, h, p, *_: (0, 0)),
        ],
        out_specs=pl.BlockSpec((tq, LANES), lambda b, h, p, qt, kt, dt, lt: (b * nq + qt[p], h)),
        scratch_shapes=[pltpu.VMEM((1, 2 * tq), f32), pltpu.VMEM((A_VDIM + ONES_ROWS, 2 * tq), f32)],
    )
    return pl.pallas_call(
        body,
        grid_spec=grid_spec,
        out_shape=jax.ShapeDtypeStruct((batch * seq, A_WIDTH), bf16),
        compiler_params=_params(("parallel", "parallel", "arbitrary")),
    )(*tabs, q, k, vt, lam_vecs, sub_g_col)


def _decode_body(pt_ref, q_ref, kn_ref, vn_ref, lam_ref, subg_ref, *refs, n_pages, lam_init):
    k_refs = refs[:n_pages]
    v_refs = refs[n_pages:2 * n_pages]
    o_ref = refs[2 * n_pages]
    kbuf, vbuf = refs[2 * n_pages + 1:]
    rows_pg = PAGE_SIZE * A_HEADS
    past = n_pages * rows_pg
    tail = 16
    n_col = past + tail

    for p in range(n_pages):
        kbuf[p * rows_pg:(p + 1) * rows_pg, :] = k_refs[p][...].astype(bf16)
        vbuf[p * rows_pg:(p + 1) * rows_pg, :] = v_refs[p][...].astype(bf16)
    kbuf[past:, :] = kn_ref[0].astype(bf16)
    vbuf[past:, :] = vn_ref[0].astype(bf16)

    qrow = q_ref[0].astype(f32)
    row = lax.broadcasted_iota(i32, (16, LANES), 0)
    lane = lax.broadcasted_iota(i32, (16, LANES), 1)
    qmat = jnp.zeros((16, LANES), f32)
    for h in range(A_HEADS):
        qh = jnp.broadcast_to(qrow[:, h * LANES:(h + 1) * LANES], (16, LANES))
        sel = ((row >> 1) == h) & ((lane >= A_HEAD_DIM) == ((row & 1) == 1))
        qmat = jnp.where(sel, qh, qmat)
    s = lax.dot_general(qmat.astype(bf16), kbuf[...], (((1,), (1,)), ((), ())), preferred_element_type=f32)
    srow = lax.broadcasted_iota(i32, (16, n_col), 0)
    scol = lax.broadcasted_iota(i32, (16, n_col), 1)
    valid = ((scol & (A_HEADS - 1)) == (srow >> 1)) & (scol < past + A_HEADS) & (srow < 2 * A_HEADS)
    s = jnp.where(valid, s, NEG_BIG)
    m = jnp.max(s, axis=-1, keepdims=True)
    pr = jnp.where(valid, jnp.exp2(s - m), 0.0)
    den = jnp.maximum(jnp.sum(pr, axis=-1, keepdims=True), 1e-30)
    pn = pr / den
    o8 = jnp.dot(pn.astype(bf16), vbuf[...], preferred_element_type=f32)
    lam = _lam_value(lam_ref, lam_init)
    for h in range(A_HEADS):
        o = o8[2 * h:2 * h + 1, :] - lam * o8[2 * h + 1:2 * h + 2, :]
        o_ref[0, :, h * LANES:(h + 1) * LANES] = (
            _rms_norm(o, subg_ref[...]) * (1.0 - lam_init)).astype(o_ref.dtype)


def _attn_decode(q_s, k_new, v_new, lam_vecs, sub_g, cache_k4, cache_v4, page_table, layer, lam_init):
    bd, n_pages = page_table.shape
    rows_pg = PAGE_SIZE * A_HEADS
    body = functools.partial(_decode_body, n_pages=n_pages, lam_init=lam_init)

    def page_spec(p):
        return pl.BlockSpec((None, None, rows_pg, LANES),
                            lambda b, pt, p=p: (layer, pt[b * n_pages + p], 0, 0))

    grid_spec = pltpu.PrefetchScalarGridSpec(
        num_scalar_prefetch=1,
        grid=(bd,),
        in_specs=[pl.BlockSpec((1, 1, A_WIDTH), lambda b, pt: (b, 0, 0)),
                  pl.BlockSpec((1, 16, LANES), lambda b, pt: (b, 0, 0)),
                  pl.BlockSpec((1, 16, LANES), lambda b, pt: (b, 0, 0)),
                  pl.BlockSpec((4, A_HEAD_DIM), lambda b, pt: (0, 0)),
                  pl.BlockSpec((1, A_VDIM), lambda b, pt: (0, 0))]
        + [page_spec(p) for p in range(n_pages)] * 2,
        out_specs=pl.BlockSpec((1, 1, A_WIDTH), lambda b, pt: (b, 0, 0)),
        scratch_shapes=[pltpu.VMEM((n_pages * rows_pg + 16, LANES), bf16),
                        pltpu.VMEM((n_pages * rows_pg + 16, LANES), bf16)],
    )
    return pl.pallas_call(
        body,
        grid_spec=grid_spec,
        out_shape=jax.ShapeDtypeStruct((bd, 1, A_WIDTH), bf16),
        compiler_params=_params(("arbitrary",)),
    )(page_table.reshape(-1), q_s, k_new, v_new, lam_vecs, sub_g,
      *([cache_k4] * n_pages), *([cache_v4] * n_pages))


def _log_decay(glr, wa2, ba):
    z = jnp.dot(glr, wa2, preferred_element_type=f32, precision=lax.Precision.HIGHEST) + ba
    return (jnp.minimum(z, 0.0) - jnp.log(1.0 + jnp.exp(-jnp.abs(z)))) * (1.0 / G_TAU)


def _gla_finish(o, gr, g):
    return _rms_norm(o, g) * (gr * _sigmoid(gr))


def _gla_prompt_body(q_ref, k_ref, v_ref, glr_ref, gr_ref, wa2_ref, ba_ref, g_ref, o_ref, s_ref, state, tmp,
                     *, c):
    ci = pl.program_id(1)
    nsub = c // GLA_SUB

    @pl.when(ci == 0)
    def _():
        state[...] = jnp.zeros(state.shape, f32)

    la = _log_decay(glr_ref[...], wa2_ref[...], ba_ref[...])
    ri = lax.broadcasted_iota(i32, (c, c), 0)
    cj = lax.broadcasted_iota(i32, (c, c), 1)
    b = jnp.dot((ri >= cj).astype(f32), la, preferred_element_type=f32, precision=lax.Precision.HIGHEST)
    b_last = b[c - 1:c, :]
    q = q_ref[...]
    k = k_ref[...]
    v = v_ref[...]
    vb = v.astype(bf16)
    lane = lax.broadcasted_iota(i32, (1, G_K_WIDTH), 1)
    heads = [(lane >> 6) == h for h in range(G_HEADS)]

    def stack_heads(x):
        return jnp.concatenate([jnp.where(m, x, 0.0) for m in heads], axis=0).astype(bf16)

    sub_i = lax.broadcasted_iota(i32, (GLA_SUB, G_K_WIDTH), 0)
    for blk in range(nsub):
        r0 = blk * GLA_SUB
        q_b = q[r0:r0 + GLA_SUB]
        b_b = b[r0:r0 + GLA_SUB]
        for j in range(GLA_SUB):
            w = jnp.where(sub_i >= j, jnp.exp(jnp.minimum(b_b - b_b[j:j + 1], 0.0)), 0.0)
            t0 = (r0 + j) * GLA_SUB
            tmp[t0:t0 + GLA_SUB, :] = (q_b * w * k[r0 + j:r0 + j + 1]).astype(bf16)
    si = lax.broadcasted_iota(i32, (G_K_WIDTH, G_V_WIDTH), 0)
    sj = lax.broadcasted_iota(i32, (G_K_WIDTH, G_V_WIDTH), 1)
    seg = ((si >> 6) == (sj >> 7)).astype(bf16)
    pair = jnp.dot(tmp[...], seg, preferred_element_type=f32)
    o_rows = []
    for blk in range(nsub):
        r0 = blk * GLA_SUB
        acc = jnp.zeros((GLA_SUB, G_V_WIDTH), f32)
        for j in range(GLA_SUB):
            t0 = (r0 + j) * GLA_SUB
            acc = acc + pair[t0:t0 + GLA_SUB, :] * v[r0 + j:r0 + j + 1]
        o_rows.append(acc)
    o_diag = jnp.concatenate(o_rows, axis=0)

    att_rows = [[jnp.zeros((GLA_SUB, c), f32)] for _ in range(G_HEADS)]
    col = lax.broadcasted_iota(i32, (G_HEADS * GLA_SUB, c), 1)
    for blk in range(1, nsub):
        r0 = blk * GLA_SUB
        ref = b[r0 - 1:r0]
        q_b = q[r0:r0 + GLA_SUB] * jnp.exp(b[r0:r0 + GLA_SUB] - ref)
        k_b = (k * jnp.exp(jnp.minimum(ref - b, 0.0))).astype(bf16)
        a = lax.dot_general(stack_heads(q_b), k_b, (((1,), (1,)), ((), ())), preferred_element_type=f32)
        a = jnp.where(col < r0, a, 0.0)
        for h in range(G_HEADS):
            att_rows[h].append(a[h * GLA_SUB:(h + 1) * GLA_SUB])

    st = state[...]
    o_inter = jnp.dot(stack_heads(q * jnp.exp(b)), st.astype(bf16), preferred_element_type=f32)
    kdt = (k * jnp.exp(b_last - b)).T.astype(bf16)
    upd = jnp.dot(kdt, vb, preferred_element_type=f32)
    dec_col = jnp.exp(jnp.broadcast_to(b_last, (8, G_K_WIDTH)).T[:, 0:1])
    for h in range(G_HEADS):
        cols = slice(h * G_DV, (h + 1) * G_DV)
        att = jnp.concatenate(att_rows[h], axis=0).astype(bf16)
        o = (o_diag[:, cols] + jnp.dot(att, vb[:, cols], preferred_element_type=f32)
             + o_inter[h * c:(h + 1) * c])
        o_ref[:, cols] = _gla_finish(o, gr_ref[:, cols], g_ref[...]).astype(o_ref.dtype)
        r0 = h * G_DK
        state[r0:r0 + G_DK, :] = dec_col[r0:r0 + G_DK] * st[r0:r0 + G_DK, :] + upd[r0:r0 + G_DK, cols]

    @pl.when(ci == pl.num_programs(1) - 1)
    def _():
        s_ref[0] = state[...]


def _gla_prompt(gq, gk, gv, glr, gr, wa2, ba, gla_g, batch, seq):
    c = math.gcd(seq, GLA_CHUNK)
    n = seq // c
    body = functools.partial(_gla_prompt_body, c=c)

    def tok(width):
        return pl.BlockSpec((c, width), lambda b, i: (b * n + i, 0))

    def whole(shape):
        return pl.BlockSpec(shape, lambda b, i: (0,) * len(shape))

    return pl.pallas_call(
        body,
        grid=(batch, n),
        in_specs=[tok(G_K_WIDTH), tok(G_K_WIDTH), tok(G_V_WIDTH), tok(LANES), tok(G_V_WIDTH),
                  whole((LANES, G_K_WIDTH)), whole((1, G_K_WIDTH)), whole((1, G_DV))],
        out_specs=[tok(G_V_WIDTH),
                   pl.BlockSpec((1, G_K_WIDTH, G_DV), lambda b, i: (b, 0, 0))],
        out_shape=[jax.ShapeDtypeStruct((batch * seq, G_V_WIDTH), bf16),
                   jax.ShapeDtypeStruct((batch, G_K_WIDTH, G_DV), f32)],
        scratch_shapes=[pltpu.VMEM((G_K_WIDTH, G_DV), f32),
                        pltpu.VMEM((c * GLA_SUB, G_K_WIDTH), bf16)],
        compiler_params=_params(("parallel", "arbitrary")),
    )(gq, gk, gv, glr, gr, wa2, ba, gla_g)


def _gla_sample_body(q_ref, k_ref, v_ref, glr_ref, gr_ref, s_ref, wa2_ref, ba_ref, g_ref, o_ref, sn_ref,
                     *, group):
    ri = lax.broadcasted_iota(i32, (3 * G_DK, G_K_WIDTH), 0)
    li = lax.broadcasted_iota(i32, (3 * G_DK, G_K_WIDTH), 1)
    diag = (ri & (G_DK - 1)) == (li & (G_DK - 1))
    si = lax.broadcasted_iota(i32, (G_K_WIDTH, G_V_WIDTH), 0)
    sj = lax.broadcasted_iota(i32, (G_K_WIDTH, G_V_WIDTH), 1)
    seg = ((si >> 6) == (sj >> 7)).astype(f32)
    for n in range(group):
        la = _log_decay(glr_ref[n], wa2_ref[...], ba_ref[...])
        rows = jnp.concatenate([jnp.broadcast_to(jnp.exp(la), (G_DK, G_K_WIDTH)),
                                jnp.broadcast_to(k_ref[n], (G_DK, G_K_WIDTH)),
                                jnp.broadcast_to(q_ref[n], (G_DK, G_K_WIDTH))], axis=0)
        picked = jnp.where(diag, rows, 0.0)
        cols = jnp.dot(picked, seg, preferred_element_type=f32, precision=lax.Precision.HIGHEST)
        v = v_ref[n]
        for h in range(G_HEADS):
            sl = slice(h * G_DV, (h + 1) * G_DV)
            a_c = cols[0:G_DK, sl]
            k_c = cols[G_DK:2 * G_DK, sl]
            q_c = cols[2 * G_DK:3 * G_DK, sl]
            s_new = a_c * s_ref[n, h] + k_c * v[:, sl]
            sn_ref[n, h] = s_new
            o = jnp.sum(q_c * s_new, axis=0, keepdims=True)
            o_ref[n, :, sl] = _gla_finish(o, gr_ref[n][:, sl], g_ref[...]).astype(o_ref.dtype)


def _gla_sample(gq, gk, gv, glr, gr, s0, wa2, ba, gla_g):
    bd = gq.shape[0]
    group = math.gcd(bd, SUBLANES)

    def vec(width):
        return pl.BlockSpec((group, 1, width), lambda b: (b, 0, 0))

    def whole(shape):
        return pl.BlockSpec(shape, lambda b: (0,) * len(shape))

    st = pl.BlockSpec((group, G_HEADS, G_DK, G_DV), lambda b: (b, 0, 0, 0))
    return pl.pallas_call(
        functools.partial(_gla_sample_body, group=group),
        grid=(bd // group,),
        in_specs=[vec(G_K_WIDTH), vec(G_K_WIDTH), vec(G_V_WIDTH), vec(LANES), vec(G_V_WIDTH), st,
                  whole((LANES, G_K_WIDTH)), whole((1, G_K_WIDTH)), whole((1, G_DV))],
        out_specs=[vec(G_V_WIDTH), st],
        out_shape=[jax.ShapeDtypeStruct((bd, 1, G_V_WIDTH), bf16),
                   jax.ShapeDtypeStruct((bd, G_HEADS, G_DK, G_DV), f32)],
        compiler_params=_params(("parallel",)),
    )(gq, gk, gv, glr, gr, s0, wa2, ba, gla_g)


def _postmix_body(oa_ref, og_ref, za_ref, zb_ref, x_ref, wpa_ref, wpb_ref, wout_ref, g_ref, b_ref,
                  x1_o, xp_o, *, alpha):
    ya = jnp.dot(oa_ref[...], wpa_ref[...], preferred_element_type=f32)
    yb = jnp.dot(og_ref[...], wpb_ref[...], preferred_element_type=f32)
    merged = _sigmoid(za_ref[...].astype(f32)) * ya + _sigmoid(zb_ref[...].astype(f32)) * yb
    mix = jnp.dot(merged.astype(bf16), wout_ref[...], preferred_element_type=f32)
    x1 = _layer_norm(alpha * x_ref[...] + mix, g_ref[...], b_ref[...])
    x1_o[...] = x1
    xp_o[...] = _pack_rows(x1)


def _postmix(oa, og, za, zb, x, wpa, wpb, wout, g, b, alpha):
    t = x.shape[0]
    tm = _row_tile(t, (384, 256, 128, 64, 32, 16, 8))

    def tok(width):
        return pl.BlockSpec((tm, width), lambda i: (i, 0))

    def whole(shape):
        return pl.BlockSpec(shape, lambda i: (0,) * len(shape))

    return pl.pallas_call(
        functools.partial(_postmix_body, alpha=alpha),
        grid=(t // tm,),
        in_specs=[tok(A_WIDTH), tok(G_V_WIDTH), tok(D_MODEL), tok(D_MODEL), tok(D_MODEL),
                  whole((A_WIDTH, D_MODEL)), whole((G_V_WIDTH, D_MODEL)), whole((D_MODEL, D_MODEL)),
                  whole((1, D_MODEL)), whole((1, D_MODEL))],
        out_specs=[tok(D_MODEL), tok(HALF)],
        out_shape=[jax.ShapeDtypeStruct((t, D_MODEL), f32),
                   jax.ShapeDtypeStruct((t, HALF), u32)],
        compiler_params=_params(("parallel",)),
    )(oa, og, za, zb, x, wpa, wpb, wout, g, b)


def _router_body(x1_ref, wrt_ref, bcol_ref, tri_ref, eidx_o, gate_o, rank_o, cnt_o, cnt, *, tm):
    @pl.when(pl.program_id(0) == 0)
    def _():
        cnt[...] = jnp.zeros(cnt.shape, f32)

    logits = lax.dot_general(wrt_ref[...], x1_ref[...], (((1,), (1,)), ((), ())),
                             preferred_element_type=f32, precision=lax.Precision.HIGHEST)
    scores = _sigmoid(logits)
    biased = scores + bcol_ref[...]
    gsz = N_EXPERTS // N_GROUPS
    neg_inf = -jnp.inf

    gi = lax.broadcasted_iota(i32, (gsz, tm), 0)
    segs, gscore = [], []
    for g in range(N_GROUPS):
        seg = biased[g * gsz:(g + 1) * gsz, :]
        m1 = jnp.max(seg, axis=0, keepdims=True)
        i1 = jnp.min(jnp.where(seg == m1, gi, gsz), axis=0, keepdims=True)
        m2 = jnp.max(jnp.where(gi == i1, neg_inf, seg), axis=0, keepdims=True)
        segs.append(seg)
        gscore.append(m1 + m2)
    parts = []
    for g in range(N_GROUPS):
        beat = jnp.zeros((1, tm), i32)
        for o in range(N_GROUPS):
            if o != g:
                wins = (gscore[o] > gscore[g]) | ((gscore[o] == gscore[g]) & (o < g))
                beat = beat + wins.astype(i32)
        parts.append(jnp.where(beat < TOPK_GROUPS, segs[g], neg_inf))
    masked = jnp.concatenate(parts, axis=0)

    ei = lax.broadcasted_iota(i32, (N_EXPERTS, tm), 0)
    sel_rows, idx_rows = [], []
    chosen = jnp.zeros((N_EXPERTS, tm), f32)
    for _ in range(TOP_K):
        m = jnp.max(masked, axis=0, keepdims=True)
        idx = jnp.min(jnp.where(masked == m, ei, N_EXPERTS), axis=0, keepdims=True)
        hit = ei == idx
        sel_rows.append(jnp.sum(jnp.where(hit, scores, 0.0), axis=0, keepdims=True))
        idx_rows.append(idx)
        chosen = jnp.where(hit, 1.0, chosen)
        masked = jnp.where(hit, neg_inf, masked)
    s_sel = jnp.concatenate(sel_rows, axis=0)
    gate_o[...] = s_sel / jnp.sum(s_sel, axis=0, keepdims=True) * ROUTED_SCALE
    eidx_o[...] = jnp.concatenate(idx_rows, axis=0)
    before = jnp.dot(chosen.astype(bf16), tri_ref[...], preferred_element_type=f32) + cnt[...]
    rank_rows = [jnp.sum(jnp.where(ei == idx, before, 0.0), axis=0, keepdims=True) for idx in idx_rows]
    rank_o[...] = jnp.concatenate(rank_rows, axis=0).astype(i32)
    cnt[...] = cnt[...] + jnp.sum(chosen, axis=1, keepdims=True)
    cnt_o[...] = cnt[...]


def _router(x1, wrt, b_col):
    t = x1.shape[0]
    tm = _row_tile(t, (384, 256, 128))
    tri = (jnp.arange(tm)[:, None] < jnp.arange(tm)[None, :]).astype(bf16)

    def tokcol(dt):
        return pl.BlockSpec((TOP_K, tm), lambda i: (0, i)), jax.ShapeDtypeStruct((TOP_K, t), dt)

    def whole(shape):
        return pl.BlockSpec(shape, lambda i: (0,) * len(shape))

    specs, shapes = zip(tokcol(i32), tokcol(f32), tokcol(i32),
                        (whole((N_EXPERTS, 1)), jax.ShapeDtypeStruct((N_EXPERTS, 1), f32)))
    return pl.pallas_call(
        functools.partial(_router_body, tm=tm),
        grid=(t // tm,),
        in_specs=[pl.BlockSpec((tm, D_MODEL), lambda i: (i, 0)), whole((N_EXPERTS, D_MODEL)),
                  whole((N_EXPERTS, 1)), whole((tm, tm))],
        out_specs=list(specs),
        out_shape=list(shapes),
        scratch_shapes=[pltpu.VMEM((N_EXPERTS, 1), f32)],
        compiler_params=_params(("arbitrary",)),
    )(x1, wrt, b_col, tri)


def _dispatch_body(dest_ref, pad0_ref, pad1_ref, x_ref, xs_out, zbuf, sem, zsem, *, tm):

    @pl.when(pl.program_id(0) == 0)
    def _():
        zbuf[...] = jnp.zeros(zbuf.shape, u32)

        def zero_row(row):
            return pltpu.make_async_copy(zbuf.at[pl.ds(0, 1)], xs_out.at[pl.ds(row, 1)], zsem)

        def zero_group(row):
            return pltpu.make_async_copy(zbuf, xs_out.at[pl.ds(pl.multiple_of(row, SUBLANES), SUBLANES)], zsem)

        def per_expert(e, waiting):
            p0 = pad0_ref[e]
            p1 = pad1_ref[e]
            head = jnp.minimum((-p0) & (SUBLANES - 1), p1 - p0)
            groups = lax.shift_right_logical(p1 - p0 - head, 3)

            def rows(r, c):
                if waiting:
                    zero_row(0).wait()
                else:
                    zero_row(p0 + r).start()
                return c

            def grps(g, c):
                if waiting:
                    zero_group(0).wait()
                else:
                    zero_group(p0 + head + g * SUBLANES).start()
                return c

            lax.fori_loop(0, head, rows, 0)
            lax.fori_loop(0, groups, grps, 0)

        lax.fori_loop(0, N_EXPERTS, lambda e, c: (per_expert(e, False), c)[1], 0)
        lax.fori_loop(0, N_EXPERTS, lambda e, c: (per_expert(e, True), c)[1], 0)

    def copy(grp, sub, dst_row):
        return pltpu.make_async_copy(x_ref.at[grp, pl.ds(sub, 1)], xs_out.at[pl.ds(dst_row, 1)], sem)

    def issue(grp, carry):
        for sub in range(SUBLANES):
            for kk in range(TOP_K):
                copy(grp, sub, dest_ref[(grp * SUBLANES + sub) * TOP_K + kk]).start(priority=kk % 2)
        return carry

    lax.fori_loop(0, tm // SUBLANES, issue, 0)

    def drain(grp, carry):
        for _ in range(SUBLANES * TOP_K):
            copy(0, 0, 0).wait()
        return carry

    lax.fori_loop(0, tm // SUBLANES, drain, 0)


def _dispatch(xp, dest, pad0, pad1, n_rows):
    t = xp.shape[0]
    tm = _row_tile(t, (384, 256, 128, 64, 32, 16, 8))
    whole = pl.BlockSpec((N_EXPERTS,), lambda i: (0,), memory_space=pltpu.SMEM)
    return pl.pallas_call(
        functools.partial(_dispatch_body, tm=tm),
        grid=(t // tm,),
        in_specs=[pl.BlockSpec((tm * TOP_K,), lambda i: (i,), memory_space=pltpu.SMEM), whole, whole,
                  pl.BlockSpec((tm // SUBLANES, SUBLANES, HALF), lambda i: (i, 0, 0))],
        out_specs=pl.BlockSpec(memory_space=pl.ANY),
        out_shape=jax.ShapeDtypeStruct((n_rows, HALF), u32),
        scratch_shapes=[pltpu.VMEM((SUBLANES, HALF), u32), pltpu.SemaphoreType.DMA(()),
                        pltpu.SemaphoreType.DMA(())],
        compiler_params=_params(("arbitrary",)),
    )(dest, pad0, pad1, xp.reshape(t // SUBLANES, SUBLANES, HALF))


def _expert_weight_copies(seq_ref, w_hbm, w_buf, sems, seq_idx, *, layer):
    e = seq_ref[seq_idx]
    slot = lax.rem(seq_idx, WEIGHT_SLOTS)
    return [pltpu.make_async_copy(w_hbm[n].at[layer, e], w_buf[n].at[slot], sems.at[slot, n]) for n in range(3)]


def _expert_body(bs_ref, seq_ref, ns_ref, nu_ref, xs_ref, wg_hbm, wu_hbm, wd_hbm, ys_ref,
                 wgf, wuf, wdf, sems, wgb, wub, wdb, *, layer):
    i = pl.program_id(0)
    j = bs_ref[i]
    copies = functools.partial(_expert_weight_copies, seq_ref, (wg_hbm, wu_hbm, wd_hbm), (wgf, wuf, wdf),
                               sems, layer=layer)

    @pl.when(i == 0)
    def _():
        for c in copies(0):
            c.start()

        @pl.when(ns_ref[0] > 1)
        def _():
            for c in copies(1):
                c.start()

    @pl.when((i == 0) | (j != bs_ref[jnp.maximum(i - 1, 0)]))
    def _():
        for c in copies(j):
            c.wait()
        slot = lax.rem(j, WEIGHT_SLOTS)
        wgb[...] = wgf[slot].astype(bf16)
        wub[...] = wuf[slot].astype(bf16)
        wdb[...] = wdf[slot].astype(bf16)

        @pl.when(j + 2 < ns_ref[0])
        def _():
            for c in copies(j + 2):
                c.start()

    @pl.when(i < nu_ref[0])
    def _():
        lo, hi = _unpack_rows(xs_ref[...])
        lo = lo.astype(bf16)
        hi = hi.astype(bf16)
        g = (jnp.dot(lo, wgb[:HALF, :], preferred_element_type=f32)
             + jnp.dot(hi, wgb[HALF:, :], preferred_element_type=f32))
        u = (jnp.dot(lo, wub[:HALF, :], preferred_element_type=f32)
             + jnp.dot(hi, wub[HALF:, :], preferred_element_type=f32))
        hdn = (g * _sigmoid(g) * u).astype(bf16)
        ys_ref[...] = _pack_rows(jnp.dot(hdn, wdb[...], preferred_element_type=f32))

    @pl.when(i >= nu_ref[0])
    def _():
        ys_ref[...] = jnp.zeros(ys_ref.shape, u32)


def _experts(xs, blk_seq, seq_exp, n_seq, n_used, w_gate, w_up, w_down, layer):
    n_rows = xs.shape[0]
    n_blocks = n_rows // EXPERT_BLOCK
    hbm = pl.BlockSpec(memory_space=pl.ANY)
    grid_spec = pltpu.PrefetchScalarGridSpec(
        num_scalar_prefetch=4,
        grid=(n_blocks,),
        in_specs=[pl.BlockSpec((EXPERT_BLOCK, HALF), lambda i, bs, sq, ns, nu: (jnp.minimum(i, nu[0] - 1), 0)),
                  hbm, hbm, hbm],
        out_specs=pl.BlockSpec((EXPERT_BLOCK, HALF), lambda i, bs, sq, ns, nu: (i, 0)),
        scratch_shapes=[pltpu.VMEM((WEIGHT_SLOTS, D_MODEL, D_EXPERT), f32),
                        pltpu.VMEM((WEIGHT_SLOTS, D_MODEL, D_EXPERT), f32),
                        pltpu.VMEM((WEIGHT_SLOTS, D_EXPERT, D_MODEL), f32),
                        pltpu.SemaphoreType.DMA((WEIGHT_SLOTS, 3)),
                        pltpu.VMEM((D_MODEL, D_EXPERT), bf16), pltpu.VMEM((D_MODEL, D_EXPERT), bf16),
                        pltpu.VMEM((D_EXPERT, D_MODEL), bf16)],
    )
    return pl.pallas_call(
        functools.partial(_expert_body, layer=layer),
        grid_spec=grid_spec,
        out_shape=jax.ShapeDtypeStruct((n_rows, HALF), u32),
        compiler_params=_params(("arbitrary",)),
    )(blk_seq, seq_exp, n_seq, n_used, xs, w_gate, w_up, w_down)


def _combine_body(dest_ref, gate_ref, x1_ref, xp_ref, ys_hbm, wsg_ref, wsu_ref, wsd_ref, g_ref, b_ref,
                  o_ref, buf, sem, *, tm, alpha):
    def copy(src_row, kk, grp, sub):
        return pltpu.make_async_copy(ys_hbm.at[pl.ds(src_row, 1)], buf.at[kk, grp, pl.ds(sub, 1)], sem)

    def issue(grp, carry):
        for sub in range(SUBLANES):
            for kk in range(TOP_K):
                copy(dest_ref[(grp * SUBLANES + sub) * TOP_K + kk], kk, grp, sub).start(priority=kk % 2)
        return carry

    lax.fori_loop(0, tm // SUBLANES, issue, 0)

    lo, hi = _unpack_rows(xp_ref[...])
    lo = lo.astype(bf16)
    hi = hi.astype(bf16)
    sg = (jnp.dot(lo, wsg_ref[:HALF, :], preferred_element_type=f32)
          + jnp.dot(hi, wsg_ref[HALF:, :], preferred_element_type=f32))
    su = (jnp.dot(lo, wsu_ref[:HALF, :], preferred_element_type=f32)
          + jnp.dot(hi, wsu_ref[HALF:, :], preferred_element_type=f32))
    shared = jnp.dot((sg * _sigmoid(sg) * su).astype(bf16), wsd_ref[...], preferred_element_type=f32)

    def drain(grp, carry):
        for _ in range(SUBLANES * TOP_K):
            copy(0, 0, 0, 0).wait()
        return carry

    lax.fori_loop(0, tm // SUBLANES, drain, 0)

    gates = gate_ref[...]
    acc_lo = jnp.zeros((tm, HALF), f32)
    acc_hi = jnp.zeros((tm, HALF), f32)
    for kk in range(TOP_K):
        ylo, yhi = _unpack_rows(buf[kk].reshape(tm, HALF))
        gk = gates[:, kk:kk + 1]
        acc_lo = acc_lo + gk * ylo
        acc_hi = acc_hi + gk * yhi
    moe = jnp.concatenate([acc_lo, acc_hi], axis=-1) + shared
    o_ref[...] = _layer_norm(alpha * x1_ref[...] + moe, g_ref[...], b_ref[...])


def _combine(dest, gates, x1, xp, ys, wsg, wsu, wsd, g, b, alpha):
    t = x1.shape[0]
    tm = _row_tile(t, (384, 256, 128, 64, 32, 16, 8))

    def tok(width):
        return pl.BlockSpec((tm, width), lambda i: (i, 0))

    def whole(shape):
        return pl.BlockSpec(shape, lambda i: (0,) * len(shape))

    return pl.pallas_call(
        functools.partial(_combine_body, tm=tm, alpha=alpha),
        grid=(t // tm,),
        in_specs=[pl.BlockSpec((tm * TOP_K,), lambda i: (i,), memory_space=pltpu.SMEM),
                  tok(TOP_K), tok(D_MODEL), tok(HALF),
                  pl.BlockSpec(memory_space=pl.ANY),
                  whole((D_MODEL, D_EXPERT)), whole((D_MODEL, D_EXPERT)), whole((D_EXPERT, D_MODEL)),
                  whole((1, D_MODEL)), whole((1, D_MODEL))],
        out_specs=tok(D_MODEL),
        out_shape=jax.ShapeDtypeStruct((t, D_MODEL), f32),
        scratch_shapes=[pltpu.VMEM((TOP_K, tm // SUBLANES, SUBLANES, HALF), u32),
                        pltpu.SemaphoreType.DMA(())],
        compiler_params=_params(("arbitrary",)),
    )(dest, gates, x1, xp, ys, wsg, wsu, wsd, g, b)


def _dest_body(eidx_ref, rank_ref, ps_ref, dest_o, *, tm):
    ei = lax.broadcasted_iota(i32, (N_EXPERTS, tm), 0)
    ps = ps_ref[...]
    rows = [jnp.sum(jnp.where(ei == eidx_ref[kk:kk + 1, :], ps, 0.0), axis=0, keepdims=True)
            for kk in range(TOP_K)]
    dest_o[...] = jnp.concatenate(rows, axis=0).astype(i32) + rank_ref[...]


def _dest_rows(eidx_t, rank_t, pad_start):
    t = eidx_t.shape[1]
    tm = _row_tile(t, (384, 256, 128))
    blk = pl.BlockSpec((TOP_K, tm), lambda i: (0, i))
    return pl.pallas_call(
        functools.partial(_dest_body, tm=tm),
        grid=(t // tm,),
        in_specs=[blk, blk, pl.BlockSpec((N_EXPERTS, 1), lambda i: (0, 0))],
        out_specs=blk,
        out_shape=jax.ShapeDtypeStruct((TOP_K, t), i32),
        compiler_params=_params(("parallel",)),
    )(eidx_t, rank_t, pad_start.astype(f32).reshape(N_EXPERTS, 1))


def _layout(eidx_t, rank_t, counts):
    t = eidx_t.shape[1]
    blk = EXPERT_BLOCK
    padded = (counts + blk - 1) // blk * blk
    pad_end = jnp.cumsum(padded)
    pad_start = pad_end - padded
    dest = _dest_rows(eidx_t, rank_t, pad_start)
    n_rows = -(-(t * TOP_K + N_EXPERTS * (blk - 1)) // blk) * blk
    n_blocks = n_rows // blk
    n_used = (pad_end[-1] // blk).astype(i32)
    first_row = jnp.minimum(jnp.arange(n_blocks), n_used - 1) * blk
    blk_exp = jnp.sum(pad_end[None, :] <= first_row[:, None], axis=1).astype(i32)
    blk_exp = jnp.minimum(blk_exp, N_EXPERTS - 1)
    used = counts > 0
    seq_of = jnp.cumsum(used.astype(i32)) - 1
    ids = jnp.arange(N_EXPERTS, dtype=i32)
    seq_exp = jnp.sum(jnp.where(used[None, :] & (seq_of[None, :] == ids[:, None]), ids[None, :], 0), axis=1)
    blk_seq = jnp.sum(jnp.where(blk_exp[:, None] == ids[None, :], seq_of[None, :], 0), axis=1)
    n_seq = jnp.sum(used.astype(i32))
    pads = ((pad_start + counts).astype(i32), pad_end.astype(i32))
    return dest, pads, blk_seq.astype(i32), seq_exp.astype(i32), n_seq.reshape(1), n_used.reshape(1), n_rows


def kernel(x_prompt, x_sample, cache_k, cache_v, state_gla, page_table, w_in, w_a2, b_a, lam_q1, lam_k1,
           lam_q2, lam_k2, sub_g, gla_g, w_pa, w_pb, w_out, ln1_g, ln1_b, w_router, b_router, w_gate, w_up,
           w_down, ws_gate, ws_up, ws_down, ln2_g, ln2_b):
    depth = w_in.shape[0]
    batch, seq, _ = x_prompt.shape
    bd = x_sample.shape[0]
    tp = batch * seq
    alpha = (2 * depth) ** 0.25
    n_pool = cache_k.shape[1]
    cache_k4 = cache_k.reshape(depth, n_pool, PAGE_SIZE * A_HEADS, 2 * A_HEAD_DIM)
    cache_v4 = cache_v.reshape(depth, n_pool, PAGE_SIZE * A_HEADS, A_VDIM)

    x = jnp.concatenate([x_prompt.reshape(tp, D_MODEL), x_sample.reshape(bd, D_MODEL)], axis=0)
    t_all = tp + bd
    kv_all = ()
    sp_l, ss_l = [], []
    for l in range(depth):
        lam_init = 0.8 - 0.6 * math.exp(-0.3 * l)
        wl = w_in[l]
        w_re = jnp.concatenate([wl[:, :3072], wl[:, 3088:5136], wl[:, 3072:3088],
                                jnp.zeros((D_MODEL, LANES - G_GATE_RANK), f32)], axis=1).astype(bf16)
        wa2 = jnp.concatenate([w_a2[l], jnp.zeros((LANES - G_GATE_RANK, G_K_WIDTH), f32)], axis=0)
        ba = b_a[l].reshape(1, G_K_WIDTH)
        lam_vecs = jnp.stack([lam_q1[l], lam_k1[l], lam_q2[l], lam_k2[l]]).astype(f32)
        subg = sub_g[l].reshape(1, A_VDIM)
        glag = gla_g[l].reshape(1, G_DV)

        wvt = wl[:, C_V:C_GQ].T.astype(bf16)
        q, kf_all, kb, vf_all, vt, gq, gk, gv, gr, za, zb, glr = _inproj(x, w_re, wvt, l, depth, kv_all)
        kv_all = (kf_all, vf_all)
        kf = kf_all[l * t_all + tp:(l + 1) * t_all]
        vf = vf_all[l * t_all + tp:(l + 1) * t_all]

        oa_p = _attn_prompt(q, kb, vt, lam_vecs, sub_g[l].reshape(A_VDIM, 1), batch, seq, lam_init)
        tail_pad = ((0, 0), (0, 16 - A_HEADS), (0, 0))
        oa_s = _attn_decode(q[tp:].reshape(bd, 1, A_WIDTH),
                            jnp.pad(kf.reshape(bd, A_HEADS, 2 * A_HEAD_DIM), tail_pad),
                            jnp.pad(vf.reshape(bd, A_HEADS, A_VDIM), tail_pad),
                            lam_vecs, subg, cache_k4, cache_v4, page_table, l, lam_init)
        og_p, s_p = _gla_prompt(gq, gk, gv, glr, gr, wa2, ba, glag, batch, seq)
        og_s, s_s = _gla_sample(gq[tp:].reshape(bd, 1, -1), gk[tp:].reshape(bd, 1, -1),
                                gv[tp:].reshape(bd, 1, -1), glr[tp:].reshape(bd, 1, -1),
                                gr[tp:].reshape(bd, 1, -1), state_gla[l], wa2, ba, glag)
        oa = jnp.concatenate([oa_p, oa_s.reshape(bd, A_WIDTH)], axis=0)
        og = jnp.concatenate([og_p, og_s.reshape(bd, G_V_WIDTH)], axis=0)

        x1, xp = _postmix(oa, og, za, zb, x, w_pa[l].astype(bf16), w_pb[l].astype(bf16),
                          w_out[l].astype(bf16), ln1_g[l].reshape(1, -1), ln1_b[l].reshape(1, -1), alpha)
        eidx_t, gates_t, rank_t, counts = _router(x1, w_router[l].T, b_router[l].reshape(N_EXPERTS, 1))
        dest_t, pads, blk_seq, seq_exp, n_seq, n_used, n_rows = _layout(eidx_t, rank_t,
                                                                        counts.reshape(-1).astype(i32))
        dest = dest_t.T.reshape(-1)
        xs = _dispatch(xp, dest, pads[0], pads[1], n_rows)
        ys = _experts(xs, blk_seq, seq_exp, n_seq, n_used, w_gate, w_up, w_down, l)
        x = _combine(dest, gates_t.T, x1, xp, ys, ws_gate[l].astype(bf16), ws_up[l].astype(bf16),
                     ws_down[l].astype(bf16), ln2_g[l].reshape(1, -1), ln2_b[l].reshape(1, -1), alpha)

        sp_l.append(s_p.reshape(batch, G_HEADS, G_DK, G_DV))
        ss_l.append(s_s)

    y_prompt = x[:tp].reshape(batch, seq, D_MODEL)
    y_sample = x[tp:].reshape(bd, 1, D_MODEL)
    k3 = kv_all[0].reshape(depth, t_all, A_WIDTH)
    v3 = kv_all[1].reshape(depth, t_all, A_WIDTH)
    pages = (depth, batch, seq // PAGE_SIZE, PAGE_SIZE, A_HEADS, A_VDIM)
    return (y_prompt, y_sample, k3[:, :tp].reshape(pages), v3[:, :tp].reshape(pages),
            k3[:, tp:].reshape(depth, bd, 1, A_HEADS, 2 * A_HEAD_DIM),
            v3[:, tp:].reshape(depth, bd, 1, A_HEADS, A_VDIM), jnp.stack(sp_l), jnp.stack(ss_l))
```
